```python
import jax, jax.numpy as jnp
from jax import lax
import numpy as np

D_MODEL = 1024
BATCH = 32
SEQ = 2048
DEPTH = 1

D_MIX = D_MODEL
POOL_WIDTH = D_MIX // 4
POOL_WINDOWS = (2, 4, 8, 16)
N_POOL_GROUPS = len(POOL_WINDOWS)
POOL_GROUP_DIM = POOL_WIDTH // N_POOL_GROUPS
SSD_WIDTH = D_MIX - POOL_WIDTH
SSD_HEAD_DIM = 64
SSD_HEADS = SSD_WIDTH // SSD_HEAD_DIM
SSD_GROUPS = 4
SSD_HEADS_PER_GROUP = SSD_HEADS // SSD_GROUPS
SSD_STATE = 128
SSD_CONV = 4
SSD_CHUNK = 128
SSD_CONV_DIM = SSD_WIDTH + 2 * SSD_GROUPS * SSD_STATE
IN_PROJ_DIM = POOL_WIDTH + SSD_WIDTH + SSD_CONV_DIM + SSD_HEADS
N_EXPERT_GROUPS = 4
EXPERTS_PER_GROUP = 8
N_EXPERTS = N_EXPERT_GROUPS * EXPERTS_PER_GROUP
TOP_K = 2
EXPERT_FF = 512
MOE_BLOCK = 128
NORM_EPS = 1e-6

kernel_name = "hymba_pool_ssd_hmoe_adaln"


def rms_norm(x, eps=NORM_EPS):
    xf = x.astype(jnp.float32)
    y = xf * lax.rsqrt(jnp.mean(xf * xf, axis=-1, keepdims=True) + eps)
    return y.astype(x.dtype)


def modulate(x, shift, scale):
    return rms_norm(x) * (1 + scale[:, None, :]) + shift[:, None, :]


def pool_mixer(u, w_pool, pool_scale):
    b, s, _ = u.shape
    ug = u.reshape(b, s, N_POOL_GROUPS, POOL_GROUP_DIM)
    cs = jnp.cumsum(ug.astype(jnp.float32), axis=1)
    pos = jnp.arange(1, s + 1, dtype=jnp.float32)
    means = []
    for g, w in enumerate(POOL_WINDOWS):
        csg = cs[:, :, g]
        lagged = jnp.pad(csg, ((0, 0), (w, 0), (0, 0)))[:, :s]
        cnt = jnp.minimum(pos, float(w))
        means.append((csg - lagged) / cnt[None, :, None])
    mean = jnp.stack(means, axis=2).astype(u.dtype)
    y = jnp.einsum('bsgc,gcd->bsgd', mean - ug, w_pool)
    return y.reshape(b, s, POOL_WIDTH) * pool_scale


def causal_depthwise_conv(x, w, bias):
    ch = x.shape[-1]
    y = lax.conv_general_dilated(
        x, w[:, None, :].astype(x.dtype), window_strides=(1,),
        padding=((SSD_CONV - 1, 0),), dimension_numbers=('NWC', 'WIO', 'NWC'),
        feature_group_count=ch)
    return y + bias


def ssd_scan(xh, dt, a, bmat, cmat):
    b, s = xh.shape[:2]
    L = SSD_CHUNK
    nc = s // L
    G, R, P, N = SSD_GROUPS, SSD_HEADS_PER_GROUP, SSD_HEAD_DIM, SSD_STATE
    xc = (xh.astype(jnp.float32) * dt[..., None]).reshape(b, nc, L, G, R, P)
    bc = bmat.astype(jnp.float32).reshape(b, nc, L, G, N)
    cc = cmat.astype(jnp.float32).reshape(b, nc, L, G, N)
    a_cs = jnp.cumsum((dt * a).reshape(b, nc, L, G, R), axis=2)
    causal = jnp.tril(jnp.ones((L, L), dtype=bool))
    seg = a_cs[:, :, :, None] - a_cs[:, :, None, :]
    decay = jnp.exp(jnp.where(causal[None, None, :, :, None, None], seg, -jnp.inf))
    cb = jnp.einsum('bclgn,bcsgn->bclsg', cc, bc)
    y_diag = jnp.einsum('bclsgr,bcsgrp->bclgrp', cb[..., None] * decay, xc)
    decay_to_end = jnp.exp(a_cs[:, :, -1:] - a_cs)
    chunk_states = jnp.einsum('bclgn,bclgrp->bcgrpn', bc, xc * decay_to_end[..., None])
    chunk_decay = jnp.exp(a_cs[:, :, -1])

    def step(state, inp):
        st, dec = inp
        return state * dec[..., None, None] + st, state

    init = jnp.zeros((b, G, R, P, N), jnp.float32)
    _, prev = lax.scan(step, init, (jnp.moveaxis(chunk_states, 1, 0),
                                    jnp.moveaxis(chunk_decay, 1, 0)))
    prev = jnp.moveaxis(prev, 0, 1)
    y_off = jnp.einsum('bclgn,bcgrpn->bclgrp', cc, prev) * jnp.exp(a_cs)[..., None]
    return (y_diag + y_off).reshape(b, s, G, R, P)


def ssd_mixer(z, xbc, dt_raw, conv_w, conv_b, dt_bias, a_log, d_skip, norm_w):
    b, s, _ = z.shape
    G, R, P, N = SSD_GROUPS, SSD_HEADS_PER_GROUP, SSD_HEAD_DIM, SSD_STATE
    xbc = jax.nn.silu(causal_depthwise_conv(xbc, conv_w, conv_b))
    xs, bm, cm = jnp.split(xbc, [SSD_WIDTH, SSD_WIDTH + G * N], axis=-1)
    xh = xs.reshape(b, s, G, R, P)
    bm = bm.reshape(b, s, G, N)
    cm = cm.reshape(b, s, G, N)
    dt = jax.nn.softplus(dt_raw.astype(jnp.float32) + dt_bias.astype(jnp.float32)).reshape(b, s, G, R)
    a = -jnp.exp(a_log.astype(jnp.float32)).reshape(G, R)
    y = ssd_scan(xh, dt, a, bm, cm) + xh.astype(jnp.float32) * d_skip.astype(jnp.float32).reshape(G, R)[..., None]
    y = y.reshape(b, s, SSD_WIDTH).astype(z.dtype) * jax.nn.silu(z)
    y = rms_norm(y.reshape(b, s, G, SSD_WIDTH // G)).reshape(b, s, SSD_WIDTH)
    return y * norm_w


def hier_moe(h, w_group, b_group, w_router, b_router, w13, w2):
    b, s, d = h.shape
    T = b * s
    hf = h.reshape(T, d)
    g_logits = (hf @ w_group + b_group).astype(jnp.float32)
    g_prob = jax.nn.softmax(g_logits, axis=-1)
    p_g, g_idx = lax.top_k(g_prob, 1)
    e_logits = (hf @ w_router + b_router).astype(jnp.float32).reshape(T, N_EXPERT_GROUPS, EXPERTS_PER_GROUP)
    within = jnp.take_along_axis(e_logits, g_idx[:, :, None], axis=1)[:, 0]
    top_v, top_i = lax.top_k(within, TOP_K)
    gate = p_g * jax.nn.softmax(top_v, axis=-1)
    expert_id = (g_idx * EXPERTS_PER_GROUP + top_i).astype(jnp.int32)
    A = T * TOP_K
    flat_e = expert_id.reshape(A)
    flat_tok = jnp.repeat(jnp.arange(T, dtype=jnp.int32), TOP_K)
    flat_w = gate.reshape(A)
    order = jnp.argsort(flat_e)
    se, stok, sw = flat_e[order], flat_tok[order], flat_w[order]
    counts = jnp.bincount(flat_e, length=N_EXPERTS)
    starts = jnp.cumsum(counts) - counts
    padded = ((counts + MOE_BLOCK - 1) // MOE_BLOCK) * MOE_BLOCK
    pends = jnp.cumsum(padded)
    pstarts = pends - padded
    dest = pstarts[se] + (jnp.arange(A, dtype=jnp.int32) - starts[se])
    R = A + N_EXPERTS * MOE_BLOCK
    n_blk = R // MOE_BLOCK
    row_tok = jnp.zeros((R,), jnp.int32).at[dest].set(stok)
    row_w = jnp.zeros((R,), h.dtype).at[dest].set(sw.astype(h.dtype))
    blk_start = jnp.arange(n_blk, dtype=jnp.int32) * MOE_BLOCK
    blk_e = jnp.clip(jnp.searchsorted(pends, blk_start, side='right'), 0, N_EXPERTS - 1)
    xs = hf[row_tok].reshape(n_blk, MOE_BLOCK, d)

    def expert_block(args):
        xb, e = args
        hu = xb @ w13[e]
        a_, b_ = jnp.split(hu, 2, axis=-1)
        return (jax.nn.silu(a_) * b_) @ w2[e]

    ys = lax.map(expert_block, (xs, blk_e)).reshape(R, d)
    out = jax.ops.segment_sum(ys * row_w[:, None], row_tok, num_segments=T)
    return out.reshape(b, s, d)


def setup_inputs(seed: int = 0) -> dict:
    key = jax.random.key(seed)
    ks = jax.random.split(key, 24)
    f32 = jnp.float32
    nrm = lambda k, shape, scale: jax.random.normal(k, shape, f32) * scale
    x = jax.random.normal(ks[0], (BATCH, SEQ, D_MODEL), f32)
    c = jax.random.normal(ks[1], (BATCH, D_MODEL), f32)
    w_ada = nrm(ks[2], (DEPTH, D_MODEL, 6 * D_MODEL), D_MODEL ** -0.5)
    b_ada = nrm(ks[3], (DEPTH, 6 * D_MODEL), 0.02)
    w_in = nrm(ks[4], (DEPTH, D_MODEL, IN_PROJ_DIM), D_MODEL ** -0.5)
    w_pool = nrm(ks[5], (DEPTH, N_POOL_GROUPS, POOL_GROUP_DIM, POOL_GROUP_DIM), POOL_GROUP_DIM ** -0.5)
    pool_scale = 1.0 + nrm(ks[6], (DEPTH, POOL_WIDTH), 0.1)
    conv_w = nrm(ks[7], (DEPTH, SSD_CONV, SSD_CONV_DIM), SSD_CONV ** -0.5)
    conv_b = nrm(ks[8], (DEPTH, SSD_CONV_DIM), 0.02)
    dt0 = jnp.exp(jax.random.uniform(ks[9], (DEPTH, SSD_HEADS), f32, np.log(1e-3), np.log(1e-1)))
    dt_bias = dt0 + jnp.log(-jnp.expm1(-dt0))
    a_log = jnp.log(jax.random.uniform(ks[10], (DEPTH, SSD_HEADS), f32, 1.0, 16.0))
    d_skip = 1.0 + nrm(ks[11], (DEPTH, SSD_HEADS), 0.1)
    ssd_norm_w = 1.0 + nrm(ks[12], (DEPTH, SSD_WIDTH), 0.1)
    w_out = nrm(ks[13], (DEPTH, D_MIX, D_MODEL), D_MIX ** -0.5)
    w_group = nrm(ks[14], (DEPTH, D_MODEL, N_EXPERT_GROUPS), D_MODEL ** -0.5)
    b_group = nrm(ks[15], (DEPTH, N_EXPERT_GROUPS), 0.01)
    w_router = nrm(ks[16], (DEPTH, D_MODEL, N_EXPERTS), D_MODEL ** -0.5)
    b_router = nrm(ks[17], (DEPTH, N_EXPERTS), 0.01)
    w13 = nrm(ks[18], (DEPTH, N_EXPERTS, D_MODEL, 2 * EXPERT_FF), D_MODEL ** -0.5)
    w2 = nrm(ks[19], (DEPTH, N_EXPERTS, EXPERT_FF, D_MODEL), EXPERT_FF ** -0.5)
    final_norm_w = 1.0 + nrm(ks[20], (D_MODEL,), 0.1)
    return {"x": x, "c": c, "w_ada": w_ada, "b_ada": b_ada, "w_in": w_in,
            "w_pool": w_pool, "pool_scale": pool_scale, "conv_w": conv_w, "conv_b": conv_b,
            "dt_bias": dt_bias, "a_log": a_log, "d_skip": d_skip, "ssd_norm_w": ssd_norm_w,
            "w_out": w_out, "w_group": w_group, "b_group": b_group, "w_router": w_router,
            "b_router": b_router, "w13": w13, "w2": w2, "final_norm_w": final_norm_w}


def reference(x, c, w_ada, b_ada, w_in, w_pool, pool_scale, conv_w, conv_b, dt_bias, a_log,
              d_skip, ssd_norm_w, w_out, w_group, b_group, w_router, b_router, w13, w2,
              final_norm_w):
    split_at = [POOL_WIDTH, POOL_WIDTH + SSD_WIDTH, POOL_WIDTH + SSD_WIDTH + SSD_CONV_DIM]
    for layer in range(DEPTH):
        mod = jax.nn.silu(c) @ w_ada[layer] + b_ada[layer]
        sh1, sc1, g1, sh2, sc2, g2 = jnp.split(mod, 6, axis=-1)
        h = modulate(x, sh1, sc1)
        proj = h @ w_in[layer]
        u, z, xbc, dt_raw = jnp.split(proj, split_at, axis=-1)
        y_pool = pool_mixer(u, w_pool[layer], pool_scale[layer])
        y_ssd = ssd_mixer(z, xbc, dt_raw, conv_w[layer], conv_b[layer], dt_bias[layer],
                          a_log[layer], d_skip[layer], ssd_norm_w[layer])
        mix = jnp.concatenate([y_pool, y_ssd], axis=-1) @ w_out[layer]
        x = x + g1[:, None, :] * mix
        h = modulate(x, sh2, sc2)
        x = x + g2[:, None, :] * hier_moe(h, w_group[layer], b_group[layer], w_router[layer],
                                           b_router[layer], w13[layer], w2[layer])
    return rms_norm(x) * final_norm_w
```

```python
import functools

import jax
import jax.numpy as jnp
from jax import lax
from jax.experimental import pallas as pl
from jax.experimental.pallas import tpu as pltpu

POOL_WINDOWS = (2, 4, 8, 16)
POOL_GROUP_DIM = 64
SSD_HEAD_DIM = 64
SSD_GROUPS = 4
SSD_HEADS_PER_GROUP = 3
SSD_HEADS = SSD_GROUPS * SSD_HEADS_PER_GROUP
SSD_STATE = 128
SSD_CONV = 4
SSD_CHUNK = 128
N_EXPERT_GROUPS = 4
EXPERTS_PER_GROUP = 8
N_EXPERTS = N_EXPERT_GROUPS * EXPERTS_PER_GROUP
TOP_K = 2
NORM_EPS = 1e-6

LANES = 128
SUBLANES = 8
VMEM_LIMIT_BYTES = 56 * 1024 * 1024

SEQ_TILE = 256
CONV_HALO = SUBLANES
POOL_HALO = 16
RANK_TILE = 512
DISPATCH_TILE = 1024
EXPERT_BLOCK = 256
FINAL_TILE = 256

NEG_BIG = -1e30
F32 = jnp.float32
BF16 = jnp.bfloat16


def _silu(v):
    return v * (1.0 / (1.0 + jnp.exp(-v)))


def _softplus(v):
    return jnp.maximum(v, 0.0) + jnp.log1p(jnp.exp(-jnp.abs(v)))


def _rms(v):
    return v * lax.rsqrt(jnp.mean(v * v, axis=-1, keepdims=True) + NORM_EPS)


def _dot(a, b):
    return jnp.dot(a, b, preferred_element_type=F32)


def _dot_exact(a, b):
    return jnp.dot(a, b, preferred_element_type=F32, precision=lax.Precision.HIGHEST)


def _ada_kernel(c_ref, w_ref, b_ref, o_ref):
    o_ref[...] = _dot_exact(_silu(c_ref[...]), w_ref[...]) + b_ref[...]


def _ada(c, w_ada, b_ada):
    b, d = c.shape
    n = w_ada.shape[1]
    return pl.pallas_call(
        _ada_kernel,
        grid=(n // d,),
        in_specs=[pl.BlockSpec((b, d), lambda i: (0, 0)),
                  pl.BlockSpec((d, d), lambda i: (0, i)),
                  pl.BlockSpec((1, d), lambda i: (0, i))],
        out_specs=pl.BlockSpec((b, d), lambda i: (0, i)),
        out_shape=jax.ShapeDtypeStruct((b, n), F32),
        compiler_params=pltpu.CompilerParams(dimension_semantics=("arbitrary",),
                                             vmem_limit_bytes=VMEM_LIMIT_BYTES),
        name="ada",
    )(c, w_ada, b_ada.reshape(1, n))


def _mixer_kernel(x_ref, mod_ref, win_ref, wp_ref, ps_ref, cw_ref, cb_ref, dtb_ref, alog_ref,
                  dsk_ref, nw_ref, wout_ref, wrh_ref, wrl_ref, br_ref,
                  x1_ref, h2_ref, route_ref,
                  state_ref, xext_ref, uext_ref, *, pool_w, ssd_w):
    ts = x_ref.shape[1]
    d = x_ref.shape[2]
    L = SSD_CHUNK
    N = SSD_STATE
    P = SSD_HEAD_DIM
    j = pl.program_id(1)

    @pl.when(j == 0)
    def _():
        state_ref[...] = jnp.zeros_like(state_ref)
        xext_ref[0:CONV_HALO, :] = jnp.zeros((CONV_HALO, xext_ref.shape[1]), F32)
        uext_ref[0:POOL_HALO, :] = jnp.zeros((POOL_HALO, uext_ref.shape[1]), F32)

    x = x_ref[0]
    mod = mod_ref[0]
    sh1, sc1, g1 = mod[0:1], mod[1:2], mod[2:3]
    sh2, sc2 = mod[3:4], mod[4:5]

    h = _rms(x) * (1.0 + sc1) + sh1
    proj = _dot(h.astype(BF16), win_ref[...])
    o_z = pool_w
    o_x = pool_w + ssd_w
    conv_dim = ssd_w + 2 * SSD_GROUPS * N
    o_dt = o_x + conv_dim
    u = proj[:, 0:pool_w]
    z = proj[:, o_z:o_x]
    xbc = proj[:, o_x:o_dt]
    dtr = proj[:, o_dt:o_dt + LANES]

    uext_ref[POOL_HALO:POOL_HALO + ts, :] = u
    lane_u = lax.broadcasted_iota(jnp.int32, (1, LANES), 1)
    upper = lane_u >= POOL_GROUP_DIM

    def ush(k, col):
        return uext_ref[POOL_HALO - k:POOL_HALO - k + ts, col * LANES:(col + 1) * LANES]

    w0, w1, w2, w3 = POOL_WINDOWS
    lo_a = ush(0, 0)
    for k in range(1, w0):
        lo_a = lo_a + ush(k, 0)
    lo_b = ush(w0, 0)
    for k in range(w0 + 1, w1):
        lo_b = lo_b + ush(k, 0)
    hi_a = ush(0, 1)
    for k in range(1, w2):
        hi_a = hi_a + ush(k, 1)
    hi_b = ush(w2, 1)
    for k in range(w2 + 1, w3):
        hi_b = hi_b + ush(k, 1)
    win_lo = lo_a + jnp.where(upper, lo_b, 0.0)
    win_hi = hi_a + jnp.where(upper, hi_b, 0.0)
    pos = (j * ts + 1 + lax.broadcasted_iota(jnp.int32, (ts, 1), 0)).astype(F32)
    cnt_lo = jnp.minimum(pos, jnp.where(upper, float(w1), float(w0)))
    cnt_hi = jnp.minimum(pos, jnp.where(upper, float(w3), float(w2)))
    pdiff = jnp.concatenate([win_lo / cnt_lo, win_hi / cnt_hi], axis=1) - u
    y_pool = _dot(pdiff.astype(BF16), wp_ref[...]) * ps_ref[...]
    uext_ref[0:POOL_HALO, :] = uext_ref[ts:ts + POOL_HALO, :]

    xext_ref[CONV_HALO:CONV_HALO + ts, :] = xbc
    cw = cw_ref[...]
    conv = cb_ref[...] + cw[SSD_CONV - 1:SSD_CONV] * xbc
    for k in range(SSD_CONV - 1):
        off = CONV_HALO - (SSD_CONV - 1) + k
        conv = conv + cw[k:k + 1] * xext_ref[off:off + ts, :]
    xa = _silu(conv)
    xext_ref[0:CONV_HALO, :] = xext_ref[ts:ts + CONV_HALO, :]

    a_neg = -jnp.exp(alog_ref[...])
    row_i = lax.broadcasted_iota(jnp.int32, (L, L), 0)
    col_i = lax.broadcasted_iota(jnp.int32, (L, L), 1)
    causal = row_i >= col_i
    tril = causal.astype(F32)
    y_chunks = []
    for c in range(ts // L):
        r0 = c * L
        xs_c = xa[r0:r0 + L, 0:ssd_w]
        b_c = xa[r0:r0 + L, ssd_w:ssd_w + SSD_GROUPS * N]
        c_c = xa[r0:r0 + L, ssd_w + SSD_GROUPS * N:conv_dim]
        dt = _softplus(dtr[r0:r0 + L] + dtb_ref[...])
        a_cs = _dot_exact(tril, dt * a_neg)
        a_cs_t = a_cs.T
        dt_t = dt.T
        ea = jnp.exp(a_cs)
        y_heads = []
        for g in range(SSD_GROUPS):
            bg = b_c[:, g * N:(g + 1) * N]
            cg = c_c[:, g * N:(g + 1) * N]
            cb = lax.dot_general(cg.astype(BF16), bg.astype(BF16), (((1,), (1,)), ((), ())),
                                 preferred_element_type=F32)
            bg_t = bg.T
            for r in range(SSD_HEADS_PER_GROUP):
                hd = g * SSD_HEADS_PER_GROUP + r
                a_col = a_cs[:, hd:hd + 1]
                a_row = a_cs_t[hd:hd + 1, :]
                dt_row = dt_t[hd:hd + 1, :]
                decay = jnp.exp(jnp.where(causal, a_col - a_row, NEG_BIG))
                m = cb * decay * dt_row
                cs = cg * ea[:, hd:hd + 1]
                lhs = jnp.concatenate([m, cs], axis=1).astype(BF16)
                xs_h = xs_c[:, hd * P:(hd + 1) * P].astype(BF16)
                st = state_ref[hd]
                rhs = jnp.concatenate([xs_h, st.astype(BF16)], axis=0)
                y_heads.append(_dot(lhs, rhs))
                a_end = a_row[:, L - 1:L]
                w_row = dt_row * jnp.exp(a_end - a_row)
                state_ref[hd] = jnp.exp(a_end) * st + _dot((bg_t * w_row).astype(BF16), xs_h)
        y_chunks.append(jnp.concatenate(y_heads, axis=1))
    y = jnp.concatenate(y_chunks, axis=0) if len(y_chunks) > 1 else y_chunks[0]
    y = y + xa[:, 0:ssd_w] * dsk_ref[...]
    y = y * _silu(z)

    gw = ssd_w // SSD_GROUPS
    lane_y = lax.broadcasted_iota(jnp.int32, (1, ssd_w), 1)
    y2 = y * y
    scale = jnp.zeros_like(y)
    for g in range(SSD_GROUPS):
        in_g = (lane_y >= g * gw) & (lane_y < (g + 1) * gw)
        ms = jnp.sum(jnp.where(in_g, y2, 0.0), axis=-1, keepdims=True) * (1.0 / gw)
        scale = scale + jnp.where(in_g, lax.rsqrt(ms + NORM_EPS), 0.0)
    y_ssd = y * scale * nw_ref[...]

    mix = _dot(jnp.concatenate([y_pool, y_ssd], axis=1).astype(BF16), wout_ref[...])
    x1 = x + g1 * mix
    x1_ref[0] = x1

    h2 = _rms(x1) * (1.0 + sc2) + sh2
    for cidx in range(d // LANES):
        h2_ref[:, cidx, :] = h2[:, cidx * LANES:(cidx + 1) * LANES]
    h_hi = h2.astype(BF16)
    h_lo = (h2 - h_hi.astype(F32)).astype(BF16)
    logits = (_dot(h_hi, wrh_ref[...]) + _dot(h_lo, wrh_ref[...]) + _dot(h_hi, wrl_ref[...])
              + br_ref[...])
    lane = lax.broadcasted_iota(jnp.int32, (1, LANES), 1).astype(F32)
    big = float(LANES)
    gl = jnp.where(lane < N_EXPERT_GROUPS, logits, NEG_BIG)
    gmax = jnp.max(gl, axis=-1, keepdims=True)
    gsum = jnp.sum(jnp.exp(gl - gmax), axis=-1, keepdims=True)
    p_g = 1.0 / gsum
    g_idx = jnp.min(jnp.where(gl == gmax, lane, big), axis=-1, keepdims=True)
    lo = N_EXPERT_GROUPS + EXPERTS_PER_GROUP * g_idx
    el = jnp.where((lane >= lo) & (lane < lo + EXPERTS_PER_GROUP), logits, NEG_BIG)
    v1 = jnp.max(el, axis=-1, keepdims=True)
    i1 = jnp.min(jnp.where(el == v1, lane, big), axis=-1, keepdims=True)
    el2 = jnp.where(lane == i1, NEG_BIG, el)
    v2 = jnp.max(el2, axis=-1, keepdims=True)
    i2 = jnp.min(jnp.where(el2 == v2, lane, big), axis=-1, keepdims=True)
    e21 = jnp.exp(v2 - v1)
    gate1 = p_g / (1.0 + e21)
    gate2 = p_g * e21 / (1.0 + e21)
    route = jnp.where(lane == 0, i1 - N_EXPERT_GROUPS,
                      jnp.where(lane == 1, i2 - N_EXPERT_GROUPS,
                                jnp.where(lane == 2, gate1, jnp.where(lane == 3, gate2, 0.0))))
    route_ref[...] = route


def _mixer(x, mod, win, wp, ps, cw, cb, dtb, alog, dsk, nw, wout, wrh, wrl, br, *, pool_w, ssd_w):
    b, s, d = x.shape
    ts = SEQ_TILE
    conv_dim = cw.shape[1]
    nrow = d // LANES

    def full(a):
        nd = a.ndim
        return pl.BlockSpec(a.shape, lambda i, j, _nd=nd: (0,) * _nd)

    tok = lambda i, j: (i * (s // ts) + j, 0, 0)
    return pl.pallas_call(
        functools.partial(_mixer_kernel, pool_w=pool_w, ssd_w=ssd_w),
        grid=(b, s // ts),
        in_specs=[pl.BlockSpec((1, ts, d), lambda i, j: (i, j, 0)),
                  pl.BlockSpec((1, 6, d), lambda i, j: (i, 0, 0)),
                  full(win), full(wp), full(ps), full(cw), full(cb), full(dtb), full(alog),
                  full(dsk), full(nw), full(wout), full(wrh), full(wrl), full(br)],
        out_specs=[pl.BlockSpec((1, ts, d), lambda i, j: (i, j, 0)),
                   pl.BlockSpec((ts, nrow, LANES), tok),
                   pl.BlockSpec((ts, LANES), lambda i, j: (i * (s // ts) + j, 0))],
        out_shape=[jax.ShapeDtypeStruct((b, s, d), F32),
                   jax.ShapeDtypeStruct((b * s, nrow, LANES), F32),
                   jax.ShapeDtypeStruct((b * s, LANES), F32)],
        scratch_shapes=[pltpu.VMEM((SSD_HEADS, SSD_STATE, SSD_HEAD_DIM), F32),
                        pltpu.VMEM((CONV_HALO + ts, conv_dim), F32),
                        pltpu.VMEM((POOL_HALO + ts, pool_w), F32)],
        compiler_params=pltpu.CompilerParams(dimension_semantics=("arbitrary", "arbitrary"),
                                             vmem_limit_bytes=VMEM_LIMIT_BYTES),
        name="mixer",
    )(x, mod, win, wp, ps, cw, cb, dtb, alog, dsk, nw, wout, wrh, wrl, br)


def _onehots(route):
    lane = lax.broadcasted_iota(jnp.int32, (1, LANES), 1).astype(F32)
    oh1 = jnp.where(route[:, 0:1] == lane, 1.0, 0.0)
    oh2 = jnp.where(route[:, 1:2] == lane, 1.0, 0.0)
    return oh1, oh2


def _count_kernel(route_ref, cnt_ref):
    @pl.when(pl.program_id(0) == 0)
    def _():
        cnt_ref[...] = jnp.zeros_like(cnt_ref)

    oh1, oh2 = _onehots(route_ref[...])
    cnt_ref[...] += jnp.sum(oh1 + oh2, axis=0, keepdims=True)


def _count(route):
    t = route.shape[0]
    return pl.pallas_call(
        _count_kernel,
        grid=(t // RANK_TILE,),
        in_specs=[pl.BlockSpec((RANK_TILE, LANES), lambda i: (i, 0))],
        out_specs=pl.BlockSpec((1, LANES), lambda i: (0, 0)),
        out_shape=jax.ShapeDtypeStruct((1, LANES), F32),
        compiler_params=pltpu.CompilerParams(dimension_semantics=("arbitrary",)),
        name="count",
    )(route)


def _rank_kernel(route_ref, start_ref, dest_ref, carry_ref):
    ta = route_ref.shape[0]

    @pl.when(pl.program_id(0) == 0)
    def _():
        carry_ref[...] = start_ref[...]

    oh1, oh2 = _onehots(route_ref[...])
    oh = oh1 + oh2
    earlier = (lax.broadcasted_iota(jnp.int32, (ta, ta), 0)
               > lax.broadcasted_iota(jnp.int32, (ta, ta), 1))
    before = _dot(jnp.where(earlier, 1.0, 0.0).astype(BF16), oh.astype(BF16))
    base = carry_ref[...] + before
    d1 = jnp.sum(oh1 * base, axis=-1, keepdims=True)
    d2 = jnp.sum(oh2 * base, axis=-1, keepdims=True)
    lane = lax.broadcasted_iota(jnp.int32, (1, LANES), 1)
    dest_ref[...] = jnp.where(lane == 0, d1, jnp.where(lane == 1, d2, 0.0))
    carry_ref[...] += jnp.sum(oh, axis=0, keepdims=True)


def _rank(route, start):
    t = route.shape[0]
    return pl.pallas_call(
        _rank_kernel,
        grid=(t // RANK_TILE,),
        in_specs=[pl.BlockSpec((RANK_TILE, LANES), lambda i: (i, 0)),
                  pl.BlockSpec((1, LANES), lambda i: (0, 0))],
        out_specs=pl.BlockSpec((RANK_TILE, LANES), lambda i: (i, 0)),
        out_shape=jax.ShapeDtypeStruct((t, LANES), F32),
        scratch_shapes=[pltpu.VMEM((1, LANES), F32)],
        compiler_params=pltpu.CompilerParams(dimension_semantics=("arbitrary",)),
        name="rank",
    )(route, start)


def _dispatch_kernel(dest_ref, h2_hbm, xs_hbm, sem):
    n = dest_ref.shape[2]
    base = pl.program_id(0) * (n // TOP_K)

    def row_copy(a):
        return pltpu.make_async_copy(h2_hbm.at[base + a // TOP_K], xs_hbm.at[dest_ref[0, 0, a]], sem)

    def issue(a, carry):
        row_copy(a).start()
        return carry

    lax.fori_loop(0, n, issue, 0)

    def drain(a, carry):
        row_copy(a).wait()
        return carry

    lax.fori_loop(0, n, drain, 0)


def _dispatch(dest, h2):
    t, nrow, _ = h2.shape
    n = DISPATCH_TILE * TOP_K
    return pl.pallas_call(
        _dispatch_kernel,
        grid=(t // DISPATCH_TILE,),
        in_specs=[pl.BlockSpec((1, 1, n), lambda i: (i, 0, 0), memory_space=pltpu.SMEM),
                  pl.BlockSpec(memory_space=pl.ANY)],
        out_specs=pl.BlockSpec(memory_space=pl.ANY),
        out_shape=jax.ShapeDtypeStruct((t * TOP_K, nrow, LANES), F32),
        scratch_shapes=[pltpu.SemaphoreType.DMA(())],
        compiler_params=pltpu.CompilerParams(dimension_semantics=("arbitrary",),
                                             has_side_effects=True),
        name="dispatch",
    )(dest.reshape(t // DISPATCH_TILE, 1, n), h2)


def _expert_kernel(blk_ref, exp_ref, lo_ref, hi_ref, first_ref, xs_ref, w13_ref, w2_ref, ys_ref):
    i = pl.program_id(0)
    nrow = xs_ref.shape[1]
    rows = xs_ref.shape[0]
    lo = lo_ref[i]
    hi = hi_ref[i]

    @pl.when(hi > lo)
    def _():
        xb = jnp.concatenate([xs_ref[:, c, :].astype(BF16) for c in range(nrow)], axis=1)
        hu = _dot(xb, w13_ref[0])
        f = hu.shape[1] // 2
        act = _silu(hu[:, :f]) * hu[:, f:]
        y = _dot(act.astype(BF16), w2_ref[0])
        ridx = lax.broadcasted_iota(jnp.int32, (rows, 1), 0)
        mine = (ridx >= lo) & (ridx < hi)

        @pl.when(first_ref[i] == 1)
        def _():
            for c in range(nrow):
                ys_ref[:, c, :] = y[:, c * LANES:(c + 1) * LANES]

        @pl.when(first_ref[i] == 0)
        def _():
            for c in range(nrow):
                ys_ref[:, c, :] = jnp.where(mine, y[:, c * LANES:(c + 1) * LANES], ys_ref[:, c, :])


def _experts(item_blk, item_exp, item_lo, item_hi, item_first, xs, w13, w2):
    a, nrow, _ = xs.shape
    n_items = item_blk.shape[0]
    d = w13.shape[1]
    ff2 = w13.shape[2]
    grid_spec = pltpu.PrefetchScalarGridSpec(
        num_scalar_prefetch=5,
        grid=(n_items,),
        in_specs=[pl.BlockSpec((EXPERT_BLOCK, nrow, LANES), lambda i, b, e, lo, hi, fr: (b[i], 0, 0)),
                  pl.BlockSpec((1, d, ff2), lambda i, b, e, lo, hi, fr: (e[i], 0, 0)),
                  pl.BlockSpec((1, ff2 // 2, d), lambda i, b, e, lo, hi, fr: (e[i], 0, 0))],
        out_specs=pl.BlockSpec((EXPERT_BLOCK, nrow, LANES), lambda i, b, e, lo, hi, fr: (b[i], 0, 0)),
    )
    return pl.pallas_call(
        _expert_kernel,
        grid_spec=grid_spec,
        out_shape=jax.ShapeDtypeStruct((a, nrow, LANES), F32),
        compiler_params=pltpu.CompilerParams(dimension_semantics=("arbitrary",),
                                             vmem_limit_bytes=VMEM_LIMIT_BYTES),
        name="experts",
    )(item_blk, item_exp, item_lo, item_hi, item_first, xs, w13, w2)


def _final_kernel(dest_ref, x1_ref, route_ref, mod_ref, fw_ref, ys_hbm, o_ref, buf1, buf2, sem):
    tf = x1_ref.shape[1]
    nrow = buf1.shape[1]

    def row_copy(tk, k, buf):
        return pltpu.make_async_copy(ys_hbm.at[dest_ref[0, 0, tk * TOP_K + k]], buf.at[tk], sem)

    def issue(tk, carry):
        row_copy(tk, 0, buf1).start()
        row_copy(tk, 1, buf2).start()
        return carry

    lax.fori_loop(0, tf, issue, 0)

    def drain(tk, carry):
        row_copy(tk, 0, buf1).wait()
        row_copy(tk, 1, buf2).wait()
        return carry

    lax.fori_loop(0, tf, drain, 0)

    r = route_ref[...]
    gate1 = r[:, 2:3]
    gate2 = r[:, 3:4]
    y = jnp.concatenate([buf1[:, c, :] * gate1 + buf2[:, c, :] * gate2 for c in range(nrow)], axis=1)
    g2 = mod_ref[0][5:6]
    o_ref[0] = _rms(x1_ref[0] + g2 * y) * fw_ref[...]


def _final(dest, x1, route, mod, fw, ys):
    b, s, d = x1.shape
    tf = FINAL_TILE
    nrow = d // LANES
    spt = s // tf
    n = tf * TOP_K
    return pl.pallas_call(
        _final_kernel,
        grid=(b, spt),
        in_specs=[pl.BlockSpec((1, 1, n), lambda i, j: (i * spt + j, 0, 0), memory_space=pltpu.SMEM),
                  pl.BlockSpec((1, tf, d), lambda i, j: (i, j, 0)),
                  pl.BlockSpec((tf, LANES), lambda i, j: (i * spt + j, 0)),
                  pl.BlockSpec((1, 6, d), lambda i, j: (i, 0, 0)),
                  pl.BlockSpec((1, d), lambda i, j: (0, 0)),
                  pl.BlockSpec(memory_space=pl.ANY)],
        out_specs=pl.BlockSpec((1, tf, d), lambda i, j: (i, j, 0)),
        out_shape=jax.ShapeDtypeStruct((b, s, d), F32),
        scratch_shapes=[pltpu.VMEM((tf, nrow, LANES), F32),
                        pltpu.VMEM((tf, nrow, LANES), F32),
                        pltpu.SemaphoreType.DMA(())],
        compiler_params=pltpu.CompilerParams(dimension_semantics=("arbitrary", "arbitrary"),
                                             vmem_limit_bytes=VMEM_LIMIT_BYTES),
        name="final",
    )(dest.reshape(b * spt, 1, n), x1, route, mod, fw, ys)


def _pad_lanes(v, fill=0.0):
    n = v.shape[-1]
    return jnp.pad(v, [(0, 0)] * (v.ndim - 1) + [(0, LANES - n)], constant_values=fill)


def _work_items(counts, n_blocks):
    n_items = n_blocks + N_EXPERTS - 1
    ends = jnp.cumsum(counts)
    starts = ends - counts
    first_blk = starts // EXPERT_BLOCK
    last_blk = jnp.maximum(ends - 1, starts) // EXPERT_BLOCK
    n_e = jnp.where(counts > 0, last_blk - first_blk + 1, 0)
    item_end = jnp.cumsum(n_e)
    item_start = item_end - n_e
    ids = jnp.arange(n_items, dtype=jnp.int32)
    total = item_end[-1]
    ids_c = jnp.minimum(ids, total - 1)
    e = jnp.sum((item_end[None, :] <= ids_c[:, None]).astype(jnp.int32), axis=1)
    onehot = (e[:, None] == jnp.arange(N_EXPERTS, dtype=jnp.int32)[None, :]).astype(jnp.int32)
    pick = lambda v: jnp.sum(onehot * v[None, :], axis=1)
    blk = pick(first_blk) + ids_c - pick(item_start)
    lo = jnp.clip(pick(starts) - blk * EXPERT_BLOCK, 0, EXPERT_BLOCK)
    hi = jnp.clip(pick(ends) - blk * EXPERT_BLOCK, 0, EXPERT_BLOCK)
    hi = jnp.where(ids < total, hi, lo)
    prev_blk = jnp.concatenate([jnp.full((1,), -1, jnp.int32), blk[:-1]])
    first = (blk != prev_blk).astype(jnp.int32)
    return blk, e, lo, hi, first


def kernel(x, c, w_ada, b_ada, w_in, w_pool, pool_scale, conv_w, conv_b, dt_bias, a_log, d_skip,
           ssd_norm_w, w_out, w_group, b_group, w_router, b_router, w13, w2, final_norm_w):
    b, s, d = x.shape
    depth = w_ada.shape[0]
    t = b * s
    pool_w = w_pool.shape[1] * w_pool.shape[2]
    ssd_w = SSD_HEADS * SSD_HEAD_DIM
    conv_dim = conv_w.shape[2]
    assert s % SEQ_TILE == 0 and SEQ_TILE % SSD_CHUNK == 0
    assert t % RANK_TILE == 0 and t % DISPATCH_TILE == 0 and s % FINAL_TILE == 0
    assert (t * TOP_K) % EXPERT_BLOCK == 0 and d % LANES == 0
    assert pool_w == 2 * LANES and len(POOL_WINDOWS) * POOL_GROUP_DIM == pool_w
    assert N_EXPERT_GROUPS + N_EXPERTS <= LANES and SSD_HEADS <= LANES
    assert depth == 1, "the final RMSNorm is fused into the last layer's combine step"

    for layer in range(depth):
        mod = _ada(c, w_ada[layer], b_ada[layer]).reshape(b, 6, d)

        o_dt = pool_w + ssd_w + conv_dim
        win = jnp.concatenate([w_in[layer][:, :o_dt], _pad_lanes(w_in[layer][:, o_dt:])], axis=1).astype(BF16)
        wp = jnp.zeros((pool_w, pool_w), F32)
        for g in range(len(POOL_WINDOWS)):
            sl = slice(g * POOL_GROUP_DIM, (g + 1) * POOL_GROUP_DIM)
            wp = wp.at[sl, sl].set(w_pool[layer, g])
        wp = wp.astype(BF16)
        w_route = _pad_lanes(jnp.concatenate([w_group[layer], w_router[layer]], axis=1))
        wrh = w_route.astype(BF16)
        wrl = (w_route - wrh.astype(F32)).astype(BF16)
        br = _pad_lanes(jnp.concatenate([b_group[layer], b_router[layer]])[None, :])

        x1, h2, route = _mixer(
            x, mod, win, wp, pool_scale[layer][None, :], conv_w[layer], conv_b[layer][None, :],
            _pad_lanes(dt_bias[layer][None, :]), _pad_lanes(a_log[layer][None, :], fill=NEG_BIG),
            jnp.repeat(d_skip[layer], SSD_HEAD_DIM)[None, :], ssd_norm_w[layer][None, :],
            w_out[layer].astype(BF16), wrh, wrl, br, pool_w=pool_w, ssd_w=ssd_w)

        counts_f = _count(route)
        start_f = jnp.cumsum(counts_f, axis=1) - counts_f
        dest = _rank(route, start_f)[:, :TOP_K].astype(jnp.int32)
        counts = counts_f[0, :N_EXPERTS].astype(jnp.int32)
        items = _work_items(counts, (t * TOP_K) // EXPERT_BLOCK)

        xs = _dispatch(dest, h2)
        ys = _experts(*items, xs, w13[layer].astype(BF16), w2[layer].astype(BF16))
        x = _final(dest, x1, route, mod, final_norm_w[None, :], ys)
    return x
```

```python
import functools

import jax
import jax.numpy as jnp
from jax import lax
from jax.experimental import pallas as pl
from jax.experimental.pallas import tpu as pltpu

POOL_WINDOWS = (2, 4, 8, 16)
POOL_GROUP_DIM = 64
SSD_HEAD_DIM = 64
SSD_GROUPS = 4
SSD_HEADS_PER_GROUP = 3
SSD_HEADS = SSD_GROUPS * SSD_HEADS_PER_GROUP
SSD_STATE = 128
SSD_CONV = 4
SSD_CHUNK = 128
N_EXPERT_GROUPS = 4
EXPERTS_PER_GROUP = 8
N_EXPERTS = N_EXPERT_GROUPS * EXPERTS_PER_GROUP
TOP_K = 2
NORM_EPS = 1e-6

LANES = 128
SUBLANES = 8
VMEM_LIMIT_BYTES = 56 * 1024 * 1024

SEQ_TILE = 256
CONV_HALO = SUBLANES
POOL_HALO = 16
RANK_TILE = 512
DISPATCH_TILE = 1024
EXPERT_BLOCK = 256
FINAL_TILE = 256

NEG_BIG = -1e30
F32 = jnp.float32
BF16 = jnp.bfloat16


def _silu(v):
    return v * (1.0 / (1.0 + jnp.exp(-v)))


def _softplus(v):
    return jnp.maximum(v, 0.0) + jnp.log1p(jnp.exp(-jnp.abs(v)))


def _rms(v):
    return v * lax.rsqrt(jnp.mean(v * v, axis=-1, keepdims=True) + NORM_EPS)


def _dot(a, b):
    return jnp.dot(a, b, preferred_element_type=F32)


def _dot_exact(a, b):
    return jnp.dot(a, b, preferred_element_type=F32, precision=lax.Precision.HIGHEST)


def _ada_kernel(c_ref, w_ref, b_ref, o_ref):
    o_ref[...] = _dot_exact(_silu(c_ref[...]), w_ref[...]) + b_ref[...]


def _ada(c, w_ada, b_ada):
    b, d = c.shape
    n = w_ada.shape[1]
    return pl.pallas_call(
        _ada_kernel,
        grid=(n // d,),
        in_specs=[pl.BlockSpec((b, d), lambda i: (0, 0)),
                  pl.BlockSpec((d, d), lambda i: (0, i)),
                  pl.BlockSpec((1, d), lambda i: (0, i))],
        out_specs=pl.BlockSpec((b, d), lambda i: (0, i)),
        out_shape=jax.ShapeDtypeStruct((b, n), F32),
        compiler_params=pltpu.CompilerParams(dimension_semantics=("arbitrary",),
                                             vmem_limit_bytes=VMEM_LIMIT_BYTES),
        name="ada",
    )(c, w_ada, b_ada.reshape(1, n))


def _mixer_kernel(x_ref, mod_ref, win_ref, wp_ref, ps_ref, cw_ref, cb_ref, dtb_ref, alog_ref,
                  dsk_ref, nw_ref, wout_ref, wrh_ref, wrl_ref, br_ref,
                  x1_ref, h2_ref, route_ref,
                  state_ref, xext_ref, uext_ref, *, pool_w, ssd_w):
    ts = x_ref.shape[1]
    d = x_ref.shape[2]
    L = SSD_CHUNK
    N = SSD_STATE
    P = SSD_HEAD_DIM
    j = pl.program_id(1)

    @pl.when(j == 0)
    def _():
        state_ref[...] = jnp.zeros_like(state_ref)
        xext_ref[0:CONV_HALO, :] = jnp.zeros((CONV_HALO, xext_ref.shape[1]), F32)
        uext_ref[0:POOL_HALO, :] = jnp.zeros((POOL_HALO, uext_ref.shape[1]), F32)

    x = x_ref[0]
    mod = mod_ref[0]
    sh1, sc1, g1 = mod[0:1], mod[1:2], mod[2:3]
    sh2, sc2 = mod[3:4], mod[4:5]

    h = _rms(x) * (1.0 + sc1) + sh1
    proj = _dot(h.astype(BF16), win_ref[...])
    o_z = pool_w
    o_x = pool_w + ssd_w
    conv_dim = ssd_w + 2 * SSD_GROUPS * N
    o_dt = o_x + conv_dim
    u = proj[:, 0:pool_w]
    z = proj[:, o_z:o_x]
    xbc = proj[:, o_x:o_dt]
    dtr = proj[:, o_dt:o_dt + LANES]

    uext_ref[POOL_HALO:POOL_HALO + ts, :] = u
    lane_u = lax.broadcasted_iota(jnp.int32, (1, LANES), 1)
    upper = lane_u >= POOL_GROUP_DIM

    def ush(k, col):
        return uext_ref[POOL_HALO - k:POOL_HALO - k + ts, col * LANES:(col + 1) * LANES]

    w0, w1, w2, w3 = POOL_WINDOWS
    lo_a = ush(0, 0)
    for k in range(1, w0):
        lo_a = lo_a + ush(k, 0)
    lo_b = ush(w0, 0)
    for k in range(w0 + 1, w1):
        lo_b = lo_b + ush(k, 0)
    hi_a = ush(0, 1)
    for k in range(1, w2):
        hi_a = hi_a + ush(k, 1)
    hi_b = ush(w2, 1)
    for k in range(w2 + 1, w3):
        hi_b = hi_b + ush(k, 1)
    win_lo = lo_a + jnp.where(upper, lo_b, 0.0)
    win_hi = hi_a + jnp.where(upper, hi_b, 0.0)
    pos = (j * ts + 1 + lax.broadcasted_iota(jnp.int32, (ts, 1), 0)).astype(F32)
    cnt_lo = jnp.minimum(pos, jnp.where(upper, float(w1), float(w0)))
    cnt_hi = jnp.minimum(pos, jnp.where(upper, float(w3), float(w2)))
    pdiff = jnp.concatenate([win_lo / cnt_lo, win_hi / cnt_hi], axis=1) - u
    y_pool = _dot(pdiff.astype(BF16), wp_ref[...]) * ps_ref[...]
    uext_ref[0:POOL_HALO, :] = uext_ref[ts:ts + POOL_HALO, :]

    xext_ref[CONV_HALO:CONV_HALO + ts, :] = xbc
    cw = cw_ref[...]
    conv = cb_ref[...] + cw[SSD_CONV - 1:SSD_CONV] * xbc
    for k in range(SSD_CONV - 1):
        off = CONV_HALO - (SSD_CONV - 1) + k
        conv = conv + cw[k:k + 1] * xext_ref[off:off + ts, :]
    xa = _silu(conv)
    xext_ref[0:CONV_HALO, :] = xext_ref[ts:ts + CONV_HALO, :]

    a_neg = -jnp.exp(alog_ref[...])
    row_i = lax.broadcasted_iota(jnp.int32, (L, L), 0)
    col_i = lax.broadcasted_iota(jnp.int32, (L, L), 1)
    causal = row_i >= col_i
    tril = causal.astype(F32)
    y_chunks = []
    for c in range(ts // L):
        r0 = c * L
        xs_c = xa[r0:r0 + L, 0:ssd_w]
        b_c = xa[r0:r0 + L, ssd_w:ssd_w + SSD_GROUPS * N]
        c_c = xa[r0:r0 + L, ssd_w + SSD_GROUPS * N:conv_dim]
        dt = _softplus(dtr[r0:r0 + L] + dtb_ref[...])
        a_cs = _dot_exact(tril, dt * a_neg)
        a_cs_t = a_cs.T
        dt_t = dt.T
        ea = jnp.exp(a_cs)
        y_heads = []
        for g in range(SSD_GROUPS):
            bg = b_c[:, g * N:(g + 1) * N]
            cg = c_c[:, g * N:(g + 1) * N]
            cb = lax.dot_general(cg.astype(BF16), bg.astype(BF16), (((1,), (1,)), ((), ())),
                                 preferred_element_type=F32)
            bg_t = bg.T
            for r in range(SSD_HEADS_PER_GROUP):
                hd = g * SSD_HEADS_PER_GROUP + r
                a_col = a_cs[:, hd:hd + 1]
                a_row = a_cs_t[hd:hd + 1, :]
                dt_row = dt_t[hd:hd + 1, :]
                decay = jnp.exp(jnp.where(causal, a_col - a_row, NEG_BIG))
                m = cb * decay * dt_row
                cs = cg * ea[:, hd:hd + 1]
                lhs = jnp.concatenate([m, cs], axis=1).astype(BF16)
                xs_h = xs_c[:, hd * P:(hd + 1) * P].astype(BF16)
                st = state_ref[hd]
                rhs = jnp.concatenate([xs_h, st.astype(BF16)], axis=0)
                y_heads.append(_dot(lhs, rhs))
                a_end = a_row[:, L - 1:L]
                w_row = dt_row * jnp.exp(a_end - a_row)
                state_ref[hd] = jnp.exp(a_end) * st + _dot((bg_t * w_row).astype(BF16), xs_h)
        y_chunks.append(jnp.concatenate(y_heads, axis=1))
    y = jnp.concatenate(y_chunks, axis=0) if len(y_chunks) > 1 else y_chunks[0]
    y = y + xa[:, 0:ssd_w] * dsk_ref[...]
    y = y * _silu(z)

    gw = ssd_w // SSD_GROUPS
    lane_y = lax.broadcasted_iota(jnp.int32, (1, ssd_w), 1)
    y2 = y * y
    scale = jnp.zeros_like(y)
    for g in range(SSD_GROUPS):
        in_g = (lane_y >= g * gw) & (lane_y < (g + 1) * gw)
        ms = jnp.sum(jnp.where(in_g, y2, 0.0), axis=-1, keepdims=True) * (1.0 / gw)
        scale = scale + jnp.where(in_g, lax.rsqrt(ms + NORM_EPS), 0.0)
    y_ssd = y * scale * nw_ref[...]

    mix = _dot(jnp.concatenate([y_pool, y_ssd], axis=1).astype(BF16), wout_ref[...])
    x1 = x + g1 * mix
    x1_ref[0] = x1

    h2 = _rms(x1) * (1.0 + sc2) + sh2
    for cidx in range(d // LANES):
        h2_ref[:, cidx, :] = h2[:, cidx * LANES:(cidx + 1) * LANES]
    h_hi = h2.astype(BF16)
    h_lo = (h2 - h_hi.astype(F32)).astype(BF16)
    logits = (_dot(h_hi, wrh_ref[...]) + _dot(h_lo, wrh_ref[...]) + _dot(h_hi, wrl_ref[...])
              + br_ref[...])
    lane = lax.broadcasted_iota(jnp.int32, (1, LANES), 1).astype(F32)
    big = float(LANES)
    gl = jnp.where(lane < N_EXPERT_GROUPS, logits, NEG_BIG)
    gmax = jnp.max(gl, axis=-1, keepdims=True)
    gsum = jnp.sum(jnp.exp(gl - gmax), axis=-1, keepdims=True)
    p_g = 1.0 / gsum
    g_idx = jnp.min(jnp.where(gl == gmax, lane, big), axis=-1, keepdims=True)
    lo = N_EXPERT_GROUPS + EXPERTS_PER_GROUP * g_idx
    el = jnp.where((lane >= lo) & (lane < lo + EXPERTS_PER_GROUP), logits, NEG_BIG)
    v1 = jnp.max(el, axis=-1, keepdims=True)
    i1 = jnp.min(jnp.where(el == v1, lane, big), axis=-1, keepdims=True)
    el2 = jnp.where(lane == i1, NEG_BIG, el)
    v2 = jnp.max(el2, axis=-1, keepdims=True)
    i2 = jnp.min(jnp.where(el2 == v2, lane, big), axis=-1, keepdims=True)
    e21 = jnp.exp(v2 - v1)
    gate1 = p_g / (1.0 + e21)
    gate2 = p_g * e21 / (1.0 + e21)
    route = jnp.where(lane == 0, i1 - N_EXPERT_GROUPS,
                      jnp.where(lane == 1, i2 - N_EXPERT_GROUPS,
                                jnp.where(lane == 2, gate1, jnp.where(lane == 3, gate2, 0.0))))
    route_ref[...] = route


def _mixer(x, mod, win, wp, ps, cw, cb, dtb, alog, dsk, nw, wout, wrh, wrl, br, *, pool_w, ssd_w):
    b, s, d = x.shape
    ts = SEQ_TILE
    conv_dim = cw.shape[1]
    nrow = d // LANES

    def full(a):
        nd = a.ndim
        return pl.BlockSpec(a.shape, lambda i, j, _nd=nd: (0,) * _nd)

    tok = lambda i, j: (i * (s // ts) + j, 0, 0)
    return pl.pallas_call(
        functools.partial(_mixer_kernel, pool_w=pool_w, ssd_w=ssd_w),
        grid=(b, s // ts),
        in_specs=[pl.BlockSpec((1, ts, d), lambda i, j: (i, j, 0)),
                  pl.BlockSpec((1, 6, d), lambda i, j: (i, 0, 0)),
                  full(win), full(wp), full(ps), full(cw), full(cb), full(dtb), full(alog),
                  full(dsk), full(nw), full(wout), full(wrh), full(wrl), full(br)],
        out_specs=[pl.BlockSpec((1, ts, d), lambda i, j: (i, j, 0)),
                   pl.BlockSpec((ts, nrow, LANES), tok),
                   pl.BlockSpec((ts, LANES), lambda i, j: (i * (s // ts) + j, 0))],
        out_shape=[jax.ShapeDtypeStruct((b, s, d), F32),
                   jax.ShapeDtypeStruct((b * s, nrow, LANES), F32),
                   jax.ShapeDtypeStruct((b * s, LANES), F32)],
        scratch_shapes=[pltpu.VMEM((SSD_HEADS, SSD_STATE, SSD_HEAD_DIM), F32),
                        pltpu.VMEM((CONV_HALO + ts, conv_dim), F32),
                        pltpu.VMEM((POOL_HALO + ts, pool_w), F32)],
        compiler_params=pltpu.CompilerParams(dimension_semantics=("arbitrary", "arbitrary"),
                                             vmem_limit_bytes=VMEM_LIMIT_BYTES),
        name="mixer",
    )(x, mod, win, wp, ps, cw, cb, dtb, alog, dsk, nw, wout, wrh, wrl, br)


def _onehots(route):
    lane = lax.broadcasted_iota(jnp.int32, (1, LANES), 1).astype(F32)
    oh1 = jnp.where(route[:, 0:1] == lane, 1.0, 0.0)
    oh2 = jnp.where(route[:, 1:2] == lane, 1.0, 0.0)
    return oh1, oh2


def _count_kernel(route_ref, cnt_ref):
    @pl.when(pl.program_id(0) == 0)
    def _():
        cnt_ref[...] = jnp.zeros_like(cnt_ref)

    oh1, oh2 = _onehots(route_ref[...])
    cnt_ref[...] += jnp.sum(oh1 + oh2, axis=0, keepdims=True)


def _count(route):
    t = route.shape[0]
    return pl.pallas_call(
        _count_kernel,
        grid=(t // RANK_TILE,),
        in_specs=[pl.BlockSpec((RANK_TILE, LANES), lambda i: (i, 0))],
        out_specs=pl.BlockSpec((1, LANES), lambda i: (0, 0)),
        out_shape=jax.ShapeDtypeStruct((1, LANES), F32),
        compiler_params=pltpu.CompilerParams(dimension_semantics=("arbitrary",)),
        name="count",
    )(route)


def _rank_kernel(route_ref, start_ref, dest_ref, carry_ref):
    ta = route_ref.shape[0]

    @pl.when(pl.program_id(0) == 0)
    def _():
        carry_ref[...] = start_ref[...]

    oh1, oh2 = _onehots(route_ref[...])
    oh = oh1 + oh2
    earlier = (lax.broadcasted_iota(jnp.int32, (ta, ta), 0)
               > lax.broadcasted_iota(jnp.int32, (ta, ta), 1))
    before = _dot(jnp.where(earlier, 1.0, 0.0).astype(BF16), oh.astype(BF16))
    base = carry_ref[...] + before
    d1 = jnp.sum(oh1 * base, axis=-1, keepdims=True)
    d2 = jnp.sum(oh2 * base, axis=-1, keepdims=True)
    lane = lax.broadcasted_iota(jnp.int32, (1, LANES), 1)
    dest_ref[...] = jnp.where(lane == 0, d1, jnp.where(lane == 1, d2, 0.0))
    carry_ref[...] += jnp.sum(oh, axis=0, keepdims=True)


def _rank(route, start):
    t = route.shape[0]
    return pl.pallas_call(
        _rank_kernel,
        grid=(t // RANK_TILE,),
        in_specs=[pl.BlockSpec((RANK_TILE, LANES), lambda i: (i, 0)),
                  pl.BlockSpec((1, LANES), lambda i: (0, 0))],
        out_specs=pl.BlockSpec((RANK_TILE, LANES), lambda i: (i, 0)),
        out_shape=jax.ShapeDtypeStruct((t, LANES), F32),
        scratch_shapes=[pltpu.VMEM((1, LANES), F32)],
        compiler_params=pltpu.CompilerParams(dimension_semantics=("arbitrary",)),
        name="rank",
    )(route, start)


def _dispatch_kernel(dest_ref, h2_ref, xs_hbm, sem):
    n = dest_ref.shape[2]

    def row_copy(a):
        return pltpu.make_async_copy(h2_ref.at[a // TOP_K], xs_hbm.at[dest_ref[0, 0, a]], sem)

    def issue(a, carry):
        row_copy(a).start()
        return carry

    lax.fori_loop(0, n, issue, 0)

    for _ in range(TOP_K):
        pltpu.make_async_copy(h2_ref, xs_hbm.at[pl.ds(0, n // TOP_K)], sem).wait()


def _dispatch(dest, h2):
    t, nrow, _ = h2.shape
    n = DISPATCH_TILE * TOP_K
    return pl.pallas_call(
        _dispatch_kernel,
        grid=(t // DISPATCH_TILE,),
        in_specs=[pl.BlockSpec((1, 1, n), lambda i: (i, 0, 0), memory_space=pltpu.SMEM),
                  pl.BlockSpec((DISPATCH_TILE, nrow, LANES), lambda i: (i, 0, 0))],
        out_specs=pl.BlockSpec(memory_space=pl.ANY),
        out_shape=jax.ShapeDtypeStruct((t * TOP_K, nrow, LANES), F32),
        scratch_shapes=[pltpu.SemaphoreType.DMA(())],
        compiler_params=pltpu.CompilerParams(dimension_semantics=("arbitrary",),
                                             has_side_effects=True),
        name="dispatch",
    )(dest.reshape(t // DISPATCH_TILE, 1, n), h2)


def _expert_kernel(blk_ref, exp_ref, lo_ref, hi_ref, first_ref, xs_ref, w13_ref, w2_ref, ys_ref):
    i = pl.program_id(0)
    nrow = xs_ref.shape[1]
    rows = xs_ref.shape[0]
    lo = lo_ref[i]
    hi = hi_ref[i]

    @pl.when(hi > lo)
    def _():
        xb = jnp.concatenate([xs_ref[:, c, :].astype(BF16) for c in range(nrow)], axis=1)
        hu = _dot(xb, w13_ref[0])
        f = hu.shape[1] // 2
        act = _silu(hu[:, :f]) * hu[:, f:]
        y = _dot(act.astype(BF16), w2_ref[0])
        ridx = lax.broadcasted_iota(jnp.int32, (rows, 1), 0)
        mine = (ridx >= lo) & (ridx < hi)

        @pl.when(first_ref[i] == 1)
        def _():
            for c in range(nrow):
                ys_ref[:, c, :] = y[:, c * LANES:(c + 1) * LANES]

        @pl.when(first_ref[i] == 0)
        def _():
            for c in range(nrow):
                ys_ref[:, c, :] = jnp.where(mine, y[:, c * LANES:(c + 1) * LANES], ys_ref[:, c, :])


def _experts(item_blk, item_exp, item_lo, item_hi, item_first, xs, w13, w2):
    a, nrow, _ = xs.shape
    n_items = item_blk.shape[0]
    d = w13.shape[1]
    ff2 = w13.shape[2]
    grid_spec = pltpu.PrefetchScalarGridSpec(
        num_scalar_prefetch=5,
        grid=(n_items,),
        in_specs=[pl.BlockSpec((EXPERT_BLOCK, nrow, LANES), lambda i, b, e, lo, hi, fr: (b[i], 0, 0)),
                  pl.BlockSpec((1, d, ff2), lambda i, b, e, lo, hi, fr: (e[i], 0, 0)),
                  pl.BlockSpec((1, ff2 // 2, d), lambda i, b, e, lo, hi, fr: (e[i], 0, 0))],
        out_specs=pl.BlockSpec((EXPERT_BLOCK, nrow, LANES), lambda i, b, e, lo, hi, fr: (b[i], 0, 0)),
    )
    return pl.pallas_call(
        _expert_kernel,
        grid_spec=grid_spec,
        out_shape=jax.ShapeDtypeStruct((a, nrow, LANES), F32),
        compiler_params=pltpu.CompilerParams(dimension_semantics=("arbitrary",),
                                             vmem_limit_bytes=VMEM_LIMIT_BYTES),
        name="experts",
    )(item_blk, item_exp, item_lo, item_hi, item_first, xs, w13, w2)


def _final_kernel(dest_ref, x1_ref, route_ref, mod_ref, fw_ref, ys_hbm, o_ref, buf1, buf2, sem):
    tf = x1_ref.shape[1]
    nrow = buf1.shape[1]

    def row_copy(tk, k, buf):
        return pltpu.make_async_copy(ys_hbm.at[dest_ref[0, 0, tk * TOP_K + k]], buf.at[tk], sem)

    def issue(tk, carry):
        row_copy(tk, 0, buf1).start()
        row_copy(tk, 1, buf2).start()
        return carry

    lax.fori_loop(0, tf, issue, 0)

    pltpu.make_async_copy(ys_hbm.at[pl.ds(0, tf)], buf1, sem).wait()
    pltpu.make_async_copy(ys_hbm.at[pl.ds(0, tf)], buf2, sem).wait()

    r = route_ref[...]
    gate1 = r[:, 2:3]
    gate2 = r[:, 3:4]
    y = jnp.concatenate([buf1[:, c, :] * gate1 + buf2[:, c, :] * gate2 for c in range(nrow)], axis=1)
    g2 = mod_ref[0][5:6]
    o_ref[0] = _rms(x1_ref[0] + g2 * y) * fw_ref[...]


def _final(dest, x1, route, mod, fw, ys):
    b, s, d = x1.shape
    tf = FINAL_TILE
    nrow = d // LANES
    spt = s // tf
    n = tf * TOP_K
    return pl.pallas_call(
        _final_kernel,
        grid=(b, spt),
        in_specs=[pl.BlockSpec((1, 1, n), lambda i, j: (i * spt + j, 0, 0), memory_space=pltpu.SMEM),
                  pl.BlockSpec((1, tf, d), lambda i, j: (i, j, 0)),
                  pl.BlockSpec((tf, LANES), lambda i, j: (i * spt + j, 0)),
                  pl.BlockSpec((1, 6, d), lambda i, j: (i, 0, 0)),
                  pl.BlockSpec((1, d), lambda i, j: (0, 0)),
                  pl.BlockSpec(memory_space=pl.ANY)],
        out_specs=pl.BlockSpec((1, tf, d), lambda i, j: (i, j, 0)),
        out_shape=jax.ShapeDtypeStruct((b, s, d), F32),
        scratch_shapes=[pltpu.VMEM((tf, nrow, LANES), F32),
                        pltpu.VMEM((tf, nrow, LANES), F32),
                        pltpu.SemaphoreType.DMA(())],
        compiler_params=pltpu.CompilerParams(dimension_semantics=("arbitrary", "arbitrary"),
                                             vmem_limit_bytes=VMEM_LIMIT_BYTES),
        name="final",
    )(dest.reshape(b * spt, 1, n), x1, route, mod, fw, ys)


def _pad_lanes(v, fill=0.0):
    n = v.shape[-1]
    return jnp.pad(v, [(0, 0)] * (v.ndim - 1) + [(0, LANES - n)], constant_values=fill)


def _work_items(counts, n_blocks):
    n_items = n_blocks + N_EXPERTS - 1
    ends = jnp.cumsum(counts)
    starts = ends - counts
    first_blk = starts // EXPERT_BLOCK
    last_blk = jnp.maximum(ends - 1, starts) // EXPERT_BLOCK
    n_e = jnp.where(counts > 0, last_blk - first_blk + 1, 0)
    item_end = jnp.cumsum(n_e)
    item_start = item_end - n_e
    ids = jnp.arange(n_items, dtype=jnp.int32)
    total = item_end[-1]
    ids_c = jnp.minimum(ids, total - 1)
    e = jnp.sum((item_end[None, :] <= ids_c[:, None]).astype(jnp.int32), axis=1)
    onehot = (e[:, None] == jnp.arange(N_EXPERTS, dtype=jnp.int32)[None, :]).astype(jnp.int32)
    pick = lambda v: jnp.sum(onehot * v[None, :], axis=1)
    blk = pick(first_blk) + ids_c - pick(item_start)
    lo = jnp.clip(pick(starts) - blk * EXPERT_BLOCK, 0, EXPERT_BLOCK)
    hi = jnp.clip(pick(ends) - blk * EXPERT_BLOCK, 0, EXPERT_BLOCK)
    hi = jnp.where(ids < total, hi, lo)
    prev_blk = jnp.concatenate([jnp.full((1,), -1, jnp.int32), blk[:-1]])
    first = (blk != prev_blk).astype(jnp.int32)
    return blk, e, lo, hi, first


def kernel(x, c, w_ada, b_ada, w_in, w_pool, pool_scale, conv_w, conv_b, dt_bias, a_log, d_skip,
           ssd_norm_w, w_out, w_group, b_group, w_router, b_router, w13, w2, final_norm_w):
    b, s, d = x.shape
    depth = w_ada.shape[0]
    t = b * s
    pool_w = w_pool.shape[1] * w_pool.shape[2]
    ssd_w = SSD_HEADS * SSD_HEAD_DIM
    conv_dim = conv_w.shape[2]
    assert s % SEQ_TILE == 0 and SEQ_TILE % SSD_CHUNK == 0
    assert t % RANK_TILE == 0 and t % DISPATCH_TILE == 0 and s % FINAL_TILE == 0
    assert (t * TOP_K) % EXPERT_BLOCK == 0 and d % LANES == 0
    assert pool_w == 2 * LANES and len(POOL_WINDOWS) * POOL_GROUP_DIM == pool_w
    assert N_EXPERT_GROUPS + N_EXPERTS <= LANES and SSD_HEADS <= LANES
    assert depth == 1, "the final RMSNorm is fused into the last layer's combine step"

    for layer in range(depth):
        mod = _ada(c, w_ada[layer], b_ada[layer]).reshape(b, 6, d)

        o_dt = pool_w + ssd_w + conv_dim
        win = jnp.concatenate([w_in[layer][:, :o_dt], _pad_lanes(w_in[layer][:, o_dt:])], axis=1).astype(BF16)
        wp = jnp.zeros((pool_w, pool_w), F32)
        for g in range(len(POOL_WINDOWS)):
            sl = slice(g * POOL_GROUP_DIM, (g + 1) * POOL_GROUP_DIM)
            wp = wp.at[sl, sl].set(w_pool[layer, g])
        wp = wp.astype(BF16)
        w_route = _pad_lanes(jnp.concatenate([w_group[layer], w_router[layer]], axis=1))
        wrh = w_route.astype(BF16)
        wrl = (w_route - wrh.astype(F32)).astype(BF16)
        br = _pad_lanes(jnp.concatenate([b_group[layer], b_router[layer]])[None, :])

        x1, h2, route = _mixer(
            x, mod, win, wp, pool_scale[layer][None, :], conv_w[layer], conv_b[layer][None, :],
            _pad_lanes(dt_bias[layer][None, :]), _pad_lanes(a_log[layer][None, :], fill=NEG_BIG),
            jnp.repeat(d_skip[layer], SSD_HEAD_DIM)[None, :], ssd_norm_w[layer][None, :],
            w_out[layer].astype(BF16), wrh, wrl, br, pool_w=pool_w, ssd_w=ssd_w)

        counts_f = _count(route)
        start_f = jnp.cumsum(counts_f, axis=1) - counts_f
        dest = _rank(route, start_f)[:, :TOP_K].astype(jnp.int32)
        counts = counts_f[0, :N_EXPERTS].astype(jnp.int32)
        items = _work_items(counts, (t * TOP_K) // EXPERT_BLOCK)

        xs = _dispatch(dest, h2)
        ys = _experts(*items, xs, w13[layer].astype(BF16), w2[layer].astype(BF16))
        x = _final(dest, x1, route, mod, final_norm_w[None, :], ys)
    return x
```

```python
import functools

import jax
import jax.numpy as jnp
from jax import lax
from jax.experimental import pallas as pl
from jax.experimental.pallas import tpu as pltpu

POOL_WINDOWS = (2, 4, 8, 16)
POOL_GROUP_DIM = 64
SSD_HEAD_DIM = 64
SSD_GROUPS = 4
SSD_HEADS_PER_GROUP = 3
SSD_HEADS = SSD_GROUPS * SSD_HEADS_PER_GROUP
SSD_STATE = 128
SSD_CONV = 4
SSD_CHUNK = 128
N_EXPERT_GROUPS = 4
EXPERTS_PER_GROUP = 8
N_EXPERTS = N_EXPERT_GROUPS * EXPERTS_PER_GROUP
TOP_K = 2
NORM_EPS = 1e-6

LANES = 128
SUBLANES = 8
VMEM_LIMIT_BYTES = 56 * 1024 * 1024

SEQ_TILE = 256
CONV_HALO = SUBLANES
POOL_HALO = 16
RANK_TILE = 512
DISPATCH_TILE = 1024
EXPERT_BLOCK = 256
FINAL_TILE = 256
DMA_UNROLL = 8

NEG_BIG = -1e30
F32 = jnp.float32
BF16 = jnp.bfloat16


def _silu(v):
    return v * (1.0 / (1.0 + jnp.exp(-v)))


def _softplus(v):
    return jnp.maximum(v, 0.0) + jnp.log1p(jnp.exp(-jnp.abs(v)))


def _rms(v):
    return v * lax.rsqrt(jnp.mean(v * v, axis=-1, keepdims=True) + NORM_EPS)


def _dot(a, b):
    return jnp.dot(a, b, preferred_element_type=F32)


def _dot_exact(a, b):
    return jnp.dot(a, b, preferred_element_type=F32, precision=lax.Precision.HIGHEST)


def _slab_chunks(ref, rows, nrow):
    return [ref[pl.ds(c, rows, stride=nrow), :] for c in range(nrow)]


def _slab_store(ref, val, nrow):
    rows = val.shape[0]
    for c in range(nrow):
        ref[pl.ds(c, rows, stride=nrow), :] = val[:, c * LANES:(c + 1) * LANES]


def _slab_of_row(ref, row, nrow):
    return ref.at[pl.ds(pl.multiple_of(row * nrow, nrow), nrow)]


def _ada_kernel(c_ref, w_ref, b_ref, o_ref):
    o_ref[...] = _dot_exact(_silu(c_ref[...]), w_ref[...]) + b_ref[...]


def _ada(c, w_ada, b_ada):
    b, d = c.shape
    n = w_ada.shape[1]
    return pl.pallas_call(
        _ada_kernel,
        grid=(n // d,),
        in_specs=[pl.BlockSpec((b, d), lambda i: (0, 0)),
                  pl.BlockSpec((d, d), lambda i: (0, i)),
                  pl.BlockSpec((1, d), lambda i: (0, i))],
        out_specs=pl.BlockSpec((b, d), lambda i: (0, i)),
        out_shape=jax.ShapeDtypeStruct((b, n), F32),
        compiler_params=pltpu.CompilerParams(dimension_semantics=("arbitrary",),
                                             vmem_limit_bytes=VMEM_LIMIT_BYTES),
        name="ada",
    )(c, w_ada, b_ada.reshape(1, n))


def _mixer_kernel(x_ref, mod_ref, win_ref, wp_ref, ps_ref, cw_ref, cb_ref, dtb_ref, alog_ref,
                  dsk_ref, nw_ref, wout_ref, wrh_ref, wrl_ref, br_ref,
                  x1_ref, h2_ref, route_ref,
                  state_ref, xext_ref, uext_ref, *, pool_w, ssd_w):
    ts = x_ref.shape[1]
    d = x_ref.shape[2]
    L = SSD_CHUNK
    N = SSD_STATE
    P = SSD_HEAD_DIM
    j = pl.program_id(1)

    @pl.when(j == 0)
    def _():
        state_ref[...] = jnp.zeros_like(state_ref)
        xext_ref[0:CONV_HALO, :] = jnp.zeros((CONV_HALO, xext_ref.shape[1]), F32)
        uext_ref[0:POOL_HALO, :] = jnp.zeros((POOL_HALO, uext_ref.shape[1]), F32)

    x = x_ref[0]
    mod = mod_ref[0]
    sh1, sc1, g1 = mod[0:1], mod[1:2], mod[2:3]
    sh2, sc2 = mod[3:4], mod[4:5]

    h = _rms(x) * (1.0 + sc1) + sh1
    proj = _dot(h.astype(BF16), win_ref[...])
    o_z = pool_w
    o_x = pool_w + ssd_w
    conv_dim = ssd_w + 2 * SSD_GROUPS * N
    o_dt = o_x + conv_dim
    u = proj[:, 0:pool_w]
    z = proj[:, o_z:o_x]
    xbc = proj[:, o_x:o_dt]
    dtr = proj[:, o_dt:o_dt + LANES]

    uext_ref[POOL_HALO:POOL_HALO + ts, :] = u
    lane_u = lax.broadcasted_iota(jnp.int32, (1, LANES), 1)
    upper = lane_u >= POOL_GROUP_DIM

    def ush(k, col):
        return uext_ref[POOL_HALO - k:POOL_HALO - k + ts, col * LANES:(col + 1) * LANES]

    w0, w1, w2, w3 = POOL_WINDOWS
    lo_a = ush(0, 0)
    for k in range(1, w0):
        lo_a = lo_a + ush(k, 0)
    lo_b = ush(w0, 0)
    for k in range(w0 + 1, w1):
        lo_b = lo_b + ush(k, 0)
    hi_a = ush(0, 1)
    for k in range(1, w2):
        hi_a = hi_a + ush(k, 1)
    hi_b = ush(w2, 1)
    for k in range(w2 + 1, w3):
        hi_b = hi_b + ush(k, 1)
    win_lo = lo_a + jnp.where(upper, lo_b, 0.0)
    win_hi = hi_a + jnp.where(upper, hi_b, 0.0)
    pos = (j * ts + 1 + lax.broadcasted_iota(jnp.int32, (ts, 1), 0)).astype(F32)
    cnt_lo = jnp.minimum(pos, jnp.where(upper, float(w1), float(w0)))
    cnt_hi = jnp.minimum(pos, jnp.where(upper, float(w3), float(w2)))
    pdiff = jnp.concatenate([win_lo / cnt_lo, win_hi / cnt_hi], axis=1) - u
    y_pool = _dot(pdiff.astype(BF16), wp_ref[...]) * ps_ref[...]
    uext_ref[0:POOL_HALO, :] = uext_ref[ts:ts + POOL_HALO, :]

    xext_ref[CONV_HALO:CONV_HALO + ts, :] = xbc
    cw = cw_ref[...]
    conv = cb_ref[...] + cw[SSD_CONV - 1:SSD_CONV] * xbc
    for k in range(SSD_CONV - 1):
        off = CONV_HALO - (SSD_CONV - 1) + k
        conv = conv + cw[k:k + 1] * xext_ref[off:off + ts, :]
    xa = _silu(conv)
    xext_ref[0:CONV_HALO, :] = xext_ref[ts:ts + CONV_HALO, :]

    a_neg = -jnp.exp(alog_ref[...])
    row_i = lax.broadcasted_iota(jnp.int32, (L, L), 0)
    col_i = lax.broadcasted_iota(jnp.int32, (L, L), 1)
    causal = row_i >= col_i
    tril = causal.astype(F32)
    y_chunks = []
    for c in range(ts // L):
        r0 = c * L
        xs_c = xa[r0:r0 + L, 0:ssd_w]
        b_c = xa[r0:r0 + L, ssd_w:ssd_w + SSD_GROUPS * N]
        c_c = xa[r0:r0 + L, ssd_w + SSD_GROUPS * N:conv_dim]
        dt = _softplus(dtr[r0:r0 + L] + dtb_ref[...])
        a_cs = _dot_exact(tril, dt * a_neg)
        a_cs_t = a_cs.T
        dt_t = dt.T
        ea = jnp.exp(a_cs)
        y_heads = []
        for g in range(SSD_GROUPS):
            bg = b_c[:, g * N:(g + 1) * N]
            cg = c_c[:, g * N:(g + 1) * N]
            cb = lax.dot_general(cg.astype(BF16), bg.astype(BF16), (((1,), (1,)), ((), ())),
                                 preferred_element_type=F32)
            bg_t = bg.T
            for r in range(SSD_HEADS_PER_GROUP):
                hd = g * SSD_HEADS_PER_GROUP + r
                a_col = a_cs[:, hd:hd + 1]
                a_row = a_cs_t[hd:hd + 1, :]
                dt_row = dt_t[hd:hd + 1, :]
                decay = jnp.exp(jnp.where(causal, a_col - a_row, NEG_BIG))
                m = cb * decay * dt_row
                cs = cg * ea[:, hd:hd + 1]
                lhs = jnp.concatenate([m, cs], axis=1).astype(BF16)
                xs_h = xs_c[:, hd * P:(hd + 1) * P].astype(BF16)
                st = state_ref[hd]
                rhs = jnp.concatenate([xs_h, st.astype(BF16)], axis=0)
                y_heads.append(_dot(lhs, rhs))
                a_end = a_row[:, L - 1:L]
                w_row = dt_row * jnp.exp(a_end - a_row)
                state_ref[hd] = jnp.exp(a_end) * st + _dot((bg_t * w_row).astype(BF16), xs_h)
        y_chunks.append(jnp.concatenate(y_heads, axis=1))
    y = jnp.concatenate(y_chunks, axis=0) if len(y_chunks) > 1 else y_chunks[0]
    y = y + xa[:, 0:ssd_w] * dsk_ref[...]
    y = y * _silu(z)

    gw = ssd_w // SSD_GROUPS
    lane_y = lax.broadcasted_iota(jnp.int32, (1, ssd_w), 1)
    y2 = y * y
    scale = jnp.zeros_like(y)
    for g in range(SSD_GROUPS):
        in_g = (lane_y >= g * gw) & (lane_y < (g + 1) * gw)
        ms = jnp.sum(jnp.where(in_g, y2, 0.0), axis=-1, keepdims=True) * (1.0 / gw)
        scale = scale + jnp.where(in_g, lax.rsqrt(ms + NORM_EPS), 0.0)
    y_ssd = y * scale * nw_ref[...]

    mix = _dot(jnp.concatenate([y_pool, y_ssd], axis=1).astype(BF16), wout_ref[...])
    x1 = x + g1 * mix
    x1_ref[0] = x1

    h2 = _rms(x1) * (1.0 + sc2) + sh2
    _slab_store(h2_ref, h2, d // LANES)
    h_hi = h2.astype(BF16)
    h_lo = (h2 - h_hi.astype(F32)).astype(BF16)
    logits = (_dot(h_hi, wrh_ref[...]) + _dot(h_lo, wrh_ref[...]) + _dot(h_hi, wrl_ref[...])
              + br_ref[...])
    lane = lax.broadcasted_iota(jnp.int32, (1, LANES), 1).astype(F32)
    big = float(LANES)
    gl = jnp.where(lane < N_EXPERT_GROUPS, logits, NEG_BIG)
    gmax = jnp.max(gl, axis=-1, keepdims=True)
    gsum = jnp.sum(jnp.exp(gl - gmax), axis=-1, keepdims=True)
    p_g = 1.0 / gsum
    g_idx = jnp.min(jnp.where(gl == gmax, lane, big), axis=-1, keepdims=True)
    lo = N_EXPERT_GROUPS + EXPERTS_PER_GROUP * g_idx
    el = jnp.where((lane >= lo) & (lane < lo + EXPERTS_PER_GROUP), logits, NEG_BIG)
    v1 = jnp.max(el, axis=-1, keepdims=True)
    i1 = jnp.min(jnp.where(el == v1, lane, big), axis=-1, keepdims=True)
    el2 = jnp.where(lane == i1, NEG_BIG, el)
    v2 = jnp.max(el2, axis=-1, keepdims=True)
    i2 = jnp.min(jnp.where(el2 == v2, lane, big), axis=-1, keepdims=True)
    e21 = jnp.exp(v2 - v1)
    gate1 = p_g / (1.0 + e21)
    gate2 = p_g * e21 / (1.0 + e21)
    route = jnp.where(lane == 0, i1 - N_EXPERT_GROUPS,
                      jnp.where(lane == 1, i2 - N_EXPERT_GROUPS,
                                jnp.where(lane == 2, gate1, jnp.where(lane == 3, gate2, 0.0))))
    route_ref[...] = route


def _mixer(x, mod, win, wp, ps, cw, cb, dtb, alog, dsk, nw, wout, wrh, wrl, br, *, pool_w, ssd_w):
    b, s, d = x.shape
    ts = SEQ_TILE
    conv_dim = cw.shape[1]
    nrow = d // LANES

    def full(a):
        nd = a.ndim
        return pl.BlockSpec(a.shape, lambda i, j, _nd=nd: (0,) * _nd)

    tok = lambda i, j: (i * (s // ts) + j, 0)
    return pl.pallas_call(
        functools.partial(_mixer_kernel, pool_w=pool_w, ssd_w=ssd_w),
        grid=(b, s // ts),
        in_specs=[pl.BlockSpec((1, ts, d), lambda i, j: (i, j, 0)),
                  pl.BlockSpec((1, 6, d), lambda i, j: (i, 0, 0)),
                  full(win), full(wp), full(ps), full(cw), full(cb), full(dtb), full(alog),
                  full(dsk), full(nw), full(wout), full(wrh), full(wrl), full(br)],
        out_specs=[pl.BlockSpec((1, ts, d), lambda i, j: (i, j, 0)),
                   pl.BlockSpec((ts * nrow, LANES), tok),
                   pl.BlockSpec((ts, LANES), tok)],
        out_shape=[jax.ShapeDtypeStruct((b, s, d), F32),
                   jax.ShapeDtypeStruct((b * s * nrow, LANES), F32),
                   jax.ShapeDtypeStruct((b * s, LANES), F32)],
        scratch_shapes=[pltpu.VMEM((SSD_HEADS, SSD_STATE, SSD_HEAD_DIM), F32),
                        pltpu.VMEM((CONV_HALO + ts, conv_dim), F32),
                        pltpu.VMEM((POOL_HALO + ts, pool_w), F32)],
        compiler_params=pltpu.CompilerParams(dimension_semantics=("arbitrary", "arbitrary"),
                                             vmem_limit_bytes=VMEM_LIMIT_BYTES),
        name="mixer",
    )(x, mod, win, wp, ps, cw, cb, dtb, alog, dsk, nw, wout, wrh, wrl, br)


def _onehots(route):
    lane = lax.broadcasted_iota(jnp.int32, (1, LANES), 1).astype(F32)
    oh1 = jnp.where(route[:, 0:1] == lane, 1.0, 0.0)
    oh2 = jnp.where(route[:, 1:2] == lane, 1.0, 0.0)
    return oh1, oh2


def _count_kernel(route_ref, cnt_ref):
    @pl.when(pl.program_id(0) == 0)
    def _():
        cnt_ref[...] = jnp.zeros_like(cnt_ref)

    oh1, oh2 = _onehots(route_ref[...])
    cnt_ref[...] += jnp.sum(oh1 + oh2, axis=0, keepdims=True)


def _count(route):
    t = route.shape[0]
    return pl.pallas_call(
        _count_kernel,
        grid=(t // RANK_TILE,),
        in_specs=[pl.BlockSpec((RANK_TILE, LANES), lambda i: (i, 0))],
        out_specs=pl.BlockSpec((1, LANES), lambda i: (0, 0)),
        out_shape=jax.ShapeDtypeStruct((1, LANES), F32),
        compiler_params=pltpu.CompilerParams(dimension_semantics=("arbitrary",)),
        name="count",
    )(route)


def _rank_kernel(route_ref, start_ref, dest_ref, carry_ref):
    ta = route_ref.shape[0]

    @pl.when(pl.program_id(0) == 0)
    def _():
        carry_ref[...] = start_ref[...]

    oh1, oh2 = _onehots(route_ref[...])
    oh = oh1 + oh2
    earlier = (lax.broadcasted_iota(jnp.int32, (ta, ta), 0)
               > lax.broadcasted_iota(jnp.int32, (ta, ta), 1))
    before = _dot(jnp.where(earlier, 1.0, 0.0).astype(BF16), oh.astype(BF16))
    base = carry_ref[...] + before
    d1 = jnp.sum(oh1 * base, axis=-1, keepdims=True)
    d2 = jnp.sum(oh2 * base, axis=-1, keepdims=True)
    lane = lax.broadcasted_iota(jnp.int32, (1, LANES), 1)
    dest_ref[...] = jnp.where(lane == 0, d1, jnp.where(lane == 1, d2, 0.0))
    carry_ref[...] += jnp.sum(oh, axis=0, keepdims=True)


def _rank(route, start):
    t = route.shape[0]
    return pl.pallas_call(
        _rank_kernel,
        grid=(t // RANK_TILE,),
        in_specs=[pl.BlockSpec((RANK_TILE, LANES), lambda i: (i, 0)),
                  pl.BlockSpec((1, LANES), lambda i: (0, 0))],
        out_specs=pl.BlockSpec((RANK_TILE, LANES), lambda i: (i, 0)),
        out_shape=jax.ShapeDtypeStruct((t, LANES), F32),
        scratch_shapes=[pltpu.VMEM((1, LANES), F32)],
        compiler_params=pltpu.CompilerParams(dimension_semantics=("arbitrary",)),
        name="rank",
    )(route, start)


def _dispatch_kernel(dest_ref, h2_ref, xs_hbm, sem, *, nrow):
    n = dest_ref.shape[2]
    tile_rows = h2_ref.shape[0]

    def issue(blk, carry):
        for u in range(DMA_UNROLL):
            tk = blk * DMA_UNROLL + u
            for k in range(TOP_K):
                pltpu.make_async_copy(_slab_of_row(h2_ref, tk, nrow),
                                      _slab_of_row(xs_hbm, dest_ref[0, 0, tk * TOP_K + k], nrow),
                                      sem).start()
        return carry

    lax.fori_loop(0, n // (TOP_K * DMA_UNROLL), issue, 0)

    for _ in range(TOP_K):
        pltpu.make_async_copy(h2_ref, xs_hbm.at[pl.ds(0, tile_rows)], sem).wait()


def _dispatch(dest, h2, nrow):
    t = h2.shape[0] // nrow
    n = DISPATCH_TILE * TOP_K
    return pl.pallas_call(
        functools.partial(_dispatch_kernel, nrow=nrow),
        grid=(t // DISPATCH_TILE,),
        in_specs=[pl.BlockSpec((1, 1, n), lambda i: (i, 0, 0), memory_space=pltpu.SMEM),
                  pl.BlockSpec((DISPATCH_TILE * nrow, LANES), lambda i: (i, 0))],
        out_specs=pl.BlockSpec(memory_space=pl.ANY),
        out_shape=jax.ShapeDtypeStruct((t * TOP_K * nrow, LANES), F32),
        scratch_shapes=[pltpu.SemaphoreType.DMA(())],
        compiler_params=pltpu.CompilerParams(dimension_semantics=("arbitrary",),
                                             has_side_effects=True),
        name="dispatch",
    )(dest.reshape(t // DISPATCH_TILE, 1, n), h2)


def _expert_kernel(blk_ref, exp_ref, lo_ref, hi_ref, first_ref, xs_ref, w13_ref, w2_ref, ys_ref, *, nrow):
    i = pl.program_id(0)
    rows = xs_ref.shape[0] // nrow
    lo = lo_ref[i]
    hi = hi_ref[i]

    @pl.when(hi > lo)
    def _():
        xb = jnp.concatenate([ch.astype(BF16) for ch in _slab_chunks(xs_ref, rows, nrow)], axis=1)
        hu = _dot(xb, w13_ref[0])
        f = hu.shape[1] // 2
        act = _silu(hu[:, :f]) * hu[:, f:]
        y = _dot(act.astype(BF16), w2_ref[0])

        @pl.when(first_ref[i] == 1)
        def _():
            _slab_store(ys_ref, y, nrow)

        @pl.when(first_ref[i] == 0)
        def _():
            ridx = lax.broadcasted_iota(jnp.int32, (rows, 1), 0)
            mine = (ridx >= lo) & (ridx < hi)
            old = jnp.concatenate(_slab_chunks(ys_ref, rows, nrow), axis=1)
            _slab_store(ys_ref, jnp.where(mine, y, old), nrow)


def _experts(item_blk, item_exp, item_lo, item_hi, item_first, xs, w13, w2, nrow):
    a = xs.shape[0] // nrow
    n_items = item_blk.shape[0]
    d = w13.shape[1]
    ff2 = w13.shape[2]
    grid_spec = pltpu.PrefetchScalarGridSpec(
        num_scalar_prefetch=5,
        grid=(n_items,),
        in_specs=[pl.BlockSpec((EXPERT_BLOCK * nrow, LANES), lambda i, b, e, lo, hi, fr: (b[i], 0)),
                  pl.BlockSpec((1, d, ff2), lambda i, b, e, lo, hi, fr: (e[i], 0, 0)),
                  pl.BlockSpec((1, ff2 // 2, d), lambda i, b, e, lo, hi, fr: (e[i], 0, 0))],
        out_specs=pl.BlockSpec((EXPERT_BLOCK * nrow, LANES), lambda i, b, e, lo, hi, fr: (b[i], 0)),
    )
    return pl.pallas_call(
        functools.partial(_expert_kernel, nrow=nrow),
        grid_spec=grid_spec,
        out_shape=jax.ShapeDtypeStruct((a * nrow, LANES), F32),
        compiler_params=pltpu.CompilerParams(dimension_semantics=("arbitrary",),
                                             vmem_limit_bytes=VMEM_LIMIT_BYTES),
        name="experts",
    )(item_blk, item_exp, item_lo, item_hi, item_first, xs, w13, w2)


def _final_kernel(dest_ref, x1_ref, route_ref, mod_ref, fw_ref, ys_hbm, o_ref, buf1, buf2, sem):
    tf = x1_ref.shape[1]
    nrow = buf1.shape[0] // tf

    def issue(blk, carry):
        for u in range(DMA_UNROLL):
            tk = blk * DMA_UNROLL + u
            for k, buf in enumerate((buf1, buf2)):
                pltpu.make_async_copy(_slab_of_row(ys_hbm, dest_ref[0, 0, tk * TOP_K + k], nrow),
                                      _slab_of_row(buf, tk, nrow), sem).start()
        return carry

    lax.fori_loop(0, tf // DMA_UNROLL, issue, 0)

    pltpu.make_async_copy(ys_hbm.at[pl.ds(0, tf * nrow)], buf1, sem).wait()
    pltpu.make_async_copy(ys_hbm.at[pl.ds(0, tf * nrow)], buf2, sem).wait()

    r = route_ref[...]
    gate1 = r[:, 2:3]
    gate2 = r[:, 3:4]
    y = jnp.concatenate([c1 * gate1 + c2 * gate2 for c1, c2 in
                         zip(_slab_chunks(buf1, tf, nrow), _slab_chunks(buf2, tf, nrow))], axis=1)
    g2 = mod_ref[0][5:6]
    o_ref[0] = _rms(x1_ref[0] + g2 * y) * fw_ref[...]


def _final(dest, x1, route, mod, fw, ys):
    b, s, d = x1.shape
    tf = FINAL_TILE
    nrow = d // LANES
    spt = s // tf
    n = tf * TOP_K
    return pl.pallas_call(
        _final_kernel,
        grid=(b, spt),
        in_specs=[pl.BlockSpec((1, 1, n), lambda i, j: (i * spt + j, 0, 0), memory_space=pltpu.SMEM),
                  pl.BlockSpec((1, tf, d), lambda i, j: (i, j, 0)),
                  pl.BlockSpec((tf, LANES), lambda i, j: (i * spt + j, 0)),
                  pl.BlockSpec((1, 6, d), lambda i, j: (i, 0, 0)),
                  pl.BlockSpec((1, d), lambda i, j: (0, 0)),
                  pl.BlockSpec(memory_space=pl.ANY)],
        out_specs=pl.BlockSpec((1, tf, d), lambda i, j: (i, j, 0)),
        out_shape=jax.ShapeDtypeStruct((b, s, d), F32),
        scratch_shapes=[pltpu.VMEM((tf * nrow, LANES), F32),
                        pltpu.VMEM((tf * nrow, LANES), F32),
                        pltpu.SemaphoreType.DMA(())],
        compiler_params=pltpu.CompilerParams(dimension_semantics=("arbitrary", "arbitrary"),
                                             vmem_limit_bytes=VMEM_LIMIT_BYTES),
        name="final",
    )(dest.reshape(b * spt, 1, n), x1, route, mod, fw, ys)


def _pad_lanes(v, fill=0.0):
    n = v.shape[-1]
    return jnp.pad(v, [(0, 0)] * (v.ndim - 1) + [(0, LANES - n)], constant_values=fill)


def _work_items(counts, n_blocks):
    n_items = n_blocks + N_EXPERTS - 1
    ends = jnp.cumsum(counts)
    starts = ends - counts
    first_blk = starts // EXPERT_BLOCK
    last_blk = jnp.maximum(ends - 1, starts) // EXPERT_BLOCK
    n_e = jnp.where(counts > 0, last_blk - first_blk + 1, 0)
    item_end = jnp.cumsum(n_e)
    item_start = item_end - n_e
    ids = jnp.arange(n_items, dtype=jnp.int32)
    total = item_end[-1]
    ids_c = jnp.minimum(ids, total - 1)
    e = jnp.sum((item_end[None, :] <= ids_c[:, None]).astype(jnp.int32), axis=1)
    onehot = (e[:, None] == jnp.arange(N_EXPERTS, dtype=jnp.int32)[None, :]).astype(jnp.int32)
    pick = lambda v: jnp.sum(onehot * v[None, :], axis=1)
    blk = pick(first_blk) + ids_c - pick(item_start)
    lo = jnp.clip(pick(starts) - blk * EXPERT_BLOCK, 0, EXPERT_BLOCK)
    hi = jnp.clip(pick(ends) - blk * EXPERT_BLOCK, 0, EXPERT_BLOCK)
    hi = jnp.where(ids < total, hi, lo)
    prev_blk = jnp.concatenate([jnp.full((1,), -1, jnp.int32), blk[:-1]])
    first = (blk != prev_blk).astype(jnp.int32)
    return blk, e, lo, hi, first


def kernel(x, c, w_ada, b_ada, w_in, w_pool, pool_scale, conv_w, conv_b, dt_bias, a_log, d_skip,
           ssd_norm_w, w_out, w_group, b_group, w_router, b_router, w13, w2, final_norm_w):
    b, s, d = x.shape
    depth = w_ada.shape[0]
    t = b * s
    pool_w = w_pool.shape[1] * w_pool.shape[2]
    ssd_w = SSD_HEADS * SSD_HEAD_DIM
    conv_dim = conv_w.shape[2]
    assert s % SEQ_TILE == 0 and SEQ_TILE % SSD_CHUNK == 0
    assert t % RANK_TILE == 0 and t % DISPATCH_TILE == 0 and s % FINAL_TILE == 0
    assert (t * TOP_K) % EXPERT_BLOCK == 0 and d % LANES == 0
    assert pool_w == 2 * LANES and len(POOL_WINDOWS) * POOL_GROUP_DIM == pool_w
    assert N_EXPERT_GROUPS + N_EXPERTS <= LANES and SSD_HEADS <= LANES
    assert depth == 1, "the final RMSNorm is fused into the last layer's combine step"

    for layer in range(depth):
        mod = _ada(c, w_ada[layer], b_ada[layer]).reshape(b, 6, d)

        o_dt = pool_w + ssd_w + conv_dim
        win = jnp.concatenate([w_in[layer][:, :o_dt], _pad_lanes(w_in[layer][:, o_dt:])], axis=1).astype(BF16)
        wp = jnp.zeros((pool_w, pool_w), F32)
        for g in range(len(POOL_WINDOWS)):
            sl = slice(g * POOL_GROUP_DIM, (g + 1) * POOL_GROUP_DIM)
            wp = wp.at[sl, sl].set(w_pool[layer, g])
        wp = wp.astype(BF16)
        w_route = _pad_lanes(jnp.concatenate([w_group[layer], w_router[layer]], axis=1))
        wrh = w_route.astype(BF16)
        wrl = (w_route - wrh.astype(F32)).astype(BF16)
        br = _pad_lanes(jnp.concatenate([b_group[layer], b_router[layer]])[None, :])

        x1, h2, route = _mixer(
            x, mod, win, wp, pool_scale[layer][None, :], conv_w[layer], conv_b[layer][None, :],
            _pad_lanes(dt_bias[layer][None, :]), _pad_lanes(a_log[layer][None, :], fill=NEG_BIG),
            jnp.repeat(d_skip[layer], SSD_HEAD_DIM)[None, :], ssd_norm_w[layer][None, :],
            w_out[layer].astype(BF16), wrh, wrl, br, pool_w=pool_w, ssd_w=ssd_w)

        counts_f = _count(route)
        start_f = jnp.cumsum(counts_f, axis=1) - counts_f
        dest = _rank(route, start_f)[:, :TOP_K].astype(jnp.int32)
        counts = counts_f[0, :N_EXPERTS].astype(jnp.int32)
        items = _work_items(counts, (t * TOP_K) // EXPERT_BLOCK)

        nrow = d // LANES
        xs = _dispatch(dest, h2, nrow)
        ys = _experts(*items, xs, w13[layer].astype(BF16), w2[layer].astype(BF16), nrow)
        x = _final(dest, x1, route, mod, final_norm_w[None, :], ys)
    return x
```

```python
import functools

import jax
import jax.numpy as jnp
from jax import lax
from jax.experimental import pallas as pl
from jax.experimental.pallas import tpu as pltpu

POOL_WINDOWS = (2, 4, 8, 16)
POOL_GROUP_DIM = 64
SSD_HEAD_DIM = 64
SSD_GROUPS = 4
SSD_HEADS_PER_GROUP = 3
SSD_HEADS = SSD_GROUPS * SSD_HEADS_PER_GROUP
SSD_STATE = 128
SSD_CONV = 4
SSD_CHUNK = 128
N_EXPERT_GROUPS = 4
EXPERTS_PER_GROUP = 8
N_EXPERTS = N_EXPERT_GROUPS * EXPERTS_PER_GROUP
TOP_K = 2
NORM_EPS = 1e-6

LANES = 128
SUBLANES = 8
VMEM_LIMIT_BYTES = 56 * 1024 * 1024

SEQ_TILE = 256
MIXER_SEQS = 2
CONV_HALO = SUBLANES
POOL_HALO = 16
RANK_TILE = 512
DISPATCH_TILE = 1024
EXPERT_BLOCK = 256
FINAL_TILE = 512
DMA_UNROLL = 8

NEG_BIG = -1e30
F32 = jnp.float32
BF16 = jnp.bfloat16


def _silu(v):
    half = 0.5 * v
    return half + half * jnp.tanh(half)


def _softplus(v):
    return jnp.maximum(v, 0.0) + jnp.log(1.0 + jnp.exp(-jnp.abs(v)))


def _rms(v):
    return v * lax.rsqrt(jnp.mean(v * v, axis=-1, keepdims=True) + NORM_EPS)


def _dot(a, b):
    return jnp.dot(a, b, preferred_element_type=F32)


def _dot_exact(a, b):
    return jnp.dot(a, b, preferred_element_type=F32, precision=lax.Precision.HIGHEST)


def _slab_chunks(ref, rows, nrow):
    return [ref[pl.ds(c, rows, stride=nrow), :] for c in range(nrow)]


def _slab_store(ref, val, nrow):
    rows = val.shape[0]
    for c in range(nrow):
        ref[pl.ds(c, rows, stride=nrow), :] = val[:, c * LANES:(c + 1) * LANES]


def _slab_of_row(ref, row, nrow):
    return ref.at[pl.ds(pl.multiple_of(row * nrow, nrow), nrow)]


def _ada_kernel(c_ref, w_ref, b_ref, o_ref):
    o_ref[...] = _dot_exact(_silu(c_ref[...]), w_ref[...]) + b_ref[...]


def _ada(c, w_ada, b_ada):
    b, d = c.shape
    n = w_ada.shape[1]
    return pl.pallas_call(
        _ada_kernel,
        grid=(n // d,),
        in_specs=[pl.BlockSpec((b, d), lambda i: (0, 0)),
                  pl.BlockSpec((d, d), lambda i: (0, i)),
                  pl.BlockSpec((1, d), lambda i: (0, i))],
        out_specs=pl.BlockSpec((b, d), lambda i: (0, i)),
        out_shape=jax.ShapeDtypeStruct((b, n), F32),
        compiler_params=pltpu.CompilerParams(dimension_semantics=("arbitrary",),
                                             vmem_limit_bytes=VMEM_LIMIT_BYTES),
        name="ada",
    )(c, w_ada, b_ada.reshape(1, n))


def _mixer_kernel(x_ref, mod_ref, win_ref, wp_ref, ps_ref, cw_ref, cb_ref, dtb_ref, alog_ref,
                  dsk_ref, nw_ref, wout_ref, wrh_ref, wrl_ref, br_ref, rep_ref,
                  x1_ref, h2_ref, route_ref,
                  state_ref, xext_ref, uext_ref, *, pool_w, ssd_w):
    @pl.when(pl.program_id(1) == 0)
    def _():
        state_ref[...] = jnp.zeros_like(state_ref)
        xext_ref[:, 0:CONV_HALO, :] = jnp.zeros((xext_ref.shape[0], CONV_HALO, xext_ref.shape[2]), F32)
        uext_ref[:, 0:POOL_HALO, :] = jnp.zeros((uext_ref.shape[0], POOL_HALO, uext_ref.shape[2]), F32)

    chains = [_mixer_tile(x_ref.at[bi], mod_ref.at[bi], win_ref, wp_ref, ps_ref, cw_ref, cb_ref, dtb_ref,
                          alog_ref, dsk_ref, nw_ref, wout_ref, wrh_ref, wrl_ref, br_ref, rep_ref,
                          x1_ref.at[bi], h2_ref.at[bi], route_ref.at[bi],
                          state_ref.at[bi], xext_ref.at[bi], uext_ref.at[bi], pool_w=pool_w, ssd_w=ssd_w)
              for bi in range(x_ref.shape[0])]
    live = [False] * len(chains)
    started = 0
    while started < len(chains) or any(live):
        if started < len(chains):
            live[started] = True
            started += 1
        for k, chain in enumerate(chains):
            if live[k]:
                try:
                    next(chain)
                except StopIteration:
                    live[k] = False


def _mixer_tile(x_ref, mod_ref, win_ref, wp_ref, ps_ref, cw_ref, cb_ref, dtb_ref, alog_ref,
                dsk_ref, nw_ref, wout_ref, wrh_ref, wrl_ref, br_ref, rep_ref,
                x1_ref, h2_ref, route_ref,
                state_ref, xext_ref, uext_ref, *, pool_w, ssd_w):
    ts = x_ref.shape[0]
    d = x_ref.shape[1]
    L = SSD_CHUNK
    N = SSD_STATE
    P = SSD_HEAD_DIM
    j = pl.program_id(1)

    x = x_ref[...]
    mod = mod_ref[...]
    sh1, sc1, g1 = mod[0:1], mod[1:2], mod[2:3]
    sh2, sc2 = mod[3:4], mod[4:5]

    h = (_rms(x) * (1.0 + sc1) + sh1).astype(BF16)
    yield
    proj = _dot(h, win_ref[...])
    yield
    o_z = pool_w
    o_x = pool_w + ssd_w
    conv_dim = ssd_w + 2 * SSD_GROUPS * N
    o_dt = o_x + conv_dim
    u = proj[:, 0:pool_w]
    z = proj[:, o_z:o_x]
    xbc = proj[:, o_x:o_dt]
    dtr = proj[:, o_dt:o_dt + LANES]

    uext_ref[POOL_HALO:POOL_HALO + ts, :] = u
    lane_u = lax.broadcasted_iota(jnp.int32, (1, LANES), 1)
    upper = lane_u >= POOL_GROUP_DIM

    def ush(k, col):
        return uext_ref[POOL_HALO - k:POOL_HALO - k + ts, col * LANES:(col + 1) * LANES]

    w0, w1, w2, w3 = POOL_WINDOWS
    lo_a = ush(0, 0)
    for k in range(1, w0):
        lo_a = lo_a + ush(k, 0)
    lo_b = ush(w0, 0)
    for k in range(w0 + 1, w1):
        lo_b = lo_b + ush(k, 0)
    hi_a = ush(0, 1)
    for k in range(1, w2):
        hi_a = hi_a + ush(k, 1)
    hi_b = ush(w2, 1)
    for k in range(w2 + 1, w3):
        hi_b = hi_b + ush(k, 1)
    win_lo = lo_a + jnp.where(upper, lo_b, 0.0)
    win_hi = hi_a + jnp.where(upper, hi_b, 0.0)
    pos = (j * ts + 1 + lax.broadcasted_iota(jnp.int32, (ts, 1), 0)).astype(F32)
    cnt_lo = jnp.minimum(pos, jnp.where(upper, float(w1), float(w0)))
    cnt_hi = jnp.minimum(pos, jnp.where(upper, float(w3), float(w2)))
    pdiff = jnp.concatenate([win_lo / cnt_lo, win_hi / cnt_hi], axis=1) - u
    y_pool = _dot(pdiff.astype(BF16), wp_ref[...]) * ps_ref[...]
    uext_ref[0:POOL_HALO, :] = uext_ref[ts:ts + POOL_HALO, :]

    xext_ref[CONV_HALO:CONV_HALO + ts, :] = xbc
    cw = cw_ref[...]
    xe = xext_ref[...]
    acc = cw[0:1] * xe
    for k in range(1, SSD_CONV):
        acc = cw[k:k + 1] * xe + pltpu.roll(acc, 1, axis=0)
    xa = _silu(cb_ref[...] + acc[CONV_HALO:CONV_HALO + ts])
    xext_ref[0:CONV_HALO, :] = xext_ref[ts:ts + CONV_HALO, :]
    yield

    a_neg = -jnp.exp(alog_ref[...])
    row_i = lax.broadcasted_iota(jnp.int32, (L, L), 0)
    col_i = lax.broadcasted_iota(jnp.int32, (L, L), 1)
    causal = row_i >= col_i
    tril = causal.astype(F32)
    hpt = LANES // P
    lane_h = lax.broadcasted_iota(jnp.int32, (1, LANES), 1)
    y_chunks = []
    for c in range(ts // L):
        r0 = c * L
        xs_c = xa[r0:r0 + L, 0:ssd_w]
        b_c = xa[r0:r0 + L, ssd_w:ssd_w + SSD_GROUPS * N]
        c_c = xa[r0:r0 + L, ssd_w + SSD_GROUPS * N:conv_dim]
        dt = _softplus(dtr[r0:r0 + L] + dtb_ref[...])
        a_cs = _dot_exact(tril, dt * a_neg)
        a_hi = a_cs.astype(BF16)
        a_lo = (a_cs - a_hi.astype(F32)).astype(BF16)
        a_q = a_hi.astype(F32) + a_lo.astype(F32)
        a_rep = _dot(jnp.concatenate([a_hi, a_lo], axis=1), rep_ref[...])
        a_q_t = a_q.T
        dt_t = dt.T
        cbs, cgs, bg_ts = [], [], []
        for g in range(SSD_GROUPS):
            bg = b_c[:, g * N:(g + 1) * N]
            cg = c_c[:, g * N:(g + 1) * N]
            cbs.append(lax.dot_general(cg.astype(BF16), bg.astype(BF16), (((1,), (1,)), ((), ())),
                                       preferred_element_type=F32))
            cgs.append(cg)
            bg_ts.append(bg.T)
        y_tiles = []
        for q in range(SSD_HEADS // hpt):
            xs_q = xs_c[:, q * LANES:(q + 1) * LANES].astype(BF16)
            st = state_ref[q]
            rhs = jnp.concatenate([xs_q, st.astype(BF16)], axis=0)
            y_q = upd_q = keep_q = None
            for hh in range(hpt):
                hd = q * hpt + hh
                g = hd // SSD_HEADS_PER_GROUP
                a_col = a_rep[:, hd * LANES:(hd + 1) * LANES]
                a_row = a_q_t[hd:hd + 1, :]
                dt_row = dt_t[hd:hd + 1, :]
                decay = jnp.exp(jnp.where(causal, a_col - a_row, NEG_BIG))
                m = cbs[g] * decay * dt_row
                cs = cgs[g] * jnp.exp(a_col)
                lhs = jnp.concatenate([m, cs], axis=1).astype(BF16)
                y_h = _dot(lhs, rhs)
                a_end = a_row[:, L - 1:L]
                w_row = dt_row * jnp.exp(a_end - a_row)
                upd_h = _dot((bg_ts[g] * w_row).astype(BF16), xs_q)
                keep_h = jnp.exp(a_end)
                if hh == 0:
                    y_q, upd_q, keep_q = y_h, upd_h, keep_h
                else:
                    mine = (lane_h >= hh * P) & (lane_h < (hh + 1) * P)
                    y_q = jnp.where(mine, y_h, y_q)
                    upd_q = jnp.where(mine, upd_h, upd_q)
                    keep_q = jnp.where(mine, keep_h, keep_q)
            y_tiles.append(y_q)
            state_ref[q] = keep_q * st + upd_q
        y_chunks.append(jnp.concatenate(y_tiles, axis=1))
        yield
    y = jnp.concatenate(y_chunks, axis=0) if len(y_chunks) > 1 else y_chunks[0]
    y = y + xa[:, 0:ssd_w] * dsk_ref[...]
    y = y * _silu(z)

    gw = ssd_w // SSD_GROUPS
    lane_y = lax.broadcasted_iota(jnp.int32, (1, ssd_w), 1)
    y2 = y * y
    scale = jnp.zeros_like(y)
    for g in range(SSD_GROUPS):
        in_g = (lane_y >= g * gw) & (lane_y < (g + 1) * gw)
        ms = jnp.sum(jnp.where(in_g, y2, 0.0), axis=-1, keepdims=True) * (1.0 / gw)
        scale = scale + jnp.where(in_g, lax.rsqrt(ms + NORM_EPS), 0.0)
    mix_in = jnp.concatenate([y_pool, y * scale * nw_ref[...]], axis=1).astype(BF16)
    yield
    mix = _dot(mix_in, wout_ref[...])
    yield
    x1 = x + g1 * mix
    x1_ref[...] = x1

    h2 = _rms(x1) * (1.0 + sc2) + sh2
    _slab_store(h2_ref, h2, d // LANES)
    h_hi = h2.astype(BF16)
    h_lo = (h2 - h_hi.astype(F32)).astype(BF16)
    yield
    logits = (_dot(h_hi, wrh_ref[...]) + _dot(h_lo, wrh_ref[...]) + _dot(h_hi, wrl_ref[...])
              + br_ref[...])
    lane = lax.broadcasted_iota(jnp.int32, (1, LANES), 1).astype(F32)
    big = float(LANES)
    gl = jnp.where(lane < N_EXPERT_GROUPS, logits, NEG_BIG)
    gmax = jnp.max(gl, axis=-1, keepdims=True)
    gsum = jnp.sum(jnp.exp(gl - gmax), axis=-1, keepdims=True)
    p_g = 1.0 / gsum
    g_idx = jnp.min(jnp.where(gl == gmax, lane, big), axis=-1, keepdims=True)
    lo = N_EXPERT_GROUPS + EXPERTS_PER_GROUP * g_idx
    el = jnp.where((lane >= lo) & (lane < lo + EXPERTS_PER_GROUP), logits, NEG_BIG)
    v1 = jnp.max(el, axis=-1, keepdims=True)
    i1 = jnp.min(jnp.where(el == v1, lane, big), axis=-1, keepdims=True)
    el2 = jnp.where(lane == i1, NEG_BIG, el)
    v2 = jnp.max(el2, axis=-1, keepdims=True)
    i2 = jnp.min(jnp.where(el2 == v2, lane, big), axis=-1, keepdims=True)
    e21 = jnp.exp(v2 - v1)
    gate1 = p_g / (1.0 + e21)
    gate2 = p_g * e21 / (1.0 + e21)
    route = jnp.where(lane == 0, i1 - N_EXPERT_GROUPS,
                      jnp.where(lane == 1, i2 - N_EXPERT_GROUPS,
                                jnp.where(lane == 2, gate1, jnp.where(lane == 3, gate2, 0.0))))
    route_ref[...] = route


def _mixer(x, mod, win, wp, ps, cw, cb, dtb, alog, dsk, nw, wout, wrh, wrl, br, *, pool_w, ssd_w):
    k_idx = jnp.arange(2 * LANES, dtype=jnp.int32)[:, None] % LANES
    h_idx = jnp.arange(SSD_HEADS * LANES, dtype=jnp.int32)[None, :] // LANES
    rep = (k_idx == h_idx).astype(BF16)
    b, s, d = x.shape
    ts = SEQ_TILE
    conv_dim = cw.shape[1]
    nrow = d // LANES

    def full(a):
        nd = a.ndim
        return pl.BlockSpec(a.shape, lambda i, j, _nd=nd: (0,) * _nd)

    nb = MIXER_SEQS
    seq_tile = lambda i, j: (i, j, 0)
    x1, h2, route = pl.pallas_call(
        functools.partial(_mixer_kernel, pool_w=pool_w, ssd_w=ssd_w),
        grid=(b // nb, s // ts),
        in_specs=[pl.BlockSpec((nb, ts, d), seq_tile),
                  pl.BlockSpec((nb, 6, d), lambda i, j: (i, 0, 0)),
                  full(win), full(wp), full(ps), full(cw), full(cb), full(dtb), full(alog),
                  full(dsk), full(nw), full(wout), full(wrh), full(wrl), full(br), full(rep)],
        out_specs=[pl.BlockSpec((nb, ts, d), seq_tile),
                   pl.BlockSpec((nb, ts * nrow, LANES), seq_tile),
                   pl.BlockSpec((nb, ts, LANES), seq_tile)],
        out_shape=[jax.ShapeDtypeStruct((b, s, d), F32),
                   jax.ShapeDtypeStruct((b, s * nrow, LANES), F32),
                   jax.ShapeDtypeStruct((b, s, LANES), F32)],
        scratch_shapes=[pltpu.VMEM((nb, ssd_w // LANES, SSD_STATE, LANES), F32),
                        pltpu.VMEM((nb, CONV_HALO + ts, conv_dim), F32),
                        pltpu.VMEM((nb, POOL_HALO + ts, pool_w), F32)],
        compiler_params=pltpu.CompilerParams(dimension_semantics=("arbitrary", "arbitrary"),
                                             vmem_limit_bytes=VMEM_LIMIT_BYTES),
        name="mixer",
    )(x, mod, win, wp, ps, cw, cb, dtb, alog, dsk, nw, wout, wrh, wrl, br, rep)
    return x1, h2.reshape(b * s * nrow, LANES), route.reshape(b * s, LANES)


def _onehots(route):
    lane = lax.broadcasted_iota(jnp.int32, (1, LANES), 1).astype(F32)
    oh1 = jnp.where(route[:, 0:1] == lane, 1.0, 0.0)
    oh2 = jnp.where(route[:, 1:2] == lane, 1.0, 0.0)
    return oh1, oh2


def _count_kernel(route_ref, cnt_ref):
    @pl.when(pl.program_id(0) == 0)
    def _():
        cnt_ref[...] = jnp.zeros_like(cnt_ref)

    oh1, oh2 = _onehots(route_ref[...])
    cnt_ref[...] += jnp.sum(oh1 + oh2, axis=0, keepdims=True)


def _count(route):
    t = route.shape[0]
    return pl.pallas_call(
        _count_kernel,
        grid=(t // RANK_TILE,),
        in_specs=[pl.BlockSpec((RANK_TILE, LANES), lambda i: (i, 0))],
        out_specs=pl.BlockSpec((1, LANES), lambda i: (0, 0)),
        out_shape=jax.ShapeDtypeStruct((1, LANES), F32),
        compiler_params=pltpu.CompilerParams(dimension_semantics=("arbitrary",)),
        name="count",
    )(route)


def _rank_kernel(route_ref, start_ref, dest_ref, carry_ref):
    ta = route_ref.shape[0]

    @pl.when(pl.program_id(0) == 0)
    def _():
        carry_ref[...] = start_ref[...]

    oh1, oh2 = _onehots(route_ref[...])
    oh = oh1 + oh2
    earlier = (lax.broadcasted_iota(jnp.int32, (ta, ta), 0)
               > lax.broadcasted_iota(jnp.int32, (ta, ta), 1))
    before = _dot(jnp.where(earlier, 1.0, 0.0).astype(BF16), oh.astype(BF16))
    base = carry_ref[...] + before
    d1 = jnp.sum(oh1 * base, axis=-1, keepdims=True)
    d2 = jnp.sum(oh2 * base, axis=-1, keepdims=True)
    lane = lax.broadcasted_iota(jnp.int32, (1, LANES), 1)
    dest_ref[...] = jnp.where(lane == 0, d1, jnp.where(lane == 1, d2, 0.0))
    carry_ref[...] += jnp.sum(oh, axis=0, keepdims=True)


def _rank(route, start):
    t = route.shape[0]
    return pl.pallas_call(
        _rank_kernel,
        grid=(t // RANK_TILE,),
        in_specs=[pl.BlockSpec((RANK_TILE, LANES), lambda i: (i, 0)),
                  pl.BlockSpec((1, LANES), lambda i: (0, 0))],
        out_specs=pl.BlockSpec((RANK_TILE, LANES), lambda i: (i, 0)),
        out_shape=jax.ShapeDtypeStruct((t, LANES), F32),
        scratch_shapes=[pltpu.VMEM((1, LANES), F32)],
        compiler_params=pltpu.CompilerParams(dimension_semantics=("arbitrary",)),
        name="rank",
    )(route, start)


def _dispatch_kernel(dest_ref, h2_ref, xs_hbm, sem, *, nrow):
    n = dest_ref.shape[2]
    tile_rows = h2_ref.shape[0]

    def issue(blk, carry):
        for u in range(DMA_UNROLL):
            tk = blk * DMA_UNROLL + u
            for k in range(TOP_K):
                pltpu.make_async_copy(_slab_of_row(h2_ref, tk, nrow),
                                      _slab_of_row(xs_hbm, dest_ref[0, 0, tk * TOP_K + k], nrow),
                                      sem).start()
        return carry

    lax.fori_loop(0, n // (TOP_K * DMA_UNROLL), issue, 0)

    for _ in range(TOP_K):
        pltpu.make_async_copy(h2_ref, xs_hbm.at[pl.ds(0, tile_rows)], sem).wait()


def _dispatch(dest, h2, nrow):
    t = h2.shape[0] // nrow
    n = DISPATCH_TILE * TOP_K
    return pl.pallas_call(
        functools.partial(_dispatch_kernel, nrow=nrow),
        grid=(t // DISPATCH_TILE,),
        in_specs=[pl.BlockSpec((1, 1, n), lambda i: (i, 0, 0), memory_space=pltpu.SMEM),
                  pl.BlockSpec((DISPATCH_TILE * nrow, LANES), lambda i: (i, 0))],
        out_specs=pl.BlockSpec(memory_space=pl.ANY),
        out_shape=jax.ShapeDtypeStruct((t * TOP_K * nrow, LANES), F32),
        scratch_shapes=[pltpu.SemaphoreType.DMA(())],
        compiler_params=pltpu.CompilerParams(dimension_semantics=("arbitrary",),
                                             has_side_effects=True),
        name="dispatch",
    )(dest.reshape(t // DISPATCH_TILE, 1, n), h2)


def _expert_kernel(blk_ref, exp_ref, lo_ref, hi_ref, first_ref, xs_ref, w13_ref, w2_ref, ys_ref, *, nrow):
    i = pl.program_id(0)
    rows = xs_ref.shape[0] // nrow
    lo = lo_ref[i]
    hi = hi_ref[i]

    @pl.when(hi > lo)
    def _():
        xb = jnp.concatenate([ch.astype(BF16) for ch in _slab_chunks(xs_ref, rows, nrow)], axis=1)
        hu = _dot(xb, w13_ref[0])
        f = hu.shape[1] // 2
        act = _silu(hu[:, :f]) * hu[:, f:]
        y = _dot(act.astype(BF16), w2_ref[0])

        @pl.when(first_ref[i] == 1)
        def _():
            _slab_store(ys_ref, y, nrow)

        @pl.when(first_ref[i] == 0)
        def _():
            ridx = lax.broadcasted_iota(jnp.int32, (rows, 1), 0)
            mine = (ridx >= lo) & (ridx < hi)
            old = jnp.concatenate(_slab_chunks(ys_ref, rows, nrow), axis=1)
            _slab_store(ys_ref, jnp.where(mine, y, old), nrow)


def _experts(item_blk, item_exp, item_lo, item_hi, item_first, xs, w13, w2, nrow):
    a = xs.shape[0] // nrow
    n_items = item_blk.shape[0]
    d = w13.shape[1]
    ff2 = w13.shape[2]
    grid_spec = pltpu.PrefetchScalarGridSpec(
        num_scalar_prefetch=5,
        grid=(n_items,),
        in_specs=[pl.BlockSpec((EXPERT_BLOCK * nrow, LANES), lambda i, b, e, lo, hi, fr: (b[i], 0)),
                  pl.BlockSpec((1, d, ff2), lambda i, b, e, lo, hi, fr: (e[i], 0, 0)),
                  pl.BlockSpec((1, ff2 // 2, d), lambda i, b, e, lo, hi, fr: (e[i], 0, 0))],
        out_specs=pl.BlockSpec((EXPERT_BLOCK * nrow, LANES), lambda i, b, e, lo, hi, fr: (b[i], 0)),
    )
    return pl.pallas_call(
        functools.partial(_expert_kernel, nrow=nrow),
        grid_spec=grid_spec,
        out_shape=jax.ShapeDtypeStruct((a * nrow, LANES), F32),
        compiler_params=pltpu.CompilerParams(dimension_semantics=("arbitrary",),
                                             vmem_limit_bytes=VMEM_LIMIT_BYTES),
        name="experts",
    )(item_blk, item_exp, item_lo, item_hi, item_first, xs, w13, w2)


def _final_kernel(dest_ref, x1_ref, route_ref, mod_ref, fw_ref, ys_hbm, o_ref, buf1, buf2, sem):
    tf = x1_ref.shape[1]
    nrow = buf1.shape[0] // tf

    def issue(blk, carry):
        for u in range(DMA_UNROLL):
            tk = blk * DMA_UNROLL + u
            for k, buf in enumerate((buf1, buf2)):
                pltpu.make_async_copy(_slab_of_row(ys_hbm, dest_ref[0, 0, tk * TOP_K + k], nrow),
                                      _slab_of_row(buf, tk, nrow), sem).start()
        return carry

    lax.fori_loop(0, tf // DMA_UNROLL, issue, 0)

    pltpu.make_async_copy(ys_hbm.at[pl.ds(0, tf * nrow)], buf1, sem).wait()
    pltpu.make_async_copy(ys_hbm.at[pl.ds(0, tf * nrow)], buf2, sem).wait()

    r = route_ref[...]
    gate1 = r[:, 2:3]
    gate2 = r[:, 3:4]
    y = jnp.concatenate([c1 * gate1 + c2 * gate2 for c1, c2 in
                         zip(_slab_chunks(buf1, tf, nrow), _slab_chunks(buf2, tf, nrow))], axis=1)
    g2 = mod_ref[0][5:6]
    o_ref[0] = _rms(x1_ref[0] + g2 * y) * fw_ref[...]


def _final(dest, x1, route, mod, fw, ys):
    b, s, d = x1.shape
    tf = FINAL_TILE
    nrow = d // LANES
    spt = s // tf
    n = tf * TOP_K
    return pl.pallas_call(
        _final_kernel,
        grid=(b, spt),
        in_specs=[pl.BlockSpec((1, 1, n), lambda i, j: (i * spt + j, 0, 0), memory_space=pltpu.SMEM),
                  pl.BlockSpec((1, tf, d), lambda i, j: (i, j, 0)),
                  pl.BlockSpec((tf, LANES), lambda i, j: (i * spt + j, 0)),
                  pl.BlockSpec((1, 6, d), lambda i, j: (i, 0, 0)),
                  pl.BlockSpec((1, d), lambda i, j: (0, 0)),
                  pl.BlockSpec(memory_space=pl.ANY)],
        out_specs=pl.BlockSpec((1, tf, d), lambda i, j: (i, j, 0)),
        out_shape=jax.ShapeDtypeStruct((b, s, d), F32),
        scratch_shapes=[pltpu.VMEM((tf * nrow, LANES), F32),
                        pltpu.VMEM((tf * nrow, LANES), F32),
                        pltpu.SemaphoreType.DMA(())],
        compiler_params=pltpu.CompilerParams(dimension_semantics=("arbitrary", "arbitrary"),
                                             vmem_limit_bytes=VMEM_LIMIT_BYTES),
        name="final",
    )(dest.reshape(b * spt, 1, n), x1, route, mod, fw, ys)


def _pad_lanes(v, fill=0.0):
    n = v.shape[-1]
    return jnp.pad(v, [(0, 0)] * (v.ndim - 1) + [(0, LANES - n)], constant_values=fill)


def _work_items(counts, n_blocks):
    n_items = n_blocks + N_EXPERTS - 1
    ends = jnp.cumsum(counts)
    starts = ends - counts
    first_blk = starts // EXPERT_BLOCK
    last_blk = jnp.maximum(ends - 1, starts) // EXPERT_BLOCK
    n_e = jnp.where(counts > 0, last_blk - first_blk + 1, 0)
    item_end = jnp.cumsum(n_e)
    item_start = item_end - n_e
    ids = jnp.arange(n_items, dtype=jnp.int32)
    total = item_end[-1]
    ids_c = jnp.minimum(ids, total - 1)
    e = jnp.sum((item_end[None, :] <= ids_c[:, None]).astype(jnp.int32), axis=1)
    onehot = (e[:, None] == jnp.arange(N_EXPERTS, dtype=jnp.int32)[None, :]).astype(jnp.int32)
    pick = lambda v: jnp.sum(onehot * v[None, :], axis=1)
    blk = pick(first_blk) + ids_c - pick(item_start)
    lo = jnp.clip(pick(starts) - blk * EXPERT_BLOCK, 0, EXPERT_BLOCK)
    hi = jnp.clip(pick(ends) - blk * EXPERT_BLOCK, 0, EXPERT_BLOCK)
    hi = jnp.where(ids < total, hi, lo)
    prev_blk = jnp.concatenate([jnp.full((1,), -1, jnp.int32), blk[:-1]])
    first = (blk != prev_blk).astype(jnp.int32)
    return blk, e, lo, hi, first


def kernel(x, c, w_ada, b_ada, w_in, w_pool, pool_scale, conv_w, conv_b, dt_bias, a_log, d_skip,
           ssd_norm_w, w_out, w_group, b_group, w_router, b_router, w13, w2, final_norm_w):
    b, s, d = x.shape
    depth = w_ada.shape[0]
    t = b * s
    pool_w = w_pool.shape[1] * w_pool.shape[2]
    ssd_w = SSD_HEADS * SSD_HEAD_DIM
    conv_dim = conv_w.shape[2]
    assert s % SEQ_TILE == 0 and SEQ_TILE % SSD_CHUNK == 0 and b % MIXER_SEQS == 0
    assert t % RANK_TILE == 0 and t % DISPATCH_TILE == 0 and s % FINAL_TILE == 0
    assert (t * TOP_K) % EXPERT_BLOCK == 0 and d % LANES == 0
    assert pool_w == 2 * LANES and len(POOL_WINDOWS) * POOL_GROUP_DIM == pool_w
    assert N_EXPERT_GROUPS + N_EXPERTS <= LANES and SSD_HEADS <= LANES
    assert depth == 1, "the final RMSNorm is fused into the last layer's combine step"

    for layer in range(depth):
        mod = _ada(c, w_ada[layer], b_ada[layer]).reshape(b, 6, d)

        o_dt = pool_w + ssd_w + conv_dim
        win = jnp.concatenate([w_in[layer][:, :o_dt], _pad_lanes(w_in[layer][:, o_dt:])], axis=1).astype(BF16)
        wp = jnp.zeros((pool_w, pool_w), F32)
        for g in range(len(POOL_WINDOWS)):
            sl = slice(g * POOL_GROUP_DIM, (g + 1) * POOL_GROUP_DIM)
            wp = wp.at[sl, sl].set(w_pool[layer, g])
        wp = wp.astype(BF16)
        w_route = _pad_lanes(jnp.concatenate([w_group[layer], w_router[layer]], axis=1))
        wrh = w_route.astype(BF16)
        wrl = (w_route - wrh.astype(F32)).astype(BF16)
        br = _pad_lanes(jnp.concatenate([b_group[layer], b_router[layer]])[None, :])

        x1, h2, route = _mixer(
            x, mod, win, wp, pool_scale[layer][None, :], conv_w[layer], conv_b[layer][None, :],
            _pad_lanes(dt_bias[layer][None, :]), _pad_lanes(a_log[layer][None, :], fill=NEG_BIG),
            jnp.repeat(d_skip[layer], SSD_HEAD_DIM)[None, :], ssd_norm_w[layer][None, :],
            w_out[layer].astype(BF16), wrh, wrl, br, pool_w=pool_w, ssd_w=ssd_w)

        counts_f = _count(route)
        start_f = jnp.cumsum(counts_f, axis=1) - counts_f
        dest = _rank(route, start_f)[:, :TOP_K].astype(jnp.int32)
        counts = counts_f[0, :N_EXPERTS].astype(jnp.int32)
        items = _work_items(counts, (t * TOP_K) // EXPERT_BLOCK)

        nrow = d // LANES
        xs = _dispatch(dest, h2, nrow)
        ys = _experts(*items, xs, w13[layer].astype(BF16), w2[layer].astype(BF16), nrow)
        x = _final(dest, x1, route, mod, final_norm_w[None, :], ys)
    return x
```

```python
import functools

import jax
import jax.numpy as jnp
from jax import lax
from jax.experimental import pallas as pl
from jax.experimental.pallas import tpu as pltpu

POOL_WINDOWS = (2, 4, 8, 16)
POOL_GROUP_DIM = 64
SSD_HEAD_DIM = 64
SSD_GROUPS = 4
SSD_HEADS_PER_GROUP = 3
SSD_HEADS = SSD_GROUPS * SSD_HEADS_PER_GROUP
SSD_STATE = 128
SSD_CONV = 4
SSD_CHUNK = 128
N_EXPERT_GROUPS = 4
EXPERTS_PER_GROUP = 8
N_EXPERTS = N_EXPERT_GROUPS * EXPERTS_PER_GROUP
TOP_K = 2
NORM_EPS = 1e-6

LANES = 128
SUBLANES = 8
VMEM_LIMIT_BYTES = 56 * 1024 * 1024

SEQ_TILE = 256
MIXER_SEQS = 2
CONV_HALO = SUBLANES
POOL_HALO = 16
RANK_TILE = 512
DISPATCH_TILE = 1024
EXPERT_BLOCK = 512
EXPERT_CHAINS = 2
FINAL_TILE = 512
DMA_UNROLL = 8
DMA_THREADS = 2

NEG_BIG = -1e30
F32 = jnp.float32
BF16 = jnp.bfloat16


def _silu(v):
    half = 0.5 * v
    return half + half * jnp.tanh(half)


def _softplus(v):
    return jnp.maximum(v, 0.0) + jnp.log(1.0 + jnp.exp(-jnp.abs(v)))


def _rms(v):
    return v * lax.rsqrt(jnp.mean(v * v, axis=-1, keepdims=True) + NORM_EPS)


def _dot(a, b):
    return jnp.dot(a, b, preferred_element_type=F32)


def _dot_exact(a, b):
    return jnp.dot(a, b, preferred_element_type=F32, precision=lax.Precision.HIGHEST)


def _run_skewed(chains):
    live = [False] * len(chains)
    started = 0
    while started < len(chains) or any(live):
        if started < len(chains):
            live[started] = True
            started += 1
        for k, chain in enumerate(chains):
            if live[k]:
                try:
                    next(chain)
                except StopIteration:
                    live[k] = False


def _slab_chunks(ref, rows, nrow):
    return [ref[pl.ds(c, rows, stride=nrow), :] for c in range(nrow)]


def _slab_store(ref, val, nrow):
    rows = val.shape[0]
    for c in range(nrow):
        ref[pl.ds(c, rows, stride=nrow), :] = val[:, c * LANES:(c + 1) * LANES]


def _slab_of_row(ref, row, nrow):
    return ref.at[pl.ds(pl.multiple_of(row * nrow, nrow), nrow)]


def _ada_kernel(c_ref, w_ref, b_ref, o_ref):
    o_ref[...] = _dot_exact(_silu(c_ref[...]), w_ref[...]) + b_ref[...]


def _ada(c, w_ada, b_ada):
    b, d = c.shape
    n = w_ada.shape[1]
    return pl.pallas_call(
        _ada_kernel,
        grid=(n // d,),
        in_specs=[pl.BlockSpec((b, d), lambda i: (0, 0)),
                  pl.BlockSpec((d, d), lambda i: (0, i)),
                  pl.BlockSpec((1, d), lambda i: (0, i))],
        out_specs=pl.BlockSpec((b, d), lambda i: (0, i)),
        out_shape=jax.ShapeDtypeStruct((b, n), F32),
        compiler_params=pltpu.CompilerParams(dimension_semantics=("arbitrary",),
                                             vmem_limit_bytes=VMEM_LIMIT_BYTES),
        name="ada",
    )(c, w_ada, b_ada.reshape(1, n))


def _mixer_kernel(x_ref, mod_ref, win_ref, wp_ref, ps_ref, cw_ref, cb_ref, dtb_ref, alog_ref,
                  dsk_ref, nw_ref, wout_ref, wrh_ref, wrl_ref, br_ref, rep_ref,
                  x1_ref, h2_ref, route_ref,
                  state_ref, xext_ref, uext_ref, *, pool_w, ssd_w):
    @pl.when(pl.program_id(1) == 0)
    def _():
        state_ref[...] = jnp.zeros_like(state_ref)
        xext_ref[:, 0:CONV_HALO, :] = jnp.zeros((xext_ref.shape[0], CONV_HALO, xext_ref.shape[2]), F32)
        uext_ref[:, 0:POOL_HALO, :] = jnp.zeros((uext_ref.shape[0], POOL_HALO, uext_ref.shape[2]), F32)

    _run_skewed([_mixer_tile(x_ref.at[bi], mod_ref.at[bi], win_ref, wp_ref, ps_ref, cw_ref, cb_ref, dtb_ref,
                             alog_ref, dsk_ref, nw_ref, wout_ref, wrh_ref, wrl_ref, br_ref, rep_ref,
                             x1_ref.at[bi], h2_ref.at[bi], route_ref.at[bi],
                             state_ref.at[bi], xext_ref.at[bi], uext_ref.at[bi], pool_w=pool_w, ssd_w=ssd_w)
                 for bi in range(x_ref.shape[0])])


def _mixer_tile(x_ref, mod_ref, win_ref, wp_ref, ps_ref, cw_ref, cb_ref, dtb_ref, alog_ref,
                dsk_ref, nw_ref, wout_ref, wrh_ref, wrl_ref, br_ref, rep_ref,
                x1_ref, h2_ref, route_ref,
                state_ref, xext_ref, uext_ref, *, pool_w, ssd_w):
    ts = x_ref.shape[0]
    d = x_ref.shape[1]
    L = SSD_CHUNK
    N = SSD_STATE
    P = SSD_HEAD_DIM
    j = pl.program_id(1)

    x = x_ref[...]
    mod = mod_ref[...]
    sh1, sc1, g1 = mod[0:1], mod[1:2], mod[2:3]
    sh2, sc2 = mod[3:4], mod[4:5]

    h = (_rms(x) * (1.0 + sc1) + sh1).astype(BF16)
    yield
    proj = _dot(h, win_ref[...])
    yield
    o_z = pool_w
    o_x = pool_w + ssd_w
    conv_dim = ssd_w + 2 * SSD_GROUPS * N
    o_dt = o_x + conv_dim
    u = proj[:, 0:pool_w]
    z = proj[:, o_z:o_x]
    xbc = proj[:, o_x:o_dt]
    dtr = proj[:, o_dt:o_dt + LANES]

    uext_ref[POOL_HALO:POOL_HALO + ts, :] = u
    lane_u = lax.broadcasted_iota(jnp.int32, (1, LANES), 1)
    upper = lane_u >= POOL_GROUP_DIM

    def ush(k, col):
        return uext_ref[POOL_HALO - k:POOL_HALO - k + ts, col * LANES:(col + 1) * LANES]

    w0, w1, w2, w3 = POOL_WINDOWS
    lo_a = ush(0, 0)
    for k in range(1, w0):
        lo_a = lo_a + ush(k, 0)
    lo_b = ush(w0, 0)
    for k in range(w0 + 1, w1):
        lo_b = lo_b + ush(k, 0)
    hi_a = ush(0, 1)
    for k in range(1, w2):
        hi_a = hi_a + ush(k, 1)
    hi_b = ush(w2, 1)
    for k in range(w2 + 1, w3):
        hi_b = hi_b + ush(k, 1)
    win_lo = lo_a + jnp.where(upper, lo_b, 0.0)
    win_hi = hi_a + jnp.where(upper, hi_b, 0.0)
    pos = (j * ts + 1 + lax.broadcasted_iota(jnp.int32, (ts, 1), 0)).astype(F32)
    cnt_lo = jnp.minimum(pos, jnp.where(upper, float(w1), float(w0)))
    cnt_hi = jnp.minimum(pos, jnp.where(upper, float(w3), float(w2)))
    pdiff = jnp.concatenate([win_lo / cnt_lo, win_hi / cnt_hi], axis=1) - u
    y_pool = _dot(pdiff.astype(BF16), wp_ref[...]) * ps_ref[...]
    uext_ref[0:POOL_HALO, :] = uext_ref[ts:ts + POOL_HALO, :]

    xext_ref[CONV_HALO:CONV_HALO + ts, :] = xbc
    cw = cw_ref[...]
    xe = xext_ref[...]
    acc = cw[0:1] * xe
    for k in range(1, SSD_CONV):
        acc = cw[k:k + 1] * xe + pltpu.roll(acc, 1, axis=0)
    xa = _silu(cb_ref[...] + acc[CONV_HALO:CONV_HALO + ts])
    xext_ref[0:CONV_HALO, :] = xext_ref[ts:ts + CONV_HALO, :]
    yield

    a_neg = -jnp.exp(alog_ref[...])
    row_i = lax.broadcasted_iota(jnp.int32, (L, L), 0)
    col_i = lax.broadcasted_iota(jnp.int32, (L, L), 1)
    causal = row_i >= col_i
    tril = causal.astype(F32)
    hpt = LANES // P
    lane_h = lax.broadcasted_iota(jnp.int32, (1, LANES), 1)
    y_chunks = []
    for c in range(ts // L):
        r0 = c * L
        xs_c = xa[r0:r0 + L, 0:ssd_w]
        b_c = xa[r0:r0 + L, ssd_w:ssd_w + SSD_GROUPS * N]
        c_c = xa[r0:r0 + L, ssd_w + SSD_GROUPS * N:conv_dim]
        dt = _softplus(dtr[r0:r0 + L] + dtb_ref[...])
        a_cs = _dot_exact(tril, dt * a_neg)
        a_hi = a_cs.astype(BF16)
        a_lo = (a_cs - a_hi.astype(F32)).astype(BF16)
        a_q = a_hi.astype(F32) + a_lo.astype(F32)
        a_rep = _dot(jnp.concatenate([a_hi, a_lo], axis=1), rep_ref[...])
        a_q_t = a_q.T
        dt_t = dt.T
        cbs, cgs, bg_ts = [], [], []
        for g in range(SSD_GROUPS):
            bg = b_c[:, g * N:(g + 1) * N]
            cg = c_c[:, g * N:(g + 1) * N]
            cbs.append(lax.dot_general(cg.astype(BF16), bg.astype(BF16), (((1,), (1,)), ((), ())),
                                       preferred_element_type=F32))
            cgs.append(cg)
            bg_ts.append(bg.T)
        y_tiles = []
        for q in range(SSD_HEADS // hpt):
            xs_q = xs_c[:, q * LANES:(q + 1) * LANES].astype(BF16)
            st = state_ref[q]
            rhs = jnp.concatenate([xs_q, st.astype(BF16)], axis=0)
            y_q = upd_q = keep_q = None
            for hh in range(hpt):
                hd = q * hpt + hh
                g = hd // SSD_HEADS_PER_GROUP
                a_col = a_rep[:, hd * LANES:(hd + 1) * LANES]
                a_row = a_q_t[hd:hd + 1, :]
                dt_row = dt_t[hd:hd + 1, :]
                decay = jnp.exp(jnp.where(causal, a_col - a_row, NEG_BIG))
                m = cbs[g] * decay * dt_row
                cs = cgs[g] * jnp.exp(a_col)
                lhs = jnp.concatenate([m, cs], axis=1).astype(BF16)
                y_h = _dot(lhs, rhs)
                a_end = a_row[:, L - 1:L]
                w_row = dt_row * jnp.exp(a_end - a_row)
                upd_h = _dot((bg_ts[g] * w_row).astype(BF16), xs_q)
                keep_h = jnp.exp(a_end)
                if hh == 0:
                    y_q, upd_q, keep_q = y_h, upd_h, keep_h
                else:
                    mine = (lane_h >= hh * P) & (lane_h < (hh + 1) * P)
                    y_q = jnp.where(mine, y_h, y_q)
                    upd_q = jnp.where(mine, upd_h, upd_q)
                    keep_q = jnp.where(mine, keep_h, keep_q)
            y_tiles.append(y_q)
            state_ref[q] = keep_q * st + upd_q
        y_chunks.append(jnp.concatenate(y_tiles, axis=1))
        yield
    y = jnp.concatenate(y_chunks, axis=0) if len(y_chunks) > 1 else y_chunks[0]
    y = y + xa[:, 0:ssd_w] * dsk_ref[...]
    y = y * _silu(z)

    gw = ssd_w // SSD_GROUPS
    lane_y = lax.broadcasted_iota(jnp.int32, (1, ssd_w), 1)
    y2 = y * y
    scale = jnp.zeros_like(y)
    for g in range(SSD_GROUPS):
        in_g = (lane_y >= g * gw) & (lane_y < (g + 1) * gw)
        ms = jnp.sum(jnp.where(in_g, y2, 0.0), axis=-1, keepdims=True) * (1.0 / gw)
        scale = scale + jnp.where(in_g, lax.rsqrt(ms + NORM_EPS), 0.0)
    mix_in = jnp.concatenate([y_pool, y * scale * nw_ref[...]], axis=1).astype(BF16)
    yield
    mix = _dot(mix_in, wout_ref[...])
    yield
    x1 = x + g1 * mix
    x1_ref[...] = x1

    h2 = _rms(x1) * (1.0 + sc2) + sh2
    _slab_store(h2_ref, h2, d // LANES)
    h_hi = h2.astype(BF16)
    h_lo = (h2 - h_hi.astype(F32)).astype(BF16)
    yield
    logits = (_dot(h_hi, wrh_ref[...]) + _dot(h_lo, wrh_ref[...]) + _dot(h_hi, wrl_ref[...])
              + br_ref[...])
    lane = lax.broadcasted_iota(jnp.int32, (1, LANES), 1).astype(F32)
    big = float(LANES)
    gl = jnp.where(lane < N_EXPERT_GROUPS, logits, NEG_BIG)
    gmax = jnp.max(gl, axis=-1, keepdims=True)
    gsum = jnp.sum(jnp.exp(gl - gmax), axis=-1, keepdims=True)
    p_g = 1.0 / gsum
    g_idx = jnp.min(jnp.where(gl == gmax, lane, big), axis=-1, keepdims=True)
    lo = N_EXPERT_GROUPS + EXPERTS_PER_GROUP * g_idx
    el = jnp.where((lane >= lo) & (lane < lo + EXPERTS_PER_GROUP), logits, NEG_BIG)
    v1 = jnp.max(el, axis=-1, keepdims=True)
    i1 = jnp.min(jnp.where(el == v1, lane, big), axis=-1, keepdims=True)
    el2 = jnp.where(lane == i1, NEG_BIG, el)
    v2 = jnp.max(el2, axis=-1, keepdims=True)
    i2 = jnp.min(jnp.where(el2 == v2, lane, big), axis=-1, keepdims=True)
    e21 = jnp.exp(v2 - v1)
    gate1 = p_g / (1.0 + e21)
    gate2 = p_g * e21 / (1.0 + e21)
    route = jnp.where(lane == 0, i1 - N_EXPERT_GROUPS,
                      jnp.where(lane == 1, i2 - N_EXPERT_GROUPS,
                                jnp.where(lane == 2, gate1, jnp.where(lane == 3, gate2, 0.0))))
    route_ref[...] = route


def _mixer(x, mod, win, wp, ps, cw, cb, dtb, alog, dsk, nw, wout, wrh, wrl, br, *, pool_w, ssd_w):
    k_idx = jnp.arange(2 * LANES, dtype=jnp.int32)[:, None] % LANES
    h_idx = jnp.arange(SSD_HEADS * LANES, dtype=jnp.int32)[None, :] // LANES
    rep = (k_idx == h_idx).astype(BF16)
    b, s, d = x.shape
    ts = SEQ_TILE
    conv_dim = cw.shape[1]
    nrow = d // LANES

    def full(a):
        nd = a.ndim
        return pl.BlockSpec(a.shape, lambda i, j, _nd=nd: (0,) * _nd)

    nb = MIXER_SEQS
    seq_tile = lambda i, j: (i, j, 0)
    x1, h2, route = pl.pallas_call(
        functools.partial(_mixer_kernel, pool_w=pool_w, ssd_w=ssd_w),
        grid=(b // nb, s // ts),
        in_specs=[pl.BlockSpec((nb, ts, d), seq_tile),
                  pl.BlockSpec((nb, 6, d), lambda i, j: (i, 0, 0)),
                  full(win), full(wp), full(ps), full(cw), full(cb), full(dtb), full(alog),
                  full(dsk), full(nw), full(wout), full(wrh), full(wrl), full(br), full(rep)],
        out_specs=[pl.BlockSpec((nb, ts, d), seq_tile),
                   pl.BlockSpec((nb, ts * nrow, LANES), seq_tile),
                   pl.BlockSpec((nb, ts, LANES), seq_tile)],
        out_shape=[jax.ShapeDtypeStruct((b, s, d), F32),
                   jax.ShapeDtypeStruct((b, s * nrow, LANES), F32),
                   jax.ShapeDtypeStruct((b, s, LANES), F32)],
        scratch_shapes=[pltpu.VMEM((nb, ssd_w // LANES, SSD_STATE, LANES), F32),
                        pltpu.VMEM((nb, CONV_HALO + ts, conv_dim), F32),
                        pltpu.VMEM((nb, POOL_HALO + ts, pool_w), F32)],
        compiler_params=pltpu.CompilerParams(dimension_semantics=("arbitrary", "arbitrary"),
                                             vmem_limit_bytes=VMEM_LIMIT_BYTES),
        name="mixer",
    )(x, mod, win, wp, ps, cw, cb, dtb, alog, dsk, nw, wout, wrh, wrl, br, rep)
    return x1, h2.reshape(b * s * nrow, LANES), route.reshape(b * s, LANES)


def _onehots(route):
    lane = lax.broadcasted_iota(jnp.int32, (1, LANES), 1).astype(F32)
    oh1 = jnp.where(route[:, 0:1] == lane, 1.0, 0.0)
    oh2 = jnp.where(route[:, 1:2] == lane, 1.0, 0.0)
    return oh1, oh2


def _count_kernel(route_ref, cnt_ref):
    @pl.when(pl.program_id(0) == 0)
    def _():
        cnt_ref[...] = jnp.zeros_like(cnt_ref)

    oh1, oh2 = _onehots(route_ref[...])
    cnt_ref[...] += jnp.sum(oh1 + oh2, axis=0, keepdims=True)


def _count(route):
    t = route.shape[0]
    return pl.pallas_call(
        _count_kernel,
        grid=(t // RANK_TILE,),
        in_specs=[pl.BlockSpec((RANK_TILE, LANES), lambda i: (i, 0))],
        out_specs=pl.BlockSpec((1, LANES), lambda i: (0, 0)),
        out_shape=jax.ShapeDtypeStruct((1, LANES), F32),
        compiler_params=pltpu.CompilerParams(dimension_semantics=("arbitrary",)),
        name="count",
    )(route)


def _rank_kernel(route_ref, start_ref, dest_ref, carry_ref):
    ta = route_ref.shape[0]

    @pl.when(pl.program_id(0) == 0)
    def _():
        carry_ref[...] = start_ref[...]

    oh1, oh2 = _onehots(route_ref[...])
    oh = oh1 + oh2
    earlier = (lax.broadcasted_iota(jnp.int32, (ta, ta), 0)
               > lax.broadcasted_iota(jnp.int32, (ta, ta), 1))
    before = _dot(jnp.where(earlier, 1.0, 0.0).astype(BF16), oh.astype(BF16))
    base = carry_ref[...] + before
    d1 = jnp.sum(oh1 * base, axis=-1, keepdims=True)
    d2 = jnp.sum(oh2 * base, axis=-1, keepdims=True)
    lane = lax.broadcasted_iota(jnp.int32, (1, LANES), 1)
    dest_ref[...] = jnp.where(lane == 0, d1, jnp.where(lane == 1, d2, 0.0))
    carry_ref[...] += jnp.sum(oh, axis=0, keepdims=True)


def _rank(route, start):
    t = route.shape[0]
    return pl.pallas_call(
        _rank_kernel,
        grid=(t // RANK_TILE,),
        in_specs=[pl.BlockSpec((RANK_TILE, LANES), lambda i: (i, 0)),
                  pl.BlockSpec((1, LANES), lambda i: (0, 0))],
        out_specs=pl.BlockSpec((RANK_TILE, LANES), lambda i: (i, 0)),
        out_shape=jax.ShapeDtypeStruct((t, LANES), F32),
        scratch_shapes=[pltpu.VMEM((1, LANES), F32)],
        compiler_params=pltpu.CompilerParams(dimension_semantics=("arbitrary",)),
        name="rank",
    )(route, start)


def _dispatch_kernel(dest_ref, h2_ref, xs_hbm, sem, *, nrow):
    n = dest_ref.shape[2]
    tile_rows = h2_ref.shape[0]

    def issue(blk, carry):
        for u in range(DMA_UNROLL):
            tk = blk * DMA_UNROLL + u
            for k in range(TOP_K):
                pltpu.make_async_copy(_slab_of_row(h2_ref, tk, nrow),
                                      _slab_of_row(xs_hbm, dest_ref[0, 0, tk * TOP_K + k], nrow),
                                      sem).start(priority=(u * TOP_K + k) % DMA_THREADS)
        return carry

    lax.fori_loop(0, n // (TOP_K * DMA_UNROLL), issue, 0)

    for _ in range(TOP_K):
        pltpu.make_async_copy(h2_ref, xs_hbm.at[pl.ds(0, tile_rows)], sem).wait()


def _dispatch(dest, h2, nrow):
    t = h2.shape[0] // nrow
    n = DISPATCH_TILE * TOP_K
    return pl.pallas_call(
        functools.partial(_dispatch_kernel, nrow=nrow),
        grid=(t // DISPATCH_TILE,),
        in_specs=[pl.BlockSpec((1, 1, n), lambda i: (i, 0, 0), memory_space=pltpu.SMEM),
                  pl.BlockSpec((DISPATCH_TILE * nrow, LANES), lambda i: (i, 0))],
        out_specs=pl.BlockSpec(memory_space=pl.ANY),
        out_shape=jax.ShapeDtypeStruct((t * TOP_K * nrow, LANES), F32),
        scratch_shapes=[pltpu.SemaphoreType.DMA(())],
        compiler_params=pltpu.CompilerParams(dimension_semantics=("arbitrary",),
                                             has_side_effects=True),
        name="dispatch",
    )(dest.reshape(t // DISPATCH_TILE, 1, n), h2)


def _expert_kernel(blk_ref, exp_ref, lo_ref, hi_ref, first_ref, xs_ref, w13_ref, w2_ref, ys_ref, *, nrow):
    i = pl.program_id(0)
    rows = xs_ref.shape[0] // nrow
    lo = lo_ref[i]
    hi = hi_ref[i]

    sub = rows // EXPERT_CHAINS

    def chain(r0):
        xb = jnp.concatenate([xs_ref[pl.ds(r0 * nrow + c, sub, stride=nrow), :].astype(BF16)
                              for c in range(nrow)], axis=1)
        yield
        hu = _dot(xb, w13_ref[0])
        yield
        f = hu.shape[1] // 2
        act = (_silu(hu[:, :f]) * hu[:, f:]).astype(BF16)
        yield
        y = _dot(act, w2_ref[0])
        yield
        ridx = r0 + lax.broadcasted_iota(jnp.int32, (sub, 1), 0)
        take = (ridx >= lo_eff) & (ridx < hi_eff)
        for c in range(nrow):
            sl = pl.ds(r0 * nrow + c, sub, stride=nrow)
            ys_ref[sl, :] = jnp.where(take, y[:, c * LANES:(c + 1) * LANES], ys_ref[sl, :])

    is_first = first_ref[i] == 1
    lo_eff = jnp.where(is_first, 0, lo)
    hi_eff = jnp.where(is_first, rows, hi)

    @pl.when(hi > lo)
    def _():
        _run_skewed([chain(k * sub) for k in range(EXPERT_CHAINS)])


def _experts(item_blk, item_exp, item_lo, item_hi, item_first, xs, w13, w2, nrow):
    a = xs.shape[0] // nrow
    n_items = item_blk.shape[0]
    d = w13.shape[1]
    ff2 = w13.shape[2]
    grid_spec = pltpu.PrefetchScalarGridSpec(
        num_scalar_prefetch=5,
        grid=(n_items,),
        in_specs=[pl.BlockSpec((EXPERT_BLOCK * nrow, LANES), lambda i, b, e, lo, hi, fr: (b[i], 0)),
                  pl.BlockSpec((1, d, ff2), lambda i, b, e, lo, hi, fr: (e[i], 0, 0)),
                  pl.BlockSpec((1, ff2 // 2, d), lambda i, b, e, lo, hi, fr: (e[i], 0, 0))],
        out_specs=pl.BlockSpec((EXPERT_BLOCK * nrow, LANES), lambda i, b, e, lo, hi, fr: (b[i], 0)),
    )
    return pl.pallas_call(
        functools.partial(_expert_kernel, nrow=nrow),
        grid_spec=grid_spec,
        out_shape=jax.ShapeDtypeStruct((a * nrow, LANES), F32),
        compiler_params=pltpu.CompilerParams(dimension_semantics=("arbitrary",),
                                             vmem_limit_bytes=VMEM_LIMIT_BYTES),
        name="experts",
    )(item_blk, item_exp, item_lo, item_hi, item_first, xs, w13, w2)


def _final_kernel(dest_ref, x1_ref, route_ref, mod_ref, fw_ref, ys_hbm, o_ref, buf1, buf2, sem):
    tf = x1_ref.shape[1]
    nrow = buf1.shape[0] // tf

    def issue(blk, carry):
        for u in range(DMA_UNROLL):
            tk = blk * DMA_UNROLL + u
            for k, buf in enumerate((buf1, buf2)):
                pltpu.make_async_copy(_slab_of_row(ys_hbm, dest_ref[0, 0, tk * TOP_K + k], nrow),
                                      _slab_of_row(buf, tk, nrow),
                                      sem).start(priority=(u * TOP_K + k) % DMA_THREADS)
        return carry

    lax.fori_loop(0, tf // DMA_UNROLL, issue, 0)

    pltpu.make_async_copy(ys_hbm.at[pl.ds(0, tf * nrow)], buf1, sem).wait()
    pltpu.make_async_copy(ys_hbm.at[pl.ds(0, tf * nrow)], buf2, sem).wait()

    r = route_ref[...]
    gate1 = r[:, 2:3]
    gate2 = r[:, 3:4]
    y = jnp.concatenate([c1 * gate1 + c2 * gate2 for c1, c2 in
                         zip(_slab_chunks(buf1, tf, nrow), _slab_chunks(buf2, tf, nrow))], axis=1)
    g2 = mod_ref[0][5:6]
    o_ref[0] = _rms(x1_ref[0] + g2 * y) * fw_ref[...]


def _final(dest, x1, route, mod, fw, ys):
    b, s, d = x1.shape
    tf = FINAL_TILE
    nrow = d // LANES
    spt = s // tf
    n = tf * TOP_K
    return pl.pallas_call(
        _final_kernel,
        grid=(b, spt),
        in_specs=[pl.BlockSpec((1, 1, n), lambda i, j: (i * spt + j, 0, 0), memory_space=pltpu.SMEM),
                  pl.BlockSpec((1, tf, d), lambda i, j: (i, j, 0)),
                  pl.BlockSpec((tf, LANES), lambda i, j: (i * spt + j, 0)),
                  pl.BlockSpec((1, 6, d), lambda i, j: (i, 0, 0)),
                  pl.BlockSpec((1, d), lambda i, j: (0, 0)),
                  pl.BlockSpec(memory_space=pl.ANY)],
        out_specs=pl.BlockSpec((1, tf, d), lambda i, j: (i, j, 0)),
        out_shape=jax.ShapeDtypeStruct((b, s, d), F32),
        scratch_shapes=[pltpu.VMEM((tf * nrow, LANES), F32),
                        pltpu.VMEM((tf * nrow, LANES), F32),
                        pltpu.SemaphoreType.DMA(())],
        compiler_params=pltpu.CompilerParams(dimension_semantics=("arbitrary", "arbitrary"),
                                             vmem_limit_bytes=VMEM_LIMIT_BYTES),
        name="final",
    )(dest.reshape(b * spt, 1, n), x1, route, mod, fw, ys)


def _pad_lanes(v, fill=0.0):
    n = v.shape[-1]
    return jnp.pad(v, [(0, 0)] * (v.ndim - 1) + [(0, LANES - n)], constant_values=fill)


def _work_items(counts, n_blocks):
    n_items = n_blocks + N_EXPERTS - 1
    ends = jnp.cumsum(counts)
    starts = ends - counts
    first_blk = starts // EXPERT_BLOCK
    last_blk = jnp.maximum(ends - 1, starts) // EXPERT_BLOCK
    n_e = jnp.where(counts > 0, last_blk - first_blk + 1, 0)
    item_end = jnp.cumsum(n_e)
    item_start = item_end - n_e
    ids = jnp.arange(n_items, dtype=jnp.int32)
    total = item_end[-1]
    ids_c = jnp.minimum(ids, total - 1)
    e = jnp.sum((item_end[None, :] <= ids_c[:, None]).astype(jnp.int32), axis=1)
    onehot = (e[:, None] == jnp.arange(N_EXPERTS, dtype=jnp.int32)[None, :]).astype(jnp.int32)
    pick = lambda v: jnp.sum(onehot * v[None, :], axis=1)
    blk = pick(first_blk) + ids_c - pick(item_start)
    lo = jnp.clip(pick(starts) - blk * EXPERT_BLOCK, 0, EXPERT_BLOCK)
    hi = jnp.clip(pick(ends) - blk * EXPERT_BLOCK, 0, EXPERT_BLOCK)
    hi = jnp.where(ids < total, hi, lo)
    prev_blk = jnp.concatenate([jnp.full((1,), -1, jnp.int32), blk[:-1]])
    first = (blk != prev_blk).astype(jnp.int32)
    return blk, e, lo, hi, first


def kernel(x, c, w_ada, b_ada, w_in, w_pool, pool_scale, conv_w, conv_b, dt_bias, a_log, d_skip,
           ssd_norm_w, w_out, w_group, b_group, w_router, b_router, w13, w2, final_norm_w):
    b, s, d = x.shape
    depth = w_ada.shape[0]
    t = b * s
    pool_w = w_pool.shape[1] * w_pool.shape[2]
    ssd_w = SSD_HEADS * SSD_HEAD_DIM
    conv_dim = conv_w.shape[2]
    assert s % SEQ_TILE == 0 and SEQ_TILE % SSD_CHUNK == 0 and b % MIXER_SEQS == 0
    assert t % RANK_TILE == 0 and t % DISPATCH_TILE == 0 and s % FINAL_TILE == 0
    assert (t * TOP_K) % EXPERT_BLOCK == 0 and d % LANES == 0
    assert pool_w == 2 * LANES and len(POOL_WINDOWS) * POOL_GROUP_DIM == pool_w
    assert N_EXPERT_GROUPS + N_EXPERTS <= LANES and SSD_HEADS <= LANES
    assert depth == 1, "the final RMSNorm is fused into the last layer's combine step"

    for layer in range(depth):
        mod = _ada(c, w_ada[layer], b_ada[layer]).reshape(b, 6, d)

        o_dt = pool_w + ssd_w + conv_dim
        win = jnp.concatenate([w_in[layer][:, :o_dt], _pad_lanes(w_in[layer][:, o_dt:])], axis=1).astype(BF16)
        wp = jnp.zeros((pool_w, pool_w), F32)
        for g in range(len(POOL_WINDOWS)):
            sl = slice(g * POOL_GROUP_DIM, (g + 1) * POOL_GROUP_DIM)
            wp = wp.at[sl, sl].set(w_pool[layer, g])
        wp = wp.astype(BF16)
        w_route = _pad_lanes(jnp.concatenate([w_group[layer], w_router[layer]], axis=1))
        wrh = w_route.astype(BF16)
        wrl = (w_route - wrh.astype(F32)).astype(BF16)
        br = _pad_lanes(jnp.concatenate([b_group[layer], b_router[layer]])[None, :])

        x1, h2, route = _mixer(
            x, mod, win, wp, pool_scale[layer][None, :], conv_w[layer], conv_b[layer][None, :],
            _pad_lanes(dt_bias[layer][None, :]), _pad_lanes(a_log[layer][None, :], fill=NEG_BIG),
            jnp.repeat(d_skip[layer], SSD_HEAD_DIM)[None, :], ssd_norm_w[layer][None, :],
            w_out[layer].astype(BF16), wrh, wrl, br, pool_w=pool_w, ssd_w=ssd_w)

        counts_f = _count(route)
        start_f = jnp.cumsum(counts_f, axis=1) - counts_f
        dest = _rank(route, start_f)[:, :TOP_K].astype(jnp.int32)
        counts = counts_f[0, :N_EXPERTS].astype(jnp.int32)
        items = _work_items(counts, (t * TOP_K) // EXPERT_BLOCK)

        nrow = d // LANES
        xs = _dispatch(dest, h2, nrow)
        ys = _experts(*items, xs, w13[layer].astype(BF16), w2[layer].astype(BF16), nrow)
        x = _final(dest, x1, route, mod, final_norm_w[None, :], ys)
    return x
```

```python
import functools

import jax
import jax.numpy as jnp
from jax import lax
from jax.experimental import pallas as pl
from jax.experimental.pallas import tpu as pltpu

POOL_WINDOWS = (2, 4, 8, 16)
POOL_GROUP_DIM = 64
SSD_HEAD_DIM = 64
SSD_GROUPS = 4
SSD_HEADS_PER_GROUP = 3
SSD_HEADS = SSD_GROUPS * SSD_HEADS_PER_GROUP
SSD_STATE = 128
SSD_CONV = 4
SSD_CHUNK = 128
N_EXPERT_GROUPS = 4
EXPERTS_PER_GROUP = 8
N_EXPERTS = N_EXPERT_GROUPS * EXPERTS_PER_GROUP
TOP_K = 2
NORM_EPS = 1e-6

LANES = 128
SUBLANES = 8
VMEM_LIMIT_BYTES = 56 * 1024 * 1024

SEQ_TILE = 256
MIXER_SEQS = 2
CONV_HALO = SUBLANES
POOL_HALO = 16
ROW_COPY_TILE = 512
EXPERT_BLOCK = 512
EXPERT_CHAINS = 2
DMA_UNROLL = 8
DMA_THREADS = 2

NEG_BIG = -1e30
F32 = jnp.float32
BF16 = jnp.bfloat16


def _silu(v):
    half = 0.5 * v
    return half + half * jnp.tanh(half)


def _softplus(v):
    return jnp.maximum(v, 0.0) + jnp.log(1.0 + jnp.exp(-jnp.abs(v)))


def _rms(v):
    return v * lax.rsqrt(jnp.mean(v * v, axis=-1, keepdims=True) + NORM_EPS)


def _dot(a, b):
    return jnp.dot(a, b, preferred_element_type=F32)


def _dot_exact(a, b):
    return jnp.dot(a, b, preferred_element_type=F32, precision=lax.Precision.HIGHEST)


def _run_skewed(chains):
    live = [False] * len(chains)
    started = 0
    while started < len(chains) or any(live):
        if started < len(chains):
            live[started] = True
            started += 1
        for k, chain in enumerate(chains):
            if live[k]:
                try:
                    next(chain)
                except StopIteration:
                    live[k] = False


def _slab_chunks(ref, rows, nrow):
    return [ref[pl.ds(c, rows, stride=nrow), :] for c in range(nrow)]


def _slab_store(ref, val, nrow):
    rows = val.shape[0]
    for c in range(nrow):
        ref[pl.ds(c, rows, stride=nrow), :] = val[:, c * LANES:(c + 1) * LANES]


def _slab_of_row(ref, row, nrow):
    return ref.at[pl.ds(pl.multiple_of(row * nrow, nrow), nrow)]


def _ada_kernel(c_ref, w_ref, b_ref, o_ref):
    o_ref[...] = _dot_exact(_silu(c_ref[...]), w_ref[...]) + b_ref[...]


def _ada(c, w_ada, b_ada):
    b, d = c.shape
    n = w_ada.shape[1]
    return pl.pallas_call(
        _ada_kernel,
        grid=(n // d,),
        in_specs=[pl.BlockSpec((b, d), lambda i: (0, 0)),
                  pl.BlockSpec((d, d), lambda i: (0, i)),
                  pl.BlockSpec((1, d), lambda i: (0, i))],
        out_specs=pl.BlockSpec((b, d), lambda i: (0, i)),
        out_shape=jax.ShapeDtypeStruct((b, n), F32),
        compiler_params=pltpu.CompilerParams(dimension_semantics=("arbitrary",),
                                             vmem_limit_bytes=VMEM_LIMIT_BYTES),
        name="ada",
    )(c, w_ada, b_ada.reshape(1, n))


def _mixer_kernel(x_ref, mod_ref, win_ref, wp_ref, ps_ref, cw_ref, cb_ref, dtb_ref, alog_ref,
                  dsk_ref, nw_ref, wout_ref, wrh_ref, wrl_ref, br_ref, rep_ref,
                  x1_ref, h2_ref, route_ref, cnt_ref,
                  state_ref, xext_ref, uext_ref, *, pool_w, ssd_w):
    @pl.when((pl.program_id(0) == 0) & (pl.program_id(1) == 0))
    def _():
        cnt_ref[...] = jnp.zeros_like(cnt_ref)

    @pl.when(pl.program_id(1) == 0)
    def _():
        state_ref[...] = jnp.zeros_like(state_ref)
        xext_ref[:, 0:CONV_HALO, :] = jnp.zeros((xext_ref.shape[0], CONV_HALO, xext_ref.shape[2]), F32)
        uext_ref[:, 0:POOL_HALO, :] = jnp.zeros((uext_ref.shape[0], POOL_HALO, uext_ref.shape[2]), F32)

    _run_skewed([_mixer_tile(x_ref.at[bi], mod_ref.at[bi], win_ref, wp_ref, ps_ref, cw_ref, cb_ref, dtb_ref,
                             alog_ref, dsk_ref, nw_ref, wout_ref, wrh_ref, wrl_ref, br_ref, rep_ref,
                             x1_ref.at[bi], h2_ref.at[bi], route_ref.at[bi], cnt_ref,
                             state_ref.at[bi], xext_ref.at[bi], uext_ref.at[bi], pool_w=pool_w, ssd_w=ssd_w)
                 for bi in range(x_ref.shape[0])])


def _mixer_tile(x_ref, mod_ref, win_ref, wp_ref, ps_ref, cw_ref, cb_ref, dtb_ref, alog_ref,
                dsk_ref, nw_ref, wout_ref, wrh_ref, wrl_ref, br_ref, rep_ref,
                x1_ref, h2_ref, route_ref, cnt_ref,
                state_ref, xext_ref, uext_ref, *, pool_w, ssd_w):
    ts = x_ref.shape[0]
    d = x_ref.shape[1]
    L = SSD_CHUNK
    N = SSD_STATE
    P = SSD_HEAD_DIM
    j = pl.program_id(1)

    x = x_ref[...]
    mod = mod_ref[...]
    sh1, sc1, g1 = mod[0:1], mod[1:2], mod[2:3]
    sh2, sc2 = mod[3:4], mod[4:5]

    h = (_rms(x) * (1.0 + sc1) + sh1).astype(BF16)
    yield
    proj = _dot(h, win_ref[...])
    yield
    o_z = pool_w
    o_x = pool_w + ssd_w
    conv_dim = ssd_w + 2 * SSD_GROUPS * N
    o_dt = o_x + conv_dim
    u = proj[:, 0:pool_w]
    z = proj[:, o_z:o_x]
    xbc = proj[:, o_x:o_dt]
    dtr = proj[:, o_dt:o_dt + LANES]

    uext_ref[POOL_HALO:POOL_HALO + ts, :] = u
    lane_u = lax.broadcasted_iota(jnp.int32, (1, LANES), 1)
    upper = lane_u >= POOL_GROUP_DIM

    def ush(k, col):
        return uext_ref[POOL_HALO - k:POOL_HALO - k + ts, col * LANES:(col + 1) * LANES]

    w0, w1, w2, w3 = POOL_WINDOWS
    lo_a = ush(0, 0)
    for k in range(1, w0):
        lo_a = lo_a + ush(k, 0)
    lo_b = ush(w0, 0)
    for k in range(w0 + 1, w1):
        lo_b = lo_b + ush(k, 0)
    hi_a = ush(0, 1)
    for k in range(1, w2):
        hi_a = hi_a + ush(k, 1)
    hi_b = ush(w2, 1)
    for k in range(w2 + 1, w3):
        hi_b = hi_b + ush(k, 1)
    win_lo = lo_a + jnp.where(upper, lo_b, 0.0)
    win_hi = hi_a + jnp.where(upper, hi_b, 0.0)
    pos = (j * ts + 1 + lax.broadcasted_iota(jnp.int32, (ts, 1), 0)).astype(F32)
    cnt_lo = jnp.minimum(pos, jnp.where(upper, float(w1), float(w0)))
    cnt_hi = jnp.minimum(pos, jnp.where(upper, float(w3), float(w2)))
    pdiff = jnp.concatenate([win_lo / cnt_lo, win_hi / cnt_hi], axis=1) - u
    y_pool = _dot(pdiff.astype(BF16), wp_ref[...]) * ps_ref[...]
    uext_ref[0:POOL_HALO, :] = uext_ref[ts:ts + POOL_HALO, :]

    xext_ref[CONV_HALO:CONV_HALO + ts, :] = xbc
    cw = cw_ref[...]
    xe = xext_ref[...]
    acc = cw[0:1] * xe
    for k in range(1, SSD_CONV):
        acc = cw[k:k + 1] * xe + pltpu.roll(acc, 1, axis=0)
    xa = _silu(cb_ref[...] + acc[CONV_HALO:CONV_HALO + ts])
    xext_ref[0:CONV_HALO, :] = xext_ref[ts:ts + CONV_HALO, :]
    yield

    a_neg = -jnp.exp(alog_ref[...])
    row_i = lax.broadcasted_iota(jnp.int32, (L, L), 0)
    col_i = lax.broadcasted_iota(jnp.int32, (L, L), 1)
    causal = row_i >= col_i
    tril = causal.astype(F32)
    hpt = LANES // P
    lane_h = lax.broadcasted_iota(jnp.int32, (1, LANES), 1)
    y_chunks = []
    for c in range(ts // L):
        r0 = c * L
        xs_c = xa[r0:r0 + L, 0:ssd_w]
        b_c = xa[r0:r0 + L, ssd_w:ssd_w + SSD_GROUPS * N]
        c_c = xa[r0:r0 + L, ssd_w + SSD_GROUPS * N:conv_dim]
        dt = _softplus(dtr[r0:r0 + L] + dtb_ref[...])
        a_cs = _dot_exact(tril, dt * a_neg)
        a_hi = a_cs.astype(BF16)
        a_lo = (a_cs - a_hi.astype(F32)).astype(BF16)
        a_q = a_hi.astype(F32) + a_lo.astype(F32)
        a_rep = _dot(jnp.concatenate([a_hi, a_lo], axis=1), rep_ref[...])
        a_q_t = a_q.T
        dt_t = dt.T
        cbs, cgs, bg_ts = [], [], []
        for g in range(SSD_GROUPS):
            bg = b_c[:, g * N:(g + 1) * N]
            cg = c_c[:, g * N:(g + 1) * N]
            cbs.append(lax.dot_general(cg.astype(BF16), bg.astype(BF16), (((1,), (1,)), ((), ())),
                                       preferred_element_type=F32))
            cgs.append(cg)
            bg_ts.append(bg.T)
        y_tiles = []
        for q in range(SSD_HEADS // hpt):
            xs_q = xs_c[:, q * LANES:(q + 1) * LANES].astype(BF16)
            st = state_ref[q]
            rhs = jnp.concatenate([xs_q, st.astype(BF16)], axis=0)
            y_q = upd_q = keep_q = None
            for hh in range(hpt):
                hd = q * hpt + hh
                g = hd // SSD_HEADS_PER_GROUP
                a_col = a_rep[:, hd * LANES:(hd + 1) * LANES]
                a_row = a_q_t[hd:hd + 1, :]
                dt_row = dt_t[hd:hd + 1, :]
                decay = jnp.exp(jnp.where(causal, a_col - a_row, NEG_BIG))
                m = cbs[g] * decay * dt_row
                cs = cgs[g] * jnp.exp(a_col)
                lhs = jnp.concatenate([m, cs], axis=1).astype(BF16)
                y_h = _dot(lhs, rhs)
                a_end = a_row[:, L - 1:L]
                w_row = dt_row * jnp.exp(a_end - a_row)
                upd_h = _dot((bg_ts[g] * w_row).astype(BF16), xs_q)
                keep_h = jnp.exp(a_end)
                if hh == 0:
                    y_q, upd_q, keep_q = y_h, upd_h, keep_h
                else:
                    mine = (lane_h >= hh * P) & (lane_h < (hh + 1) * P)
                    y_q = jnp.where(mine, y_h, y_q)
                    upd_q = jnp.where(mine, upd_h, upd_q)
                    keep_q = jnp.where(mine, keep_h, keep_q)
            y_tiles.append(y_q)
            state_ref[q] = keep_q * st + upd_q
        y_chunks.append(jnp.concatenate(y_tiles, axis=1))
        yield
    y = jnp.concatenate(y_chunks, axis=0) if len(y_chunks) > 1 else y_chunks[0]
    y = y + xa[:, 0:ssd_w] * dsk_ref[...]
    y = y * _silu(z)

    gw = ssd_w // SSD_GROUPS
    lane_y = lax.broadcasted_iota(jnp.int32, (1, ssd_w), 1)
    y2 = y * y
    scale = jnp.zeros_like(y)
    for g in range(SSD_GROUPS):
        in_g = (lane_y >= g * gw) & (lane_y < (g + 1) * gw)
        ms = jnp.sum(jnp.where(in_g, y2, 0.0), axis=-1, keepdims=True) * (1.0 / gw)
        scale = scale + jnp.where(in_g, lax.rsqrt(ms + NORM_EPS), 0.0)
    mix_in = jnp.concatenate([y_pool, y * scale * nw_ref[...]], axis=1).astype(BF16)
    yield
    mix = _dot(mix_in, wout_ref[...])
    yield
    x1 = x + g1 * mix
    x1_ref[...] = x1

    h2 = _rms(x1) * (1.0 + sc2) + sh2
    _slab_store(h2_ref, h2, d // LANES)
    h_hi = h2.astype(BF16)
    h_lo = (h2 - h_hi.astype(F32)).astype(BF16)
    yield
    logits = (_dot(h_hi, wrh_ref[...]) + _dot(h_lo, wrh_ref[...]) + _dot(h_hi, wrl_ref[...])
              + br_ref[...])
    lane = lax.broadcasted_iota(jnp.int32, (1, LANES), 1).astype(F32)
    big = float(LANES)
    gl = jnp.where(lane < N_EXPERT_GROUPS, logits, NEG_BIG)
    gmax = jnp.max(gl, axis=-1, keepdims=True)
    gsum = jnp.sum(jnp.exp(gl - gmax), axis=-1, keepdims=True)
    p_g = 1.0 / gsum
    g_idx = jnp.min(jnp.where(gl == gmax, lane, big), axis=-1, keepdims=True)
    lo = N_EXPERT_GROUPS + EXPERTS_PER_GROUP * g_idx
    el = jnp.where((lane >= lo) & (lane < lo + EXPERTS_PER_GROUP), logits, NEG_BIG)
    v1 = jnp.max(el, axis=-1, keepdims=True)
    i1 = jnp.min(jnp.where(el == v1, lane, big), axis=-1, keepdims=True)
    el2 = jnp.where(lane == i1, NEG_BIG, el)
    v2 = jnp.max(el2, axis=-1, keepdims=True)
    i2 = jnp.min(jnp.where(el2 == v2, lane, big), axis=-1, keepdims=True)
    e21 = jnp.exp(v2 - v1)
    gate1 = p_g / (1.0 + e21)
    gate2 = p_g * e21 / (1.0 + e21)
    route = jnp.where(lane == 0, i1 - N_EXPERT_GROUPS,
                      jnp.where(lane == 1, i2 - N_EXPERT_GROUPS,
                                jnp.where(lane == 2, gate1, jnp.where(lane == 3, gate2, 0.0))))
    route_ref[...] = route
    chosen = (lane == i1 - N_EXPERT_GROUPS) | (lane == i2 - N_EXPERT_GROUPS)
    cnt_ref[...] += jnp.sum(jnp.where(chosen, 1.0, 0.0), axis=0, keepdims=True)


def _mixer(x, mod, win, wp, ps, cw, cb, dtb, alog, dsk, nw, wout, wrh, wrl, br, *, pool_w, ssd_w):
    k_idx = jnp.arange(2 * LANES, dtype=jnp.int32)[:, None] % LANES
    h_idx = jnp.arange(SSD_HEADS * LANES, dtype=jnp.int32)[None, :] // LANES
    rep = (k_idx == h_idx).astype(BF16)
    b, s, d = x.shape
    ts = SEQ_TILE
    conv_dim = cw.shape[1]
    nrow = d // LANES

    def full(a):
        nd = a.ndim
        return pl.BlockSpec(a.shape, lambda i, j, _nd=nd: (0,) * _nd)

    nb = MIXER_SEQS
    seq_tile = lambda i, j: (i, j, 0)
    x1, h2, route, counts = pl.pallas_call(
        functools.partial(_mixer_kernel, pool_w=pool_w, ssd_w=ssd_w),
        grid=(b // nb, s // ts),
        in_specs=[pl.BlockSpec((nb, ts, d), seq_tile),
                  pl.BlockSpec((nb, 6, d), lambda i, j: (i, 0, 0)),
                  full(win), full(wp), full(ps), full(cw), full(cb), full(dtb), full(alog),
                  full(dsk), full(nw), full(wout), full(wrh), full(wrl), full(br), full(rep)],
        out_specs=[pl.BlockSpec((nb, ts, d), seq_tile),
                   pl.BlockSpec((nb, ts * nrow, LANES), seq_tile),
                   pl.BlockSpec((nb, ts, LANES), seq_tile),
                   pl.BlockSpec((1, LANES), lambda i, j: (0, 0))],
        out_shape=[jax.ShapeDtypeStruct((b, s, d), F32),
                   jax.ShapeDtypeStruct((b, s * nrow, LANES), F32),
                   jax.ShapeDtypeStruct((b, s, LANES), F32),
                   jax.ShapeDtypeStruct((1, LANES), F32)],
        scratch_shapes=[pltpu.VMEM((nb, ssd_w // LANES, SSD_STATE, LANES), F32),
                        pltpu.VMEM((nb, CONV_HALO + ts, conv_dim), F32),
                        pltpu.VMEM((nb, POOL_HALO + ts, pool_w), F32)],
        compiler_params=pltpu.CompilerParams(dimension_semantics=("arbitrary", "arbitrary"),
                                             vmem_limit_bytes=VMEM_LIMIT_BYTES),
        name="mixer",
    )(x, mod, win, wp, ps, cw, cb, dtb, alog, dsk, nw, wout, wrh, wrl, br, rep)
    return x1, h2.reshape(b * s * nrow, LANES), route.reshape(b * s, LANES), counts


def _rank_kernel(route_ref, start_ref, dest_ref, carry_ref):
    ta = route_ref.shape[0]

    @pl.when(pl.program_id(0) == 0)
    def _():
        carry_ref[...] = start_ref[...]

    route_t = route_ref[...].T
    e_idx = lax.broadcasted_iota(jnp.int32, (LANES, ta), 0).astype(F32)
    ohs = [jnp.where(e_idx == route_t[k:k + 1, :], 1.0, 0.0) for k in range(TOP_K)]
    oh = (ohs[0] + ohs[1]).astype(BF16)
    earlier = (lax.broadcasted_iota(jnp.int32, (ta, ta), 0)
               < lax.broadcasted_iota(jnp.int32, (ta, ta), 1))
    before = _dot(oh, jnp.where(earlier, 1.0, 0.0).astype(BF16))
    carry = carry_ref[...]
    base = jnp.concatenate([carry] * (ta // LANES), axis=1) + before
    for k in range(TOP_K):
        dest_ref[0, k:k + 1, :] = jnp.sum(ohs[k] * base, axis=0, keepdims=True).astype(jnp.int32)
    carry_ref[...] = carry + _dot(oh, jnp.ones((ta, LANES), BF16))


def _rank(route, start_rep):
    t = route.shape[0]
    return pl.pallas_call(
        _rank_kernel,
        grid=(t // ROW_COPY_TILE,),
        in_specs=[pl.BlockSpec((ROW_COPY_TILE, LANES), lambda i: (i, 0)),
                  pl.BlockSpec((LANES, LANES), lambda i: (0, 0))],
        out_specs=pl.BlockSpec((1, TOP_K, ROW_COPY_TILE), lambda i: (i, 0, 0)),
        out_shape=jax.ShapeDtypeStruct((t // ROW_COPY_TILE, TOP_K, ROW_COPY_TILE), jnp.int32),
        scratch_shapes=[pltpu.VMEM((LANES, LANES), F32)],
        compiler_params=pltpu.CompilerParams(dimension_semantics=("arbitrary",)),
        name="rank",
    )(route, start_rep)


def _dispatch_kernel(dest_ref, h2_ref, xs_hbm, sem, *, nrow):
    tile_rows = h2_ref.shape[0]

    def issue(blk, carry):
        for u in range(DMA_UNROLL):
            tk = blk * DMA_UNROLL + u
            for k in range(TOP_K):
                pltpu.make_async_copy(_slab_of_row(h2_ref, tk, nrow),
                                      _slab_of_row(xs_hbm, dest_ref[0, k, tk], nrow),
                                      sem).start(priority=(u * TOP_K + k) % DMA_THREADS)
        return carry

    lax.fori_loop(0, dest_ref.shape[2] // DMA_UNROLL, issue, 0)

    for _ in range(TOP_K):
        pltpu.make_async_copy(h2_ref, xs_hbm.at[pl.ds(0, tile_rows)], sem).wait()


def _dispatch(dest, h2, nrow):
    t = h2.shape[0] // nrow
    return pl.pallas_call(
        functools.partial(_dispatch_kernel, nrow=nrow),
        grid=(t // ROW_COPY_TILE,),
        in_specs=[pl.BlockSpec((1, TOP_K, ROW_COPY_TILE), lambda i: (i, 0, 0), memory_space=pltpu.SMEM),
                  pl.BlockSpec((ROW_COPY_TILE * nrow, LANES), lambda i: (i, 0))],
        out_specs=pl.BlockSpec(memory_space=pl.ANY),
        out_shape=jax.ShapeDtypeStruct((t * TOP_K * nrow, LANES), F32),
        scratch_shapes=[pltpu.SemaphoreType.DMA(())],
        compiler_params=pltpu.CompilerParams(dimension_semantics=("arbitrary",),
                                             has_side_effects=True),
        name="dispatch",
    )(dest, h2)


def _expert_kernel(blk_ref, exp_ref, lo_ref, hi_ref, first_ref, new_ref, xs_ref, w13_ref, w2_ref, ys_ref,
                   w13_bf, w2_bf, *, nrow):
    i = pl.program_id(0)
    rows = xs_ref.shape[0] // nrow
    lo = lo_ref[i]
    hi = hi_ref[i]

    sub = rows // EXPERT_CHAINS

    @pl.when(new_ref[i] == 1)
    def _():
        w13_bf[...] = w13_ref[0].astype(BF16)
        w2_bf[...] = w2_ref[0].astype(BF16)

    def chain(r0):
        xb = jnp.concatenate([xs_ref[pl.ds(r0 * nrow + c, sub, stride=nrow), :].astype(BF16)
                              for c in range(nrow)], axis=1)
        yield
        hu = _dot(xb, w13_bf[...])
        yield
        f = hu.shape[1] // 2
        act = (_silu(hu[:, :f]) * hu[:, f:]).astype(BF16)
        yield
        y = _dot(act, w2_bf[...])
        yield
        ridx = r0 + lax.broadcasted_iota(jnp.int32, (sub, 1), 0)
        take = (ridx >= lo_eff) & (ridx < hi_eff)
        for c in range(nrow):
            sl = pl.ds(r0 * nrow + c, sub, stride=nrow)
            ys_ref[sl, :] = jnp.where(take, y[:, c * LANES:(c + 1) * LANES], ys_ref[sl, :])

    is_first = first_ref[i] == 1
    lo_eff = jnp.where(is_first, 0, lo)
    hi_eff = jnp.where(is_first, rows, hi)

    @pl.when(hi > lo)
    def _():
        _run_skewed([chain(k * sub) for k in range(EXPERT_CHAINS)])


def _experts(item_blk, item_exp, item_lo, item_hi, item_first, item_new, xs, w13, w2, nrow):
    a = xs.shape[0] // nrow
    n_items = item_blk.shape[0]
    d = w13.shape[1]
    ff2 = w13.shape[2]
    grid_spec = pltpu.PrefetchScalarGridSpec(
        num_scalar_prefetch=6,
        grid=(n_items,),
        in_specs=[pl.BlockSpec((EXPERT_BLOCK * nrow, LANES), lambda i, b, e, lo, hi, fr, nw: (b[i], 0)),
                  pl.BlockSpec((1, d, ff2), lambda i, b, e, lo, hi, fr, nw: (e[i], 0, 0)),
                  pl.BlockSpec((1, ff2 // 2, d), lambda i, b, e, lo, hi, fr, nw: (e[i], 0, 0))],
        out_specs=pl.BlockSpec((EXPERT_BLOCK * nrow, LANES), lambda i, b, e, lo, hi, fr, nw: (b[i], 0)),
        scratch_shapes=[pltpu.VMEM((d, ff2), BF16), pltpu.VMEM((ff2 // 2, d), BF16)],
    )
    return pl.pallas_call(
        functools.partial(_expert_kernel, nrow=nrow),
        grid_spec=grid_spec,
        out_shape=jax.ShapeDtypeStruct((a * nrow, LANES), F32),
        compiler_params=pltpu.CompilerParams(dimension_semantics=("arbitrary",),
                                             vmem_limit_bytes=VMEM_LIMIT_BYTES),
        name="experts",
    )(item_blk, item_exp, item_lo, item_hi, item_first, item_new, xs, w13, w2)


def _final_kernel(dest_ref, dnext_ref, x1_ref, route_ref, mod_ref, fw_ref, ys_hbm, o_ref, bufs, sems):
    tf = x1_ref.shape[1]
    nrow = bufs.shape[2] // tf
    step = pl.program_id(0)
    slot = lax.rem(step, 2)

    def start_gathers(d_ref, to_slot):
        def issue(blk, carry):
            for u in range(DMA_UNROLL):
                tk = blk * DMA_UNROLL + u
                for k in range(TOP_K):
                    pltpu.make_async_copy(_slab_of_row(ys_hbm, d_ref[0, k, tk], nrow),
                                          _slab_of_row(bufs.at[to_slot, k], tk, nrow),
                                          sems.at[to_slot]).start(priority=(u * TOP_K + k) % DMA_THREADS)
            return carry

        lax.fori_loop(0, tf // DMA_UNROLL, issue, 0)

    @pl.when(step == 0)
    def _():
        start_gathers(dest_ref, 0)

    @pl.when(step + 1 < pl.num_programs(0))
    def _():
        start_gathers(dnext_ref, 1 - slot)

    for k in range(TOP_K):
        pltpu.make_async_copy(ys_hbm.at[pl.ds(0, tf * nrow)], bufs.at[slot, k], sems.at[slot]).wait()

    r = route_ref[...]
    gate1 = r[:, 2:3]
    gate2 = r[:, 3:4]
    y = jnp.concatenate([c1 * gate1 + c2 * gate2 for c1, c2 in
                         zip(_slab_chunks(bufs.at[slot, 0], tf, nrow),
                             _slab_chunks(bufs.at[slot, 1], tf, nrow))], axis=1)
    g2 = mod_ref[0][5:6]
    o_ref[0] = _rms(x1_ref[0] + g2 * y) * fw_ref[...]


def _final(dest, x1, route, mod, fw, ys):
    b, s, d = x1.shape
    tf = ROW_COPY_TILE
    nrow = d // LANES
    spt = s // tf
    n_steps = b * spt
    return pl.pallas_call(
        _final_kernel,
        grid=(n_steps,),
        in_specs=[pl.BlockSpec((1, TOP_K, tf), lambda i: (i, 0, 0), memory_space=pltpu.SMEM),
                  pl.BlockSpec((1, TOP_K, tf), lambda i: (jnp.minimum(i + 1, n_steps - 1), 0, 0),
                               memory_space=pltpu.SMEM),
                  pl.BlockSpec((1, tf, d), lambda i: (i // spt, i % spt, 0)),
                  pl.BlockSpec((tf, LANES), lambda i: (i, 0)),
                  pl.BlockSpec((1, 6, d), lambda i: (i // spt, 0, 0)),
                  pl.BlockSpec((1, d), lambda i: (0, 0)),
                  pl.BlockSpec(memory_space=pl.ANY)],
        out_specs=pl.BlockSpec((1, tf, d), lambda i: (i // spt, i % spt, 0)),
        out_shape=jax.ShapeDtypeStruct((b, s, d), F32),
        scratch_shapes=[pltpu.VMEM((2, TOP_K, tf * nrow, LANES), F32),
                        pltpu.SemaphoreType.DMA((2,))],
        compiler_params=pltpu.CompilerParams(dimension_semantics=("arbitrary",),
                                             vmem_limit_bytes=VMEM_LIMIT_BYTES),
        name="final",
    )(dest, dest, x1, route, mod, fw, ys)


def _pad_lanes(v, fill=0.0):
    n = v.shape[-1]
    return jnp.pad(v, [(0, 0)] * (v.ndim - 1) + [(0, LANES - n)], constant_values=fill)


def _work_items(counts, n_blocks):
    n_items = n_blocks + N_EXPERTS - 1
    ends = jnp.cumsum(counts)
    starts = ends - counts
    first_blk = starts // EXPERT_BLOCK
    last_blk = jnp.maximum(ends - 1, starts) // EXPERT_BLOCK
    n_e = jnp.where(counts > 0, last_blk - first_blk + 1, 0)
    item_end = jnp.cumsum(n_e)
    item_start = item_end - n_e
    ids = jnp.arange(n_items, dtype=jnp.int32)
    total = item_end[-1]
    ids_c = jnp.minimum(ids, total - 1)
    e = jnp.sum((item_end[None, :] <= ids_c[:, None]).astype(jnp.int32), axis=1)
    onehot = (e[:, None] == jnp.arange(N_EXPERTS, dtype=jnp.int32)[None, :]).astype(jnp.int32)
    pick = lambda v: jnp.sum(onehot * v[None, :], axis=1)
    blk = pick(first_blk) + ids_c - pick(item_start)
    lo = jnp.clip(pick(starts) - blk * EXPERT_BLOCK, 0, EXPERT_BLOCK)
    hi = jnp.clip(pick(ends) - blk * EXPERT_BLOCK, 0, EXPERT_BLOCK)
    hi = jnp.where(ids < total, hi, lo)
    prev_blk = jnp.concatenate([jnp.full((1,), -1, jnp.int32), blk[:-1]])
    first = (blk != prev_blk).astype(jnp.int32)
    prev_e = jnp.concatenate([jnp.full((1,), -1, jnp.int32), e[:-1]])
    new_expert = (e != prev_e).astype(jnp.int32)
    return blk, e, lo, hi, first, new_expert


def kernel(x, c, w_ada, b_ada, w_in, w_pool, pool_scale, conv_w, conv_b, dt_bias, a_log, d_skip,
           ssd_norm_w, w_out, w_group, b_group, w_router, b_router, w13, w2, final_norm_w):
    b, s, d = x.shape
    depth = w_ada.shape[0]
    t = b * s
    pool_w = w_pool.shape[1] * w_pool.shape[2]
    ssd_w = SSD_HEADS * SSD_HEAD_DIM
    conv_dim = conv_w.shape[2]
    assert s % SEQ_TILE == 0 and SEQ_TILE % SSD_CHUNK == 0 and b % MIXER_SEQS == 0
    assert s % ROW_COPY_TILE == 0 and ROW_COPY_TILE % LANES == 0
    assert (t * TOP_K) % EXPERT_BLOCK == 0 and d % LANES == 0
    assert pool_w == 2 * LANES and len(POOL_WINDOWS) * POOL_GROUP_DIM == pool_w
    assert N_EXPERT_GROUPS + N_EXPERTS <= LANES and SSD_HEADS <= LANES
    assert depth == 1, "the final RMSNorm is fused into the last layer's combine step"

    for layer in range(depth):
        mod = _ada(c, w_ada[layer], b_ada[layer]).reshape(b, 6, d)

        o_dt = pool_w + ssd_w + conv_dim
        win = jnp.concatenate([w_in[layer][:, :o_dt], _pad_lanes(w_in[layer][:, o_dt:])], axis=1).astype(BF16)
        wp = jnp.zeros((pool_w, pool_w), F32)
        for g in range(len(POOL_WINDOWS)):
            sl = slice(g * POOL_GROUP_DIM, (g + 1) * POOL_GROUP_DIM)
            wp = wp.at[sl, sl].set(w_pool[layer, g])
        wp = wp.astype(BF16)
        w_route = _pad_lanes(jnp.concatenate([w_group[layer], w_router[layer]], axis=1))
        wrh = w_route.astype(BF16)
        wrl = (w_route - wrh.astype(F32)).astype(BF16)
        br = _pad_lanes(jnp.concatenate([b_group[layer], b_router[layer]])[None, :])

        x1, h2, route, counts_f = _mixer(
            x, mod, win, wp, pool_scale[layer][None, :], conv_w[layer], conv_b[layer][None, :],
            _pad_lanes(dt_bias[layer][None, :]), _pad_lanes(a_log[layer][None, :], fill=NEG_BIG),
            jnp.repeat(d_skip[layer], SSD_HEAD_DIM)[None, :], ssd_norm_w[layer][None, :],
            w_out[layer].astype(BF16), wrh, wrl, br, pool_w=pool_w, ssd_w=ssd_w)

        start_f = jnp.cumsum(counts_f, axis=1) - counts_f
        start_rep = jnp.broadcast_to(start_f.reshape(LANES, 1), (LANES, LANES))
        dest = _rank(route, start_rep)
        counts = counts_f[0, :N_EXPERTS].astype(jnp.int32)
        items = _work_items(counts, (t * TOP_K) // EXPERT_BLOCK)

        nrow = d // LANES
        xs = _dispatch(dest, h2, nrow)
        ys = _experts(*items, xs, w13[layer], w2[layer], nrow)
        x = _final(dest, x1, route, mod, final_norm_w[None, :], ys)
    return x
```

```python
import functools

import jax
import jax.numpy as jnp
from jax import lax
from jax.experimental import pallas as pl
from jax.experimental.pallas import tpu as pltpu

POOL_WINDOWS = (2, 4, 8, 16)
POOL_GROUP_DIM = 64
SSD_HEAD_DIM = 64
SSD_GROUPS = 4
SSD_HEADS_PER_GROUP = 3
SSD_HEADS = SSD_GROUPS * SSD_HEADS_PER_GROUP
SSD_STATE = 128
SSD_CONV = 4
SSD_CHUNK = 128
N_EXPERT_GROUPS = 4
EXPERTS_PER_GROUP = 8
N_EXPERTS = N_EXPERT_GROUPS * EXPERTS_PER_GROUP
TOP_K = 2
NORM_EPS = 1e-6

LANES = 128
SUBLANES = 8
VMEM_LIMIT_BYTES = 56 * 1024 * 1024

SEQ_TILE = 256
MIXER_SEQS = 2
PROJ_PIECE = 1024
MIXER_ORDER = (0, 1, 0, 0, 0,
               1, 0, 0, 0,
               1, 0,
               1, 0,
               0, 1, 1, 1,
               0, 1,
               0, 0, 1,
               1, 1, 1, 1)
CONV_PAD = SUBLANES
CONV_PITCH = SUBLANES + 1
POOL_HALO = 16
ROW_COPY_TILE = 512
EXPERT_BLOCK = 512
EXPERT_CHAINS = 2
DISPATCH_PARTS = 2
DMA_UNROLL = 8
DMA_THREADS = 2

NEG_BIG = -1e30
F32 = jnp.float32
BF16 = jnp.bfloat16


def _silu(v):
    half = 0.5 * v
    return half + half * jnp.tanh(half)


def _softplus(v):
    return jnp.maximum(v, 0.0) + jnp.log(1.0 + jnp.exp(-jnp.abs(v)))


def _rms(v):
    return v * lax.rsqrt(jnp.mean(v * v, axis=-1, keepdims=True) + NORM_EPS)


def _dot(a, b):
    return jnp.dot(a, b, preferred_element_type=F32)


def _dot_exact(a, b):
    return jnp.dot(a, b, preferred_element_type=F32, precision=lax.Precision.HIGHEST)


def _run_skewed(chains):
    live = [False] * len(chains)
    started = 0
    while started < len(chains) or any(live):
        if started < len(chains):
            live[started] = True
            started += 1
        for k, chain in enumerate(chains):
            if live[k]:
                try:
                    next(chain)
                except StopIteration:
                    live[k] = False


def _run_ordered(chains, order):
    for k in order:
        next(chains[k], None)
    for chain in chains:
        assert next(chain, "done") == "done", "order does not cover every phase"


def _slab_chunks(ref, rows, nrow):
    return [ref[pl.ds(c, rows, stride=nrow), :] for c in range(nrow)]


def _slab_store(ref, val, nrow):
    rows = val.shape[0]
    for c in range(nrow):
        ref[pl.ds(c, rows, stride=nrow), :] = val[:, c * LANES:(c + 1) * LANES]


def _slab_of_row(ref, row, nrow):
    return ref.at[pl.ds(pl.multiple_of(row * nrow, nrow), nrow)]


def _ada_kernel(c_ref, w_ref, b_ref, o_ref):
    o_ref[...] = _dot_exact(_silu(c_ref[...]), w_ref[...]) + b_ref[...]


def _ada(c, w_ada, b_ada):
    b, d = c.shape
    n = w_ada.shape[1]
    return pl.pallas_call(
        _ada_kernel,
        grid=(n // d,),
        in_specs=[pl.BlockSpec((b, d), lambda i: (0, 0)),
                  pl.BlockSpec((d, d), lambda i: (0, i)),
                  pl.BlockSpec((1, d), lambda i: (0, i))],
        out_specs=pl.BlockSpec((b, d), lambda i: (0, i)),
        out_shape=jax.ShapeDtypeStruct((b, n), F32),
        compiler_params=pltpu.CompilerParams(dimension_semantics=("arbitrary",),
                                             vmem_limit_bytes=VMEM_LIMIT_BYTES),
        name="ada",
    )(c, w_ada, b_ada.reshape(1, n))


def _mixer_kernel(x_ref, mod_ref, win_ref, wp_ref, ps_ref, cw_ref, cb_ref, dtb_ref, alog_ref,
                  dsk_ref, nw_ref, wout_ref, wrh_ref, wrl_ref, br_ref, rep_ref,
                  x1_ref, h2_ref, route_ref, cnt_ref,
                  state_ref, xext_ref, uext_ref, xa_ref, *, pool_w, ssd_w):
    @pl.when((pl.program_id(0) == 0) & (pl.program_id(1) == 0))
    def _():
        cnt_ref[...] = jnp.zeros_like(cnt_ref)
        xext_ref[...] = jnp.zeros_like(xext_ref)

    @pl.when(pl.program_id(1) == 0)
    def _():
        state_ref[...] = jnp.zeros_like(state_ref)
        pc = x_ref.shape[1] // SUBLANES
        for bi in range(xext_ref.shape[0]):
            for lt in range(xext_ref.shape[1]):
                xext_ref[bi, lt, pl.ds(CONV_PAD * CONV_PITCH, pc, stride=CONV_PITCH), :] = (
                    jnp.zeros((pc, LANES), F32))
        uext_ref[:, 0:POOL_HALO, :] = jnp.zeros((uext_ref.shape[0], POOL_HALO, uext_ref.shape[2]), F32)

    _run_ordered([_mixer_tile(x_ref.at[bi], mod_ref.at[bi], win_ref, wp_ref, ps_ref, cw_ref, cb_ref, dtb_ref,
                             alog_ref, dsk_ref, nw_ref, wout_ref, wrh_ref, wrl_ref, br_ref, rep_ref,
                             x1_ref.at[bi], h2_ref.at[bi], route_ref.at[bi], cnt_ref,
                             state_ref.at[bi], xext_ref.at[bi], uext_ref.at[bi], xa_ref.at[bi], pool_w=pool_w, ssd_w=ssd_w)
                  for bi in range(x_ref.shape[0])], MIXER_ORDER)


def _mixer_tile(x_ref, mod_ref, win_ref, wp_ref, ps_ref, cw_ref, cb_ref, dtb_ref, alog_ref,
                dsk_ref, nw_ref, wout_ref, wrh_ref, wrl_ref, br_ref, rep_ref,
                x1_ref, h2_ref, route_ref, cnt_ref,
                state_ref, xext_ref, uext_ref, xa_ref, *, pool_w, ssd_w):
    ts = x_ref.shape[0]
    d = x_ref.shape[1]
    L = SSD_CHUNK
    N = SSD_STATE
    P = SSD_HEAD_DIM
    j = pl.program_id(1)

    x = x_ref[...]
    mod = mod_ref[...]
    sh1, sc1, g1 = mod[0:1], mod[1:2], mod[2:3]
    sh2, sc2 = mod[3:4], mod[4:5]

    h = (_rms(x) * (1.0 + sc1) + sh1).astype(BF16)
    yield
    n_proj = win_ref.shape[1]
    edges = [min(n_proj, e * PROJ_PIECE) for e in range(pl.cdiv(n_proj, PROJ_PIECE) + 1)]
    pieces = []
    for a, b in zip(edges[:-1], edges[1:]):
        pieces.append(_dot(h, win_ref[:, a:b]))
        yield
    proj = jnp.concatenate(pieces, axis=1)
    o_z = pool_w
    o_x = pool_w + ssd_w
    conv_dim = ssd_w + 2 * SSD_GROUPS * N
    o_dt = o_x + conv_dim
    u = proj[:, 0:pool_w]
    z = proj[:, o_z:o_x]
    xbc = proj[:, o_x:o_dt]
    dtr = proj[:, o_dt:o_dt + LANES]

    uext_ref[POOL_HALO:POOL_HALO + ts, :] = u
    lane_u = lax.broadcasted_iota(jnp.int32, (1, LANES), 1)
    upper = lane_u >= POOL_GROUP_DIM

    def ush(k, col):
        return uext_ref[POOL_HALO - k:POOL_HALO - k + ts, col * LANES:(col + 1) * LANES]

    w0, w1, w2, w3 = POOL_WINDOWS
    lo_a = ush(0, 0)
    for k in range(1, w0):
        lo_a = lo_a + ush(k, 0)
    lo_b = ush(w0, 0)
    for k in range(w0 + 1, w1):
        lo_b = lo_b + ush(k, 0)
    hi_a = ush(0, 1)
    for k in range(1, w2):
        hi_a = hi_a + ush(k, 1)
    hi_b = ush(w2, 1)
    for k in range(w2 + 1, w3):
        hi_b = hi_b + ush(k, 1)
    win_lo = lo_a + jnp.where(upper, lo_b, 0.0)
    win_hi = hi_a + jnp.where(upper, hi_b, 0.0)
    pos = (j * ts + 1 + lax.broadcasted_iota(jnp.int32, (ts, 1), 0)).astype(F32)
    cnt_lo = jnp.minimum(pos, jnp.where(upper, float(w1), float(w0)))
    cnt_hi = jnp.minimum(pos, jnp.where(upper, float(w3), float(w2)))
    pdiff = jnp.concatenate([win_lo / cnt_lo, win_hi / cnt_hi], axis=1) - u
    y_pool = _dot(pdiff.astype(BF16), wp_ref[...]) * ps_ref[...]
    uext_ref[0:POOL_HALO, :] = uext_ref[ts:ts + POOL_HALO, :]
    yield

    pc = ts // SUBLANES
    n_lt = xext_ref.shape[0]
    for lt in range(n_lt):
        for col in range(SUBLANES):
            xext_ref[lt, pl.ds(CONV_PAD * CONV_PITCH + col + 1, pc, stride=CONV_PITCH), :] = (
                xbc[col * pc:(col + 1) * pc, lt * LANES:(lt + 1) * LANES])
    yield
    sub_i = lax.broadcasted_iota(jnp.int32, (SUBLANES, 1), 0)
    for lt in range(n_lt):
        lanes = slice(lt * LANES, (lt + 1) * LANES)
        cw = cw_ref[:, lanes]
        for col in range(SUBLANES):
            acc = cb_ref[:, lanes] + cw[SSD_CONV - 1:SSD_CONV] * xbc[col * pc:(col + 1) * pc, lanes]
            for k in range(1, SSD_CONV):
                main = xext_ref[lt, pl.ds((CONV_PAD - k) * CONV_PITCH + col + 1, pc, stride=CONV_PITCH), :]
                prev = xext_ref[lt, pl.ds((CONV_PAD + pc - k) * CONV_PITCH + col, SUBLANES,
                                          stride=CONV_PITCH), :]
                tap = jnp.concatenate([jnp.where(sub_i < k, prev, main[0:SUBLANES]), main[SUBLANES:]], axis=0)
                acc = acc + cw[SSD_CONV - 1 - k:SSD_CONV - k] * tap
            xa_ref[col * pc:(col + 1) * pc, lanes] = _silu(acc)
    for lt in range(n_lt):
        xext_ref[lt, pl.ds(CONV_PAD * CONV_PITCH, pc, stride=CONV_PITCH), :] = (
            xext_ref[lt, pl.ds(CONV_PAD * CONV_PITCH + SUBLANES, pc, stride=CONV_PITCH), :])
    yield

    a_neg = -jnp.exp(alog_ref[...])
    row_i = lax.broadcasted_iota(jnp.int32, (L, L), 0)
    col_i = lax.broadcasted_iota(jnp.int32, (L, L), 1)
    causal = row_i >= col_i
    tril = causal.astype(F32)
    hpt = LANES // P
    lane_h = lax.broadcasted_iota(jnp.int32, (1, LANES), 1)
    y_chunks = []
    for c in range(ts // L):
        r0 = c * L
        xs_c = xa_ref[r0:r0 + L,0:ssd_w]
        b_c = xa_ref[r0:r0 + L,ssd_w:ssd_w + SSD_GROUPS * N]
        c_c = xa_ref[r0:r0 + L,ssd_w + SSD_GROUPS * N:conv_dim]
        dt = _softplus(dtr[r0:r0 + L] + dtb_ref[...])
        a_cs = _dot_exact(tril, dt * a_neg)
        a_hi = a_cs.astype(BF16)
        a_lo = (a_cs - a_hi.astype(F32)).astype(BF16)
        a_q = a_hi.astype(F32) + a_lo.astype(F32)
        a_rep = _dot(jnp.concatenate([a_hi, a_lo], axis=1), rep_ref[...])
        a_q_t = a_q.T
        dt_t = dt.T
        cbs, cgs, bg_ts = [], [], []
        for g in range(SSD_GROUPS):
            bg = b_c[:, g * N:(g + 1) * N]
            cg = c_c[:, g * N:(g + 1) * N]
            cbs.append(lax.dot_general(cg.astype(BF16), bg.astype(BF16), (((1,), (1,)), ((), ())),
                                       preferred_element_type=F32))
            cgs.append(cg)
            bg_ts.append(bg.T)
        y_tiles = []
        for q in range(SSD_HEADS // hpt):
            xs_q = xs_c[:, q * LANES:(q + 1) * LANES].astype(BF16)
            st = state_ref[q]
            rhs = jnp.concatenate([xs_q, st.astype(BF16)], axis=0)
            y_q = upd_q = keep_q = None
            for hh in range(hpt):
                hd = q * hpt + hh
                g = hd // SSD_HEADS_PER_GROUP
                a_col = a_rep[:, hd * LANES:(hd + 1) * LANES]
                a_row = a_q_t[hd:hd + 1, :]
                dt_row = dt_t[hd:hd + 1, :]
                decay = jnp.exp(jnp.where(causal, a_col - a_row, NEG_BIG))
                m = cbs[g] * decay * dt_row
                cs = cgs[g] * jnp.exp(a_col)
                lhs = jnp.concatenate([m, cs], axis=1).astype(BF16)
                y_h = _dot(lhs, rhs)
                a_end = a_row[:, L - 1:L]
                w_row = dt_row * jnp.exp(a_end - a_row)
                upd_h = _dot((bg_ts[g] * w_row).astype(BF16), xs_q)
                keep_h = jnp.exp(a_end)
                if hh == 0:
                    y_q, upd_q, keep_q = y_h, upd_h, keep_h
                else:
                    mine = (lane_h >= hh * P) & (lane_h < (hh + 1) * P)
                    y_q = jnp.where(mine, y_h, y_q)
                    upd_q = jnp.where(mine, upd_h, upd_q)
                    keep_q = jnp.where(mine, keep_h, keep_q)
            y_tiles.append(y_q)
            state_ref[q] = keep_q * st + upd_q
        y_chunks.append(jnp.concatenate(y_tiles, axis=1))
        yield
    y = jnp.concatenate(y_chunks, axis=0) if len(y_chunks) > 1 else y_chunks[0]
    y = y + xa_ref[:, 0:ssd_w] * dsk_ref[...]
    y = y * _silu(z)

    gw = ssd_w // SSD_GROUPS
    lane_y = lax.broadcasted_iota(jnp.int32, (1, ssd_w), 1)
    y2 = y * y
    scale = jnp.zeros_like(y)
    for g in range(SSD_GROUPS):
        in_g = (lane_y >= g * gw) & (lane_y < (g + 1) * gw)
        ms = jnp.sum(jnp.where(in_g, y2, 0.0), axis=-1, keepdims=True) * (1.0 / gw)
        scale = scale + jnp.where(in_g, lax.rsqrt(ms + NORM_EPS), 0.0)
    mix_in = jnp.concatenate([y_pool, y * scale * nw_ref[...]], axis=1).astype(BF16)
    yield
    mix = _dot(mix_in, wout_ref[...])
    yield
    x1 = x + g1 * mix
    x1_ref[...] = x1

    h2 = _rms(x1) * (1.0 + sc2) + sh2
    _slab_store(h2_ref, h2, d // LANES)
    h_hi = h2.astype(BF16)
    h_lo = (h2 - h_hi.astype(F32)).astype(BF16)
    yield
    logits = (_dot(h_hi, wrh_ref[...]) + _dot(h_lo, wrh_ref[...]) + _dot(h_hi, wrl_ref[...])
              + br_ref[...])
    lane = lax.broadcasted_iota(jnp.int32, (1, LANES), 1).astype(F32)
    big = float(LANES)
    gl = jnp.where(lane < N_EXPERT_GROUPS, logits, NEG_BIG)
    gmax = jnp.max(gl, axis=-1, keepdims=True)
    gsum = jnp.sum(jnp.exp(gl - gmax), axis=-1, keepdims=True)
    p_g = 1.0 / gsum
    g_idx = jnp.min(jnp.where(gl == gmax, lane, big), axis=-1, keepdims=True)
    lo = N_EXPERT_GROUPS + EXPERTS_PER_GROUP * g_idx
    el = jnp.where((lane >= lo) & (lane < lo + EXPERTS_PER_GROUP), logits, NEG_BIG)
    v1 = jnp.max(el, axis=-1, keepdims=True)
    i1 = jnp.min(jnp.where(el == v1, lane, big), axis=-1, keepdims=True)
    el2 = jnp.where(lane == i1, NEG_BIG, el)
    v2 = jnp.max(el2, axis=-1, keepdims=True)
    i2 = jnp.min(jnp.where(el2 == v2, lane, big), axis=-1, keepdims=True)
    e21 = jnp.exp(v2 - v1)
    gate1 = p_g / (1.0 + e21)
    gate2 = p_g * e21 / (1.0 + e21)
    route = jnp.where(lane == 0, i1 - N_EXPERT_GROUPS,
                      jnp.where(lane == 1, i2 - N_EXPERT_GROUPS,
                                jnp.where(lane == 2, gate1, jnp.where(lane == 3, gate2, 0.0))))
    route_ref[...] = route
    chosen = (lane == i1 - N_EXPERT_GROUPS) | (lane == i2 - N_EXPERT_GROUPS)
    cnt_ref[...] += jnp.sum(jnp.where(chosen, 1.0, 0.0), axis=0, keepdims=True)


def _mixer(x, mod, win, wp, ps, cw, cb, dtb, alog, dsk, nw, wout, wrh, wrl, br, *, pool_w, ssd_w):
    k_idx = jnp.arange(2 * LANES, dtype=jnp.int32)[:, None] % LANES
    h_idx = jnp.arange(SSD_HEADS * LANES, dtype=jnp.int32)[None, :] // LANES
    rep = (k_idx == h_idx).astype(BF16)
    b, s, d = x.shape
    ts = SEQ_TILE
    conv_dim = cw.shape[1]
    nrow = d // LANES

    def full(a):
        nd = a.ndim
        return pl.BlockSpec(a.shape, lambda i, j, _nd=nd: (0,) * _nd)

    nb = MIXER_SEQS
    seq_tile = lambda i, j: (i, j, 0)
    x1, h2, route, counts = pl.pallas_call(
        functools.partial(_mixer_kernel, pool_w=pool_w, ssd_w=ssd_w),
        grid=(b // nb, s // ts),
        in_specs=[pl.BlockSpec((nb, ts, d), seq_tile),
                  pl.BlockSpec((nb, 6, d), lambda i, j: (i, 0, 0)),
                  full(win), full(wp), full(ps), full(cw), full(cb), full(dtb), full(alog),
                  full(dsk), full(nw), full(wout), full(wrh), full(wrl), full(br), full(rep)],
        out_specs=[pl.BlockSpec((nb, ts, d), seq_tile),
                   pl.BlockSpec((nb, ts * nrow, LANES), seq_tile),
                   pl.BlockSpec((nb, ts, LANES), seq_tile),
                   pl.BlockSpec((1, LANES), lambda i, j: (0, 0))],
        out_shape=[jax.ShapeDtypeStruct((b, s, d), F32),
                   jax.ShapeDtypeStruct((b, s * nrow, LANES), F32),
                   jax.ShapeDtypeStruct((b, s, LANES), F32),
                   jax.ShapeDtypeStruct((1, LANES), F32)],
        scratch_shapes=[pltpu.VMEM((nb, ssd_w // LANES, SSD_STATE, LANES), F32),
                        pltpu.VMEM((nb, conv_dim // LANES, (ts // SUBLANES + 2 * CONV_PAD) * CONV_PITCH, LANES),
                                   F32),
                        pltpu.VMEM((nb, POOL_HALO + ts, pool_w), F32),
                        pltpu.VMEM((nb, ts, conv_dim), F32)],
        compiler_params=pltpu.CompilerParams(dimension_semantics=("arbitrary", "arbitrary"),
                                             vmem_limit_bytes=VMEM_LIMIT_BYTES),
        name="mixer",
    )(x, mod, win, wp, ps, cw, cb, dtb, alog, dsk, nw, wout, wrh, wrl, br, rep)
    return x1, h2.reshape(b * s * nrow, LANES), route.reshape(b * s, LANES), counts


def _rank_kernel(route_ref, start_ref, dest_ref, carry_ref):
    ta = route_ref.shape[0]

    @pl.when(pl.program_id(0) == 0)
    def _():
        carry_ref[...] = start_ref[...]

    route_t = route_ref[...].T
    e_idx = lax.broadcasted_iota(jnp.int32, (LANES, ta), 0).astype(F32)
    ohs = [jnp.where(e_idx == route_t[k:k + 1, :], 1.0, 0.0) for k in range(TOP_K)]
    oh = (ohs[0] + ohs[1]).astype(BF16)
    earlier = (lax.broadcasted_iota(jnp.int32, (ta, ta), 0)
               < lax.broadcasted_iota(jnp.int32, (ta, ta), 1))
    before = _dot(oh, jnp.where(earlier, 1.0, 0.0).astype(BF16))
    carry = carry_ref[...]
    base = jnp.concatenate([carry] * (ta // LANES), axis=1) + before
    for k in range(TOP_K):
        dest_ref[0, k:k + 1, :] = jnp.sum(ohs[k] * base, axis=0, keepdims=True).astype(jnp.int32)
    carry_ref[...] = carry + _dot(oh, jnp.ones((ta, LANES), BF16))


def _rank(route, start_rep):
    t = route.shape[0]
    return pl.pallas_call(
        _rank_kernel,
        grid=(t // ROW_COPY_TILE,),
        in_specs=[pl.BlockSpec((ROW_COPY_TILE, LANES), lambda i: (i, 0)),
                  pl.BlockSpec((LANES, LANES), lambda i: (0, 0))],
        out_specs=pl.BlockSpec((1, TOP_K, ROW_COPY_TILE), lambda i: (i, 0, 0)),
        out_shape=jax.ShapeDtypeStruct((t // ROW_COPY_TILE, TOP_K, ROW_COPY_TILE), jnp.int32),
        scratch_shapes=[pltpu.VMEM((LANES, LANES), F32)],
        compiler_params=pltpu.CompilerParams(dimension_semantics=("arbitrary",)),
        name="rank",
    )(route, start_rep)


def _dispatch_kernel(dest_ref, h2_ref, xs_hbm, sem, *, nrow):
    tile_rows = h2_ref.shape[0]

    part_tokens = dest_ref.shape[2]
    for part in range(dest_ref.shape[0]):
        def issue(blk, carry, part=part):
            for u in range(DMA_UNROLL):
                tk = blk * DMA_UNROLL + u
                for k in range(TOP_K):
                    pltpu.make_async_copy(_slab_of_row(h2_ref, part * part_tokens + tk, nrow),
                                          _slab_of_row(xs_hbm, dest_ref[part, k, tk], nrow),
                                          sem).start(priority=(u * TOP_K + k) % DMA_THREADS)
            return carry

        lax.fori_loop(0, part_tokens // DMA_UNROLL, issue, 0)

    for _ in range(TOP_K):
        pltpu.make_async_copy(h2_ref, xs_hbm.at[pl.ds(0, tile_rows)], sem).wait()


def _dispatch(dest, h2, nrow):
    t = h2.shape[0] // nrow
    return pl.pallas_call(
        functools.partial(_dispatch_kernel, nrow=nrow),
        grid=(t // (DISPATCH_PARTS * ROW_COPY_TILE),),
        in_specs=[pl.BlockSpec((DISPATCH_PARTS, TOP_K, ROW_COPY_TILE), lambda i: (i, 0, 0),
                               memory_space=pltpu.SMEM),
                  pl.BlockSpec((DISPATCH_PARTS * ROW_COPY_TILE * nrow, LANES), lambda i: (i, 0))],
        out_specs=pl.BlockSpec(memory_space=pl.ANY),
        out_shape=jax.ShapeDtypeStruct((t * TOP_K * nrow, LANES), F32),
        scratch_shapes=[pltpu.SemaphoreType.DMA(())],
        compiler_params=pltpu.CompilerParams(dimension_semantics=("arbitrary",),
                                             has_side_effects=True),
        name="dispatch",
    )(dest, h2)


def _expert_kernel(blk_ref, exp_ref, lo_ref, hi_ref, first_ref, new_ref, xs_ref, w13_ref, w2_ref, ys_ref,
                   w13_bf, w2_bf, *, nrow):
    i = pl.program_id(0)
    rows = xs_ref.shape[0] // nrow
    lo = lo_ref[i]
    hi = hi_ref[i]

    sub = rows // EXPERT_CHAINS

    @pl.when(new_ref[i] == 1)
    def _():
        w13_bf[...] = w13_ref[0].astype(BF16)
        w2_bf[...] = w2_ref[0].astype(BF16)

    def chain(r0):
        xb = jnp.concatenate([xs_ref[pl.ds(r0 * nrow + c, sub, stride=nrow), :].astype(BF16)
                              for c in range(nrow)], axis=1)
        yield
        hu = _dot(xb, w13_bf[...])
        yield
        f = hu.shape[1] // 2
        act = (_silu(hu[:, :f]) * hu[:, f:]).astype(BF16)
        yield
        y = _dot(act, w2_bf[...])
        yield
        ridx = r0 + lax.broadcasted_iota(jnp.int32, (sub, 1), 0)
        take = (ridx >= lo_eff) & (ridx < hi_eff)
        for c in range(nrow):
            sl = pl.ds(r0 * nrow + c, sub, stride=nrow)
            ys_ref[sl, :] = jnp.where(take, y[:, c * LANES:(c + 1) * LANES], ys_ref[sl, :])

    is_first = first_ref[i] == 1
    lo_eff = jnp.where(is_first, 0, lo)
    hi_eff = jnp.where(is_first, rows, hi)

    @pl.when(hi > lo)
    def _():
        _run_skewed([chain(k * sub) for k in range(EXPERT_CHAINS)])


def _experts(item_blk, item_exp, item_lo, item_hi, item_first, item_new, xs, w13, w2, nrow):
    a = xs.shape[0] // nrow
    n_items = item_blk.shape[0]
    d = w13.shape[1]
    ff2 = w13.shape[2]
    grid_spec = pltpu.PrefetchScalarGridSpec(
        num_scalar_prefetch=6,
        grid=(n_items,),
        in_specs=[pl.BlockSpec((EXPERT_BLOCK * nrow, LANES), lambda i, b, e, lo, hi, fr, nw: (b[i], 0)),
                  pl.BlockSpec((1, d, ff2), lambda i, b, e, lo, hi, fr, nw: (e[i], 0, 0)),
                  pl.BlockSpec((1, ff2 // 2, d), lambda i, b, e, lo, hi, fr, nw: (e[i], 0, 0))],
        out_specs=pl.BlockSpec((EXPERT_BLOCK * nrow, LANES), lambda i, b, e, lo, hi, fr, nw: (b[i], 0)),
        scratch_shapes=[pltpu.VMEM((d, ff2), BF16), pltpu.VMEM((ff2 // 2, d), BF16)],
    )
    return pl.pallas_call(
        functools.partial(_expert_kernel, nrow=nrow),
        grid_spec=grid_spec,
        out_shape=jax.ShapeDtypeStruct((a * nrow, LANES), F32),
        compiler_params=pltpu.CompilerParams(dimension_semantics=("arbitrary",),
                                             vmem_limit_bytes=VMEM_LIMIT_BYTES),
        name="experts",
    )(item_blk, item_exp, item_lo, item_hi, item_first, item_new, xs, w13, w2)


def _final_kernel(dest_ref, dnext_ref, x1_ref, route_ref, mod_ref, fw_ref, ys_hbm, o_ref, bufs, sems):
    tf = x1_ref.shape[1]
    nrow = bufs.shape[2] // tf
    step = pl.program_id(0)
    slot = lax.rem(step, 2)

    def start_gathers(d_ref, to_slot):
        def issue(blk, carry):
            for u in range(DMA_UNROLL):
                tk = blk * DMA_UNROLL + u
                for k in range(TOP_K):
                    pltpu.make_async_copy(_slab_of_row(ys_hbm, d_ref[0, k, tk], nrow),
                                          _slab_of_row(bufs.at[to_slot, k], tk, nrow),
                                          sems.at[to_slot]).start(priority=(u * TOP_K + k) % DMA_THREADS)
            return carry

        lax.fori_loop(0, tf // DMA_UNROLL, issue, 0)

    @pl.when(step == 0)
    def _():
        start_gathers(dest_ref, 0)

    @pl.when(step + 1 < pl.num_programs(0))
    def _():
        start_gathers(dnext_ref, 1 - slot)

    for k in range(TOP_K):
        pltpu.make_async_copy(ys_hbm.at[pl.ds(0, tf * nrow)], bufs.at[slot, k], sems.at[slot]).wait()

    r = route_ref[...]
    gate1 = r[:, 2:3]
    gate2 = r[:, 3:4]
    y = jnp.concatenate([c1 * gate1 + c2 * gate2 for c1, c2 in
                         zip(_slab_chunks(bufs.at[slot, 0], tf, nrow),
                             _slab_chunks(bufs.at[slot, 1], tf, nrow))], axis=1)
    g2 = mod_ref[0][5:6]
    o_ref[0] = _rms(x1_ref[0] + g2 * y) * fw_ref[...]


def _final(dest, x1, route, mod, fw, ys):
    b, s, d = x1.shape
    tf = ROW_COPY_TILE
    nrow = d // LANES
    spt = s // tf
    n_steps = b * spt
    return pl.pallas_call(
        _final_kernel,
        grid=(n_steps,),
        in_specs=[pl.BlockSpec((1, TOP_K, tf), lambda i: (i, 0, 0), memory_space=pltpu.SMEM),
                  pl.BlockSpec((1, TOP_K, tf), lambda i: (jnp.minimum(i + 1, n_steps - 1), 0, 0),
                               memory_space=pltpu.SMEM),
                  pl.BlockSpec((1, tf, d), lambda i: (i // spt, i % spt, 0)),
                  pl.BlockSpec((tf, LANES), lambda i: (i, 0)),
                  pl.BlockSpec((1, 6, d), lambda i: (i // spt, 0, 0)),
                  pl.BlockSpec((1, d), lambda i: (0, 0)),
                  pl.BlockSpec(memory_space=pl.ANY)],
        out_specs=pl.BlockSpec((1, tf, d), lambda i: (i // spt, i % spt, 0)),
        out_shape=jax.ShapeDtypeStruct((b, s, d), F32),
        scratch_shapes=[pltpu.VMEM((2, TOP_K, tf * nrow, LANES), F32),
                        pltpu.SemaphoreType.DMA((2,))],
        compiler_params=pltpu.CompilerParams(dimension_semantics=("arbitrary",),
                                             vmem_limit_bytes=VMEM_LIMIT_BYTES),
        name="final",
    )(dest, dest, x1, route, mod, fw, ys)


def _pad_lanes(v, fill=0.0):
    n = v.shape[-1]
    return jnp.pad(v, [(0, 0)] * (v.ndim - 1) + [(0, LANES - n)], constant_values=fill)


def _work_items(counts, n_blocks):
    n_items = n_blocks + N_EXPERTS - 1
    ends = jnp.cumsum(counts)
    starts = ends - counts
    first_blk = starts // EXPERT_BLOCK
    last_blk = jnp.maximum(ends - 1, starts) // EXPERT_BLOCK
    n_e = jnp.where(counts > 0, last_blk - first_blk + 1, 0)
    item_end = jnp.cumsum(n_e)
    item_start = item_end - n_e
    ids = jnp.arange(n_items, dtype=jnp.int32)
    total = item_end[-1]
    ids_c = jnp.minimum(ids, total - 1)
    e = jnp.sum((item_end[None, :] <= ids_c[:, None]).astype(jnp.int32), axis=1)
    onehot = (e[:, None] == jnp.arange(N_EXPERTS, dtype=jnp.int32)[None, :]).astype(jnp.int32)
    pick = lambda v: jnp.sum(onehot * v[None, :], axis=1)
    blk = pick(first_blk) + ids_c - pick(item_start)
    lo = jnp.clip(pick(starts) - blk * EXPERT_BLOCK, 0, EXPERT_BLOCK)
    hi = jnp.clip(pick(ends) - blk * EXPERT_BLOCK, 0, EXPERT_BLOCK)
    hi = jnp.where(ids < total, hi, lo)
    prev_blk = jnp.concatenate([jnp.full((1,), -1, jnp.int32), blk[:-1]])
    first = (blk != prev_blk).astype(jnp.int32)
    prev_e = jnp.concatenate([jnp.full((1,), -1, jnp.int32), e[:-1]])
    new_expert = (e != prev_e).astype(jnp.int32)
    return blk, e, lo, hi, first, new_expert


def kernel(x, c, w_ada, b_ada, w_in, w_pool, pool_scale, conv_w, conv_b, dt_bias, a_log, d_skip,
           ssd_norm_w, w_out, w_group, b_group, w_router, b_router, w13, w2, final_norm_w):
    b, s, d = x.shape
    depth = w_ada.shape[0]
    t = b * s
    pool_w = w_pool.shape[1] * w_pool.shape[2]
    ssd_w = SSD_HEADS * SSD_HEAD_DIM
    conv_dim = conv_w.shape[2]
    assert s % SEQ_TILE == 0 and SEQ_TILE % SSD_CHUNK == 0 and b % MIXER_SEQS == 0
    assert s % ROW_COPY_TILE == 0 and ROW_COPY_TILE % LANES == 0
    assert (t * TOP_K) % EXPERT_BLOCK == 0 and d % LANES == 0
    assert pool_w == 2 * LANES and len(POOL_WINDOWS) * POOL_GROUP_DIM == pool_w
    assert N_EXPERT_GROUPS + N_EXPERTS <= LANES and SSD_HEADS <= LANES
    assert depth == 1, "the final RMSNorm is fused into the last layer's combine step"

    for layer in range(depth):
        mod = _ada(c, w_ada[layer], b_ada[layer]).reshape(b, 6, d)

        o_dt = pool_w + ssd_w + conv_dim
        win = jnp.concatenate([w_in[layer][:, :o_dt], _pad_lanes(w_in[layer][:, o_dt:])], axis=1).astype(BF16)
        wp = jnp.zeros((pool_w, pool_w), F32)
        for g in range(len(POOL_WINDOWS)):
            sl = slice(g * POOL_GROUP_DIM, (g + 1) * POOL_GROUP_DIM)
            wp = wp.at[sl, sl].set(w_pool[layer, g])
        wp = wp.astype(BF16)
        w_route = _pad_lanes(jnp.concatenate([w_group[layer], w_router[layer]], axis=1))
        wrh = w_route.astype(BF16)
        wrl = (w_route - wrh.astype(F32)).astype(BF16)
        br = _pad_lanes(jnp.concatenate([b_group[layer], b_router[layer]])[None, :])

        x1, h2, route, counts_f = _mixer(
            x, mod, win, wp, pool_scale[layer][None, :], conv_w[layer], conv_b[layer][None, :],
            _pad_lanes(dt_bias[layer][None, :]), _pad_lanes(a_log[layer][None, :], fill=NEG_BIG),
            jnp.repeat(d_skip[layer], SSD_HEAD_DIM)[None, :], ssd_norm_w[layer][None, :],
            w_out[layer].astype(BF16), wrh, wrl, br, pool_w=pool_w, ssd_w=ssd_w)

        start_f = jnp.cumsum(counts_f, axis=1) - counts_f
        start_rep = jnp.broadcast_to(start_f.reshape(LANES, 1), (LANES, LANES))
        dest = _rank(route, start_rep)
        counts = counts_f[0, :N_EXPERTS].astype(jnp.int32)
        items = _work_items(counts, (t * TOP_K) // EXPERT_BLOCK)

        nrow = d // LANES
        xs = _dispatch(dest, h2, nrow)
        ys = _experts(*items, xs, w13[layer], w2[layer], nrow)
        x = _final(dest, x1, route, mod, final_norm_w[None, :], ys)
    return x
```

```python
import functools

import jax
import jax.numpy as jnp
from jax import lax
from jax.experimental import pallas as pl
from jax.experimental.pallas import tpu as pltpu

POOL_WINDOWS = (2, 4, 8, 16)
POOL_GROUP_DIM = 64
SSD_HEAD_DIM = 64
SSD_GROUPS = 4
SSD_HEADS_PER_GROUP = 3
SSD_HEADS = SSD_GROUPS * SSD_HEADS_PER_GROUP
SSD_STATE = 128
SSD_CONV = 4
SSD_CHUNK = 128
N_EXPERT_GROUPS = 4
EXPERTS_PER_GROUP = 8
N_EXPERTS = N_EXPERT_GROUPS * EXPERTS_PER_GROUP
TOP_K = 2
NORM_EPS = 1e-6

LANES = 128
SUBLANES = 8
VMEM_LIMIT_BYTES = 56 * 1024 * 1024

SEQ_TILE = 256
MIXER_SEQS = 2
PROJ_PIECE = 1024
NORM_ROWS = 32
MIXER_ORDER = (0, 1, 0, 0, 0,
               1, 0, 0, 0,
               1, 0,
               1, 0,
               0, 1, 1, 1,
               0, 1,
               0, 0, 1,
               1, 1, 1, 1)
CONV_PAD = SUBLANES
CONV_PITCH = SUBLANES + 1
POOL_HALO = 16
ROW_COPY_TILE = 512
EXPERT_BLOCK = 512
EXPERT_CHAINS = 2
DISPATCH_PARTS = 2
DMA_UNROLL = 8
DMA_THREADS = 2

NEG_BIG = -1e30
F32 = jnp.float32
BF16 = jnp.bfloat16


def _silu(v):
    half = 0.5 * v
    return half + half * jnp.tanh(half)


def _softplus(v):
    return jnp.maximum(v, 0.0) + jnp.log(1.0 + jnp.exp(-jnp.abs(v)))


def _rms(v):
    return v * lax.rsqrt(jnp.mean(v * v, axis=-1, keepdims=True) + NORM_EPS)


def _dot(a, b):
    return jnp.dot(a, b, preferred_element_type=F32)


def _dot_exact(a, b):
    return jnp.dot(a, b, preferred_element_type=F32, precision=lax.Precision.HIGHEST)


def _run_skewed(chains):
    live = [False] * len(chains)
    started = 0
    while started < len(chains) or any(live):
        if started < len(chains):
            live[started] = True
            started += 1
        for k, chain in enumerate(chains):
            if live[k]:
                try:
                    next(chain)
                except StopIteration:
                    live[k] = False


def _run_ordered(chains, order):
    for k in order:
        next(chains[k], None)
    for chain in chains:
        assert next(chain, "done") == "done", "order does not cover every phase"


def _slab_chunks(ref, rows, nrow):
    return [ref[pl.ds(c, rows, stride=nrow), :] for c in range(nrow)]


def _slab_store(ref, val, nrow):
    rows = val.shape[0]
    for c in range(nrow):
        ref[pl.ds(c, rows, stride=nrow), :] = val[:, c * LANES:(c + 1) * LANES]


def _slab_of_row(ref, row, nrow):
    return ref.at[pl.ds(pl.multiple_of(row * nrow, nrow), nrow)]


def _ada_kernel(c_ref, w_ref, b_ref, o_ref):
    o_ref[...] = _dot_exact(_silu(c_ref[...]), w_ref[...]) + b_ref[...]


def _ada(c, w_ada, b_ada):
    b, d = c.shape
    n = w_ada.shape[1]
    return pl.pallas_call(
        _ada_kernel,
        grid=(n // d,),
        in_specs=[pl.BlockSpec((b, d), lambda i: (0, 0)),
                  pl.BlockSpec((d, d), lambda i: (0, i)),
                  pl.BlockSpec((1, d), lambda i: (0, i))],
        out_specs=pl.BlockSpec((b, d), lambda i: (0, i)),
        out_shape=jax.ShapeDtypeStruct((b, n), F32),
        compiler_params=pltpu.CompilerParams(dimension_semantics=("arbitrary",),
                                             vmem_limit_bytes=VMEM_LIMIT_BYTES),
        name="ada",
    )(c, w_ada, b_ada.reshape(1, n))


def _mixer_kernel(x_ref, mod_ref, win_ref, wp_ref, ps_ref, cw_ref, cb_ref, dtb_ref, alog_ref,
                  dsk_ref, nw_ref, wout_ref, wrh_ref, wrl_ref, br_ref, rep_ref, tril3_ref,
                  x1_ref, h2_ref, route_ref, cnt_ref,
                  state_ref, xext_ref, uext_ref, xa_ref, hb_ref, *, pool_w, ssd_w):
    @pl.when((pl.program_id(0) == 0) & (pl.program_id(1) == 0))
    def _():
        cnt_ref[...] = jnp.zeros_like(cnt_ref)
        xext_ref[...] = jnp.zeros_like(xext_ref)

    @pl.when(pl.program_id(1) == 0)
    def _():
        state_ref[...] = jnp.zeros_like(state_ref)
        pc = x_ref.shape[1] // SUBLANES
        for bi in range(xext_ref.shape[0]):
            for lt in range(xext_ref.shape[1]):
                xext_ref[bi, lt, pl.ds(CONV_PAD * CONV_PITCH, pc, stride=CONV_PITCH), :] = (
                    jnp.zeros((pc, LANES), F32))
        uext_ref[:, 0:POOL_HALO, :] = jnp.zeros((uext_ref.shape[0], POOL_HALO, uext_ref.shape[2]), F32)

    _run_ordered([_mixer_tile(x_ref.at[bi], mod_ref.at[bi], win_ref, wp_ref, ps_ref, cw_ref, cb_ref, dtb_ref,
                             alog_ref, dsk_ref, nw_ref, wout_ref, wrh_ref, wrl_ref, br_ref, rep_ref, tril3_ref,
                             x1_ref.at[bi], h2_ref.at[bi], route_ref.at[bi], cnt_ref,
                             state_ref.at[bi], xext_ref.at[bi], uext_ref.at[bi], xa_ref.at[bi], hb_ref.at[bi], pool_w=pool_w, ssd_w=ssd_w)
                  for bi in range(x_ref.shape[0])], MIXER_ORDER)


def _mixer_tile(x_ref, mod_ref, win_ref, wp_ref, ps_ref, cw_ref, cb_ref, dtb_ref, alog_ref,
                dsk_ref, nw_ref, wout_ref, wrh_ref, wrl_ref, br_ref, rep_ref, tril3_ref,
                x1_ref, h2_ref, route_ref, cnt_ref,
                state_ref, xext_ref, uext_ref, xa_ref, hb_ref, *, pool_w, ssd_w):
    ts = x_ref.shape[0]
    d = x_ref.shape[1]
    L = SSD_CHUNK
    N = SSD_STATE
    P = SSD_HEAD_DIM
    j = pl.program_id(1)

    mod = mod_ref[...]
    sh1, sc1, g1 = mod[0:1], mod[1:2], mod[2:3]
    sh2, sc2 = mod[3:4], mod[4:5]

    for r in range(0, ts, NORM_ROWS):
        hb_ref[r:r + NORM_ROWS, :] = (_rms(x_ref[r:r + NORM_ROWS, :]) * (1.0 + sc1) + sh1).astype(BF16)
    h = hb_ref[...]
    yield
    n_proj = win_ref.shape[1]
    edges = [min(n_proj, e * PROJ_PIECE) for e in range(pl.cdiv(n_proj, PROJ_PIECE) + 1)]
    pieces = []
    for a, b in zip(edges[:-1], edges[1:]):
        pieces.append(_dot(h, win_ref[:, a:b]))
        yield
    proj = jnp.concatenate(pieces, axis=1)
    o_z = pool_w
    o_x = pool_w + ssd_w
    conv_dim = ssd_w + 2 * SSD_GROUPS * N
    o_dt = o_x + conv_dim
    u = proj[:, 0:pool_w]
    z = proj[:, o_z:o_x]
    xbc = proj[:, o_x:o_dt]
    dtr = proj[:, o_dt:o_dt + LANES]

    uext_ref[POOL_HALO:POOL_HALO + ts, :] = u
    lane_u = lax.broadcasted_iota(jnp.int32, (1, LANES), 1)
    upper = lane_u >= POOL_GROUP_DIM

    def ush(k, col):
        return uext_ref[POOL_HALO - k:POOL_HALO - k + ts, col * LANES:(col + 1) * LANES]

    w0, w1, w2, w3 = POOL_WINDOWS
    lo_a = ush(0, 0)
    for k in range(1, w0):
        lo_a = lo_a + ush(k, 0)
    lo_b = ush(w0, 0)
    for k in range(w0 + 1, w1):
        lo_b = lo_b + ush(k, 0)
    hi_a = ush(0, 1)
    for k in range(1, w2):
        hi_a = hi_a + ush(k, 1)
    hi_b = ush(w2, 1)
    for k in range(w2 + 1, w3):
        hi_b = hi_b + ush(k, 1)
    win_lo = lo_a + jnp.where(upper, lo_b, 0.0)
    win_hi = hi_a + jnp.where(upper, hi_b, 0.0)
    pos = (j * ts + 1 + lax.broadcasted_iota(jnp.int32, (ts, 1), 0)).astype(F32)
    cnt_lo = jnp.minimum(pos, jnp.where(upper, float(w1), float(w0)))
    cnt_hi = jnp.minimum(pos, jnp.where(upper, float(w3), float(w2)))
    pdiff = jnp.concatenate([win_lo / cnt_lo, win_hi / cnt_hi], axis=1) - u
    y_pool = _dot(pdiff.astype(BF16), wp_ref[...]) * ps_ref[...]
    uext_ref[0:POOL_HALO, :] = uext_ref[ts:ts + POOL_HALO, :]

    a_neg = -jnp.exp(alog_ref[...])
    dt_all = _softplus(dtr + dtb_ref[...])
    da = dt_all * a_neg
    d1 = da.astype(BF16)
    r1 = da - d1.astype(F32)
    d2 = r1.astype(BF16)
    d3 = (r1 - d2.astype(F32)).astype(BF16)
    a_cs = _dot(tril3_ref[...], jnp.concatenate([d1, d2, d3], axis=0))
    a_hi = a_cs.astype(BF16)
    a_lo = (a_cs - a_hi.astype(F32)).astype(BF16)
    a_q_all = a_hi.astype(F32) + a_lo.astype(F32)
    a_rep_all = _dot(jnp.concatenate([a_hi, a_lo], axis=1), rep_ref[...])
    yield

    pc = ts // SUBLANES
    n_lt = xext_ref.shape[0]
    for lt in range(n_lt):
        for col in range(SUBLANES):
            xext_ref[lt, pl.ds(CONV_PAD * CONV_PITCH + col + 1, pc, stride=CONV_PITCH), :] = (
                xbc[col * pc:(col + 1) * pc, lt * LANES:(lt + 1) * LANES])
    yield
    sub_i = lax.broadcasted_iota(jnp.int32, (SUBLANES, 1), 0)
    for lt in range(n_lt):
        lanes = slice(lt * LANES, (lt + 1) * LANES)
        cw = cw_ref[:, lanes]
        for col in range(SUBLANES):
            acc = cb_ref[:, lanes] + cw[SSD_CONV - 1:SSD_CONV] * xbc[col * pc:(col + 1) * pc, lanes]
            for k in range(1, SSD_CONV):
                main = xext_ref[lt, pl.ds((CONV_PAD - k) * CONV_PITCH + col + 1, pc, stride=CONV_PITCH), :]
                prev = xext_ref[lt, pl.ds((CONV_PAD + pc - k) * CONV_PITCH + col, SUBLANES,
                                          stride=CONV_PITCH), :]
                tap = jnp.concatenate([jnp.where(sub_i < k, prev, main[0:SUBLANES]), main[SUBLANES:]], axis=0)
                acc = acc + cw[SSD_CONV - 1 - k:SSD_CONV - k] * tap
            xa_ref[col * pc:(col + 1) * pc, lanes] = _silu(acc)
    for lt in range(n_lt):
        xext_ref[lt, pl.ds(CONV_PAD * CONV_PITCH, pc, stride=CONV_PITCH), :] = (
            xext_ref[lt, pl.ds(CONV_PAD * CONV_PITCH + SUBLANES, pc, stride=CONV_PITCH), :])
    yield

    row_i = lax.broadcasted_iota(jnp.int32, (L, L), 0)
    col_i = lax.broadcasted_iota(jnp.int32, (L, L), 1)
    causal = row_i >= col_i
    hpt = LANES // P
    lane_h = lax.broadcasted_iota(jnp.int32, (1, LANES), 1)
    y_chunks = []
    for c in range(ts // L):
        r0 = c * L
        xs_c = xa_ref[r0:r0 + L,0:ssd_w]
        b_c = xa_ref[r0:r0 + L,ssd_w:ssd_w + SSD_GROUPS * N]
        c_c = xa_ref[r0:r0 + L,ssd_w + SSD_GROUPS * N:conv_dim]
        a_rep = a_rep_all[r0:r0 + L]
        a_q_t = a_q_all[r0:r0 + L].T
        dt_t = dt_all[r0:r0 + L].T
        cbs, cgs, bg_ts = [], [], []
        for g in range(SSD_GROUPS):
            bg = b_c[:, g * N:(g + 1) * N]
            cg = c_c[:, g * N:(g + 1) * N]
            cbs.append(lax.dot_general(cg.astype(BF16), bg.astype(BF16), (((1,), (1,)), ((), ())),
                                       preferred_element_type=F32))
            cgs.append(cg)
            bg_ts.append(bg.T)
        y_tiles = []
        for q in range(SSD_HEADS // hpt):
            xs_q = xs_c[:, q * LANES:(q + 1) * LANES].astype(BF16)
            st = state_ref[q]
            rhs = jnp.concatenate([xs_q, st.astype(BF16)], axis=0)
            y_q = upd_q = keep_q = None
            for hh in range(hpt):
                hd = q * hpt + hh
                g = hd // SSD_HEADS_PER_GROUP
                a_col = a_rep[:, hd * LANES:(hd + 1) * LANES]
                a_row = a_q_t[hd:hd + 1, :]
                dt_row = dt_t[hd:hd + 1, :]
                decay = jnp.exp(jnp.where(causal, a_col - a_row, NEG_BIG))
                m = cbs[g] * decay * dt_row
                cs = cgs[g] * jnp.exp(a_col)
                lhs = jnp.concatenate([m, cs], axis=1).astype(BF16)
                y_h = _dot(lhs, rhs)
                a_end = a_row[:, L - 1:L]
                w_row = dt_row * jnp.exp(a_end - a_row)
                upd_h = _dot((bg_ts[g] * w_row).astype(BF16), xs_q)
                keep_h = jnp.exp(a_end)
                if hh == 0:
                    y_q, upd_q, keep_q = y_h, upd_h, keep_h
                else:
                    mine = (lane_h >= hh * P) & (lane_h < (hh + 1) * P)
                    y_q = jnp.where(mine, y_h, y_q)
                    upd_q = jnp.where(mine, upd_h, upd_q)
                    keep_q = jnp.where(mine, keep_h, keep_q)
            y_tiles.append(y_q)
            state_ref[q] = keep_q * st + upd_q
        y_chunks.append(jnp.concatenate(y_tiles, axis=1))
        yield
    y_all = jnp.concatenate(y_chunks, axis=0) if len(y_chunks) > 1 else y_chunks[0]

    gw = ssd_w // SSD_GROUPS
    lane_y = lax.broadcasted_iota(jnp.int32, (1, ssd_w), 1)
    for r in range(0, ts, NORM_ROWS):
        rows = slice(r, r + NORM_ROWS)
        y = (y_all[rows] + xa_ref[rows, 0:ssd_w] * dsk_ref[...]) * _silu(z[rows])
        y2 = y * y
        scale = jnp.zeros_like(y)
        for g in range(SSD_GROUPS):
            in_g = (lane_y >= g * gw) & (lane_y < (g + 1) * gw)
            ms = jnp.sum(jnp.where(in_g, y2, 0.0), axis=-1, keepdims=True) * (1.0 / gw)
            scale = scale + jnp.where(in_g, lax.rsqrt(ms + NORM_EPS), 0.0)
        hb_ref[rows, :] = jnp.concatenate([y_pool[rows], y * scale * nw_ref[...]], axis=1).astype(BF16)
    yield
    mix = _dot(hb_ref[...], wout_ref[...])
    yield

    nrow = d // LANES
    for r in range(0, ts, NORM_ROWS):
        rows = slice(r, r + NORM_ROWS)
        x1 = x_ref[rows, :] + g1 * mix[rows]
        x1_ref[rows, :] = x1
        h2 = _rms(x1) * (1.0 + sc2) + sh2
        for c in range(nrow):
            h2_ref[pl.ds(r * nrow + c, NORM_ROWS, stride=nrow), :] = h2[:, c * LANES:(c + 1) * LANES]
        hb_ref[rows, :] = h2.astype(BF16)
    h_hi = hb_ref[...]
    yield
    logits = _dot(h_hi, wrh_ref[...]) + br_ref[...]
    lane = lax.broadcasted_iota(jnp.int32, (1, LANES), 1).astype(F32)
    big = float(LANES)
    gl = jnp.where(lane < N_EXPERT_GROUPS, logits, NEG_BIG)
    gmax = jnp.max(gl, axis=-1, keepdims=True)
    gsum = jnp.sum(jnp.exp(gl - gmax), axis=-1, keepdims=True)
    p_g = 1.0 / gsum
    g_idx = jnp.min(jnp.where(gl == gmax, lane, big), axis=-1, keepdims=True)
    lo = N_EXPERT_GROUPS + EXPERTS_PER_GROUP * g_idx
    el = jnp.where((lane >= lo) & (lane < lo + EXPERTS_PER_GROUP), logits, NEG_BIG)
    v1 = jnp.max(el, axis=-1, keepdims=True)
    i1 = jnp.min(jnp.where(el == v1, lane, big), axis=-1, keepdims=True)
    el2 = jnp.where(lane == i1, NEG_BIG, el)
    v2 = jnp.max(el2, axis=-1, keepdims=True)
    i2 = jnp.min(jnp.where(el2 == v2, lane, big), axis=-1, keepdims=True)
    e21 = jnp.exp(v2 - v1)
    gate1 = p_g / (1.0 + e21)
    gate2 = p_g * e21 / (1.0 + e21)
    route = jnp.where(lane == 0, i1 - N_EXPERT_GROUPS,
                      jnp.where(lane == 1, i2 - N_EXPERT_GROUPS,
                                jnp.where(lane == 2, gate1, jnp.where(lane == 3, gate2, 0.0))))
    route_ref[...] = route
    chosen = (lane == i1 - N_EXPERT_GROUPS) | (lane == i2 - N_EXPERT_GROUPS)
    cnt_ref[...] += jnp.sum(jnp.where(chosen, 1.0, 0.0), axis=0, keepdims=True)


def _mixer(x, mod, win, wp, ps, cw, cb, dtb, alog, dsk, nw, wout, wrh, wrl, br, *, pool_w, ssd_w):
    k_idx = jnp.arange(2 * LANES, dtype=jnp.int32)[:, None] % LANES
    h_idx = jnp.arange(SSD_HEADS * LANES, dtype=jnp.int32)[None, :] // LANES
    rep = (k_idx == h_idx).astype(BF16)
    r_idx = jnp.arange(SEQ_TILE, dtype=jnp.int32)[:, None]
    c_idx = jnp.arange(3 * SEQ_TILE, dtype=jnp.int32)[None, :] % SEQ_TILE
    tril3 = ((r_idx >= c_idx) & (r_idx // SSD_CHUNK == c_idx // SSD_CHUNK)).astype(BF16)
    b, s, d = x.shape
    ts = SEQ_TILE
    conv_dim = cw.shape[1]
    nrow = d // LANES

    def full(a):
        nd = a.ndim
        return pl.BlockSpec(a.shape, lambda i, j, _nd=nd: (0,) * _nd)

    nb = MIXER_SEQS
    seq_tile = lambda i, j: (i, j, 0)
    x1, h2, route, counts = pl.pallas_call(
        functools.partial(_mixer_kernel, pool_w=pool_w, ssd_w=ssd_w),
        grid=(b // nb, s // ts),
        in_specs=[pl.BlockSpec((nb, ts, d), seq_tile),
                  pl.BlockSpec((nb, 6, d), lambda i, j: (i, 0, 0)),
                  full(win), full(wp), full(ps), full(cw), full(cb), full(dtb), full(alog),
                  full(dsk), full(nw), full(wout), full(wrh), full(wrl), full(br), full(rep), full(tril3)],
        out_specs=[pl.BlockSpec((nb, ts, d), seq_tile),
                   pl.BlockSpec((nb, ts * nrow, LANES), seq_tile),
                   pl.BlockSpec((nb, ts, LANES), seq_tile),
                   pl.BlockSpec((1, LANES), lambda i, j: (0, 0))],
        out_shape=[jax.ShapeDtypeStruct((b, s, d), F32),
                   jax.ShapeDtypeStruct((b, s * nrow, LANES), F32),
                   jax.ShapeDtypeStruct((b, s, LANES), F32),
                   jax.ShapeDtypeStruct((1, LANES), F32)],
        scratch_shapes=[pltpu.VMEM((nb, ssd_w // LANES, SSD_STATE, LANES), F32),
                        pltpu.VMEM((nb, conv_dim // LANES, (ts // SUBLANES + 2 * CONV_PAD) * CONV_PITCH, LANES),
                                   F32),
                        pltpu.VMEM((nb, POOL_HALO + ts, pool_w), F32),
                        pltpu.VMEM((nb, ts, conv_dim), F32),
                        pltpu.VMEM((nb, ts, d), BF16)],
        compiler_params=pltpu.CompilerParams(dimension_semantics=("arbitrary", "arbitrary"),
                                             vmem_limit_bytes=VMEM_LIMIT_BYTES),
        name="mixer",
    )(x, mod, win, wp, ps, cw, cb, dtb, alog, dsk, nw, wout, wrh, wrl, br, rep, tril3)
    return x1, h2.reshape(b * s * nrow, LANES), route.reshape(b * s, LANES), counts


def _rank_kernel(route_ref, start_ref, dest_ref, carry_ref):
    ta = route_ref.shape[0]

    @pl.when(pl.program_id(0) == 0)
    def _():
        carry_ref[...] = start_ref[...]

    route_t = route_ref[...].T
    e_idx = lax.broadcasted_iota(jnp.int32, (LANES, ta), 0).astype(F32)
    ohs = [jnp.where(e_idx == route_t[k:k + 1, :], 1.0, 0.0) for k in range(TOP_K)]
    oh = (ohs[0] + ohs[1]).astype(BF16)
    earlier = (lax.broadcasted_iota(jnp.int32, (ta, ta), 0)
               < lax.broadcasted_iota(jnp.int32, (ta, ta), 1))
    before = _dot(oh, jnp.where(earlier, 1.0, 0.0).astype(BF16))
    carry = carry_ref[...]
    base = jnp.concatenate([carry] * (ta // LANES), axis=1) + before
    for k in range(TOP_K):
        dest_ref[0, k:k + 1, :] = jnp.sum(ohs[k] * base, axis=0, keepdims=True).astype(jnp.int32)
    carry_ref[...] = carry + _dot(oh, jnp.ones((ta, LANES), BF16))


def _rank(route, start_rep):
    t = route.shape[0]
    return pl.pallas_call(
        _rank_kernel,
        grid=(t // ROW_COPY_TILE,),
        in_specs=[pl.BlockSpec((ROW_COPY_TILE, LANES), lambda i: (i, 0)),
                  pl.BlockSpec((LANES, LANES), lambda i: (0, 0))],
        out_specs=pl.BlockSpec((1, TOP_K, ROW_COPY_TILE), lambda i: (i, 0, 0)),
        out_shape=jax.ShapeDtypeStruct((t // ROW_COPY_TILE, TOP_K, ROW_COPY_TILE), jnp.int32),
        scratch_shapes=[pltpu.VMEM((LANES, LANES), F32)],
        compiler_params=pltpu.CompilerParams(dimension_semantics=("arbitrary",)),
        name="rank",
    )(route, start_rep)


def _dispatch_kernel(dest_ref, h2_ref, xs_hbm, sem, *, nrow):
    tile_rows = h2_ref.shape[0]

    part_tokens = dest_ref.shape[2]
    for part in range(dest_ref.shape[0]):
        def issue(blk, carry, part=part):
            for u in range(DMA_UNROLL):
                tk = blk * DMA_UNROLL + u
                for k in range(TOP_K):
                    pltpu.make_async_copy(_slab_of_row(h2_ref, part * part_tokens + tk, nrow),
                                          _slab_of_row(xs_hbm, dest_ref[part, k, tk], nrow),
                                          sem).start(priority=(u * TOP_K + k) % DMA_THREADS)
            return carry

        lax.fori_loop(0, part_tokens // DMA_UNROLL, issue, 0)

    for _ in range(TOP_K):
        pltpu.make_async_copy(h2_ref, xs_hbm.at[pl.ds(0, tile_rows)], sem).wait()


def _dispatch(dest, h2, nrow):
    t = h2.shape[0] // nrow
    return pl.pallas_call(
        functools.partial(_dispatch_kernel, nrow=nrow),
        grid=(t // (DISPATCH_PARTS * ROW_COPY_TILE),),
        in_specs=[pl.BlockSpec((DISPATCH_PARTS, TOP_K, ROW_COPY_TILE), lambda i: (i, 0, 0),
                               memory_space=pltpu.SMEM),
                  pl.BlockSpec((DISPATCH_PARTS * ROW_COPY_TILE * nrow, LANES), lambda i: (i, 0))],
        out_specs=pl.BlockSpec(memory_space=pl.ANY),
        out_shape=jax.ShapeDtypeStruct((t * TOP_K * nrow, LANES), F32),
        scratch_shapes=[pltpu.SemaphoreType.DMA(())],
        compiler_params=pltpu.CompilerParams(dimension_semantics=("arbitrary",),
                                             has_side_effects=True),
        name="dispatch",
    )(dest, h2)


def _expert_kernel(blk_ref, exp_ref, lo_ref, hi_ref, first_ref, new_ref, xs_ref, w13_ref, w2_ref, ys_ref,
                   w13_bf, w2_bf, *, nrow):
    i = pl.program_id(0)
    rows = xs_ref.shape[0] // nrow
    lo = lo_ref[i]
    hi = hi_ref[i]

    sub = rows // EXPERT_CHAINS

    @pl.when(new_ref[i] == 1)
    def _():
        w13_bf[...] = w13_ref[0].astype(BF16)
        w2_bf[...] = w2_ref[0].astype(BF16)

    def chain(r0):
        xb = jnp.concatenate([xs_ref[pl.ds(r0 * nrow + c, sub, stride=nrow), :].astype(BF16)
                              for c in range(nrow)], axis=1)
        yield
        hu = _dot(xb, w13_bf[...])
        yield
        f = hu.shape[1] // 2
        act = (_silu(hu[:, :f]) * hu[:, f:]).astype(BF16)
        yield
        y = _dot(act, w2_bf[...])
        yield
        ridx = r0 + lax.broadcasted_iota(jnp.int32, (sub, 1), 0)
        take = (ridx >= lo_eff) & (ridx < hi_eff)
        for c in range(nrow):
            sl = pl.ds(r0 * nrow + c, sub, stride=nrow)
            ys_ref[sl, :] = jnp.where(take, y[:, c * LANES:(c + 1) * LANES], ys_ref[sl, :])

    is_first = first_ref[i] == 1
    lo_eff = jnp.where(is_first, 0, lo)
    hi_eff = jnp.where(is_first, rows, hi)

    @pl.when(hi > lo)
    def _():
        _run_skewed([chain(k * sub) for k in range(EXPERT_CHAINS)])


def _experts(item_blk, item_exp, item_lo, item_hi, item_first, item_new, xs, w13, w2, nrow):
    a = xs.shape[0] // nrow
    n_items = item_blk.shape[0]
    d = w13.shape[1]
    ff2 = w13.shape[2]
    grid_spec = pltpu.PrefetchScalarGridSpec(
        num_scalar_prefetch=6,
        grid=(n_items,),
        in_specs=[pl.BlockSpec((EXPERT_BLOCK * nrow, LANES), lambda i, b, e, lo, hi, fr, nw: (b[i], 0)),
                  pl.BlockSpec((1, d, ff2), lambda i, b, e, lo, hi, fr, nw: (e[i], 0, 0)),
                  pl.BlockSpec((1, ff2 // 2, d), lambda i, b, e, lo, hi, fr, nw: (e[i], 0, 0))],
        out_specs=pl.BlockSpec((EXPERT_BLOCK * nrow, LANES), lambda i, b, e, lo, hi, fr, nw: (b[i], 0)),
        scratch_shapes=[pltpu.VMEM((d, ff2), BF16), pltpu.VMEM((ff2 // 2, d), BF16)],
    )
    return pl.pallas_call(
        functools.partial(_expert_kernel, nrow=nrow),
        grid_spec=grid_spec,
        out_shape=jax.ShapeDtypeStruct((a * nrow, LANES), F32),
        compiler_params=pltpu.CompilerParams(dimension_semantics=("arbitrary",),
                                             vmem_limit_bytes=VMEM_LIMIT_BYTES),
        name="experts",
    )(item_blk, item_exp, item_lo, item_hi, item_first, item_new, xs, w13, w2)


def _final_kernel(dest_ref, dnext_ref, x1_ref, route_ref, mod_ref, fw_ref, ys_hbm, o_ref, bufs, sems):
    tf = x1_ref.shape[1]
    nrow = bufs.shape[2] // tf
    step = pl.program_id(0)
    slot = lax.rem(step, 2)

    def start_gathers(d_ref, to_slot):
        def issue(blk, carry):
            for u in range(DMA_UNROLL):
                tk = blk * DMA_UNROLL + u
                for k in range(TOP_K):
                    pltpu.make_async_copy(_slab_of_row(ys_hbm, d_ref[0, k, tk], nrow),
                                          _slab_of_row(bufs.at[to_slot, k], tk, nrow),
                                          sems.at[to_slot]).start(priority=(u * TOP_K + k) % DMA_THREADS)
            return carry

        lax.fori_loop(0, tf // DMA_UNROLL, issue, 0)

    @pl.when(step == 0)
    def _():
        start_gathers(dest_ref, 0)

    @pl.when(step + 1 < pl.num_programs(0))
    def _():
        start_gathers(dnext_ref, 1 - slot)

    for k in range(TOP_K):
        pltpu.make_async_copy(ys_hbm.at[pl.ds(0, tf * nrow)], bufs.at[slot, k], sems.at[slot]).wait()

    r = route_ref[...]
    gate1 = r[:, 2:3]
    gate2 = r[:, 3:4]
    y = jnp.concatenate([c1 * gate1 + c2 * gate2 for c1, c2 in
                         zip(_slab_chunks(bufs.at[slot, 0], tf, nrow),
                             _slab_chunks(bufs.at[slot, 1], tf, nrow))], axis=1)
    g2 = mod_ref[0][5:6]
    o_ref[0] = _rms(x1_ref[0] + g2 * y) * fw_ref[...]


def _final(dest, x1, route, mod, fw, ys):
    b, s, d = x1.shape
    tf = ROW_COPY_TILE
    nrow = d // LANES
    spt = s // tf
    n_steps = b * spt
    return pl.pallas_call(
        _final_kernel,
        grid=(n_steps,),
        in_specs=[pl.BlockSpec((1, TOP_K, tf), lambda i: (i, 0, 0), memory_space=pltpu.SMEM),
                  pl.BlockSpec((1, TOP_K, tf), lambda i: (jnp.minimum(i + 1, n_steps - 1), 0, 0),
                               memory_space=pltpu.SMEM),
                  pl.BlockSpec((1, tf, d), lambda i: (i // spt, i % spt, 0)),
                  pl.BlockSpec((tf, LANES), lambda i: (i, 0)),
                  pl.BlockSpec((1, 6, d), lambda i: (i // spt, 0, 0)),
                  pl.BlockSpec((1, d), lambda i: (0, 0)),
                  pl.BlockSpec(memory_space=pl.ANY)],
        out_specs=pl.BlockSpec((1, tf, d), lambda i: (i // spt, i % spt, 0)),
        out_shape=jax.ShapeDtypeStruct((b, s, d), F32),
        scratch_shapes=[pltpu.VMEM((2, TOP_K, tf * nrow, LANES), F32),
                        pltpu.SemaphoreType.DMA((2,))],
        compiler_params=pltpu.CompilerParams(dimension_semantics=("arbitrary",),
                                             vmem_limit_bytes=VMEM_LIMIT_BYTES),
        name="final",
    )(dest, dest, x1, route, mod, fw, ys)


def _pad_lanes(v, fill=0.0):
    n = v.shape[-1]
    return jnp.pad(v, [(0, 0)] * (v.ndim - 1) + [(0, LANES - n)], constant_values=fill)


def _work_items(counts, n_blocks):
    n_items = n_blocks + N_EXPERTS - 1
    ends = jnp.cumsum(counts)
    starts = ends - counts
    first_blk = starts // EXPERT_BLOCK
    last_blk = jnp.maximum(ends - 1, starts) // EXPERT_BLOCK
    n_e = jnp.where(counts > 0, last_blk - first_blk + 1, 0)
    item_end = jnp.cumsum(n_e)
    item_start = item_end - n_e
    ids = jnp.arange(n_items, dtype=jnp.int32)
    total = item_end[-1]
    ids_c = jnp.minimum(ids, total - 1)
    e = jnp.sum((item_end[None, :] <= ids_c[:, None]).astype(jnp.int32), axis=1)
    onehot = (e[:, None] == jnp.arange(N_EXPERTS, dtype=jnp.int32)[None, :]).astype(jnp.int32)
    pick = lambda v: jnp.sum(onehot * v[None, :], axis=1)
    blk = pick(first_blk) + ids_c - pick(item_start)
    lo = jnp.clip(pick(starts) - blk * EXPERT_BLOCK, 0, EXPERT_BLOCK)
    hi = jnp.clip(pick(ends) - blk * EXPERT_BLOCK, 0, EXPERT_BLOCK)
    hi = jnp.where(ids < total, hi, lo)
    prev_blk = jnp.concatenate([jnp.full((1,), -1, jnp.int32), blk[:-1]])
    first = (blk != prev_blk).astype(jnp.int32)
    prev_e = jnp.concatenate([jnp.full((1,), -1, jnp.int32), e[:-1]])
    new_expert = (e != prev_e).astype(jnp.int32)
    return blk, e, lo, hi, first, new_expert


def kernel(x, c, w_ada, b_ada, w_in, w_pool, pool_scale, conv_w, conv_b, dt_bias, a_log, d_skip,
           ssd_norm_w, w_out, w_group, b_group, w_router, b_router, w13, w2, final_norm_w):
    b, s, d = x.shape
    depth = w_ada.shape[0]
    t = b * s
    pool_w = w_pool.shape[1] * w_pool.shape[2]
    ssd_w = SSD_HEADS * SSD_HEAD_DIM
    conv_dim = conv_w.shape[2]
    assert s % SEQ_TILE == 0 and SEQ_TILE % SSD_CHUNK == 0 and b % MIXER_SEQS == 0
    assert s % ROW_COPY_TILE == 0 and ROW_COPY_TILE % LANES == 0
    assert (t * TOP_K) % EXPERT_BLOCK == 0 and d % LANES == 0
    assert pool_w == 2 * LANES and len(POOL_WINDOWS) * POOL_GROUP_DIM == pool_w
    assert N_EXPERT_GROUPS + N_EXPERTS <= LANES and SSD_HEADS <= LANES
    assert depth == 1, "the final RMSNorm is fused into the last layer's combine step"

    for layer in range(depth):
        mod = _ada(c, w_ada[layer], b_ada[layer]).reshape(b, 6, d)

        o_dt = pool_w + ssd_w + conv_dim
        win = jnp.concatenate([w_in[layer][:, :o_dt], _pad_lanes(w_in[layer][:, o_dt:])], axis=1).astype(BF16)
        wp = jnp.zeros((pool_w, pool_w), F32)
        for g in range(len(POOL_WINDOWS)):
            sl = slice(g * POOL_GROUP_DIM, (g + 1) * POOL_GROUP_DIM)
            wp = wp.at[sl, sl].set(w_pool[layer, g])
        wp = wp.astype(BF16)
        w_route = _pad_lanes(jnp.concatenate([w_group[layer], w_router[layer]], axis=1))
        wrh = w_route.astype(BF16)
        wrl = (w_route - wrh.astype(F32)).astype(BF16)
        br = _pad_lanes(jnp.concatenate([b_group[layer], b_router[layer]])[None, :])

        x1, h2, route, counts_f = _mixer(
            x, mod, win, wp, pool_scale[layer][None, :], conv_w[layer], conv_b[layer][None, :],
            _pad_lanes(dt_bias[layer][None, :]), _pad_lanes(a_log[layer][None, :], fill=NEG_BIG),
            jnp.repeat(d_skip[layer], SSD_HEAD_DIM)[None, :], ssd_norm_w[layer][None, :],
            w_out[layer].astype(BF16), wrh, wrl, br, pool_w=pool_w, ssd_w=ssd_w)

        start_f = jnp.cumsum(counts_f, axis=1) - counts_f
        start_rep = jnp.broadcast_to(start_f.reshape(LANES, 1), (LANES, LANES))
        dest = _rank(route, start_rep)
        counts = counts_f[0, :N_EXPERTS].astype(jnp.int32)
        items = _work_items(counts, (t * TOP_K) // EXPERT_BLOCK)

        nrow = d // LANES
        xs = _dispatch(dest, h2, nrow)
        ys = _experts(*items, xs, w13[layer], w2[layer], nrow)
        x = _final(dest, x1, route, mod, final_norm_w[None, :], ys)
    return x
```

```python
import functools

import jax
import jax.numpy as jnp
from jax import lax
from jax.experimental import pallas as pl
from jax.experimental.pallas import tpu as pltpu

POOL_WINDOWS = (2, 4, 8, 16)
POOL_GROUP_DIM = 64
SSD_HEAD_DIM = 64
SSD_GROUPS = 4
SSD_HEADS_PER_GROUP = 3
SSD_HEADS = SSD_GROUPS * SSD_HEADS_PER_GROUP
SSD_STATE = 128
SSD_CONV = 4
SSD_CHUNK = 128
N_EXPERT_GROUPS = 4
EXPERTS_PER_GROUP = 8
N_EXPERTS = N_EXPERT_GROUPS * EXPERTS_PER_GROUP
TOP_K = 2
NORM_EPS = 1e-6

LANES = 128
SUBLANES = 8
VMEM_LIMIT_BYTES = 56 * 1024 * 1024

SEQ_TILE = 256
MIXER_SEQS = 2
PROJ_PIECE = 1024
NORM_ROWS = 32
MIXER_ORDER = (0, 1, 0, 0, 0,
               1, 0, 0, 0,
               1, 0,
               1, 0,
               0, 1, 1, 1,
               0, 1,
               0, 0, 1,
               1, 1, 1, 1)
CONV_PAD = SUBLANES
CONV_PITCH = SUBLANES + 1
POOL_HALO = 16
ROW_COPY_TILE = 512
EXPERT_BLOCK = 512
EXPERT_CHAINS = 2
DISPATCH_PARTS = 2
DMA_THREADS = 2

NEG_BIG = -1e30
F32 = jnp.float32
BF16 = jnp.bfloat16


def _silu(v):
    half = 0.5 * v
    return half + half * jnp.tanh(half)


def _softplus(v):
    return jnp.maximum(v, 0.0) + jnp.log(1.0 + jnp.exp(-jnp.abs(v)))


def _rms(v):
    return v * lax.rsqrt(jnp.mean(v * v, axis=-1, keepdims=True) + NORM_EPS)


def _dot(a, b):
    return jnp.dot(a, b, preferred_element_type=F32)


def _dot_exact(a, b):
    return jnp.dot(a, b, preferred_element_type=F32, precision=lax.Precision.HIGHEST)


def _run_skewed(chains):
    live = [False] * len(chains)
    started = 0
    while started < len(chains) or any(live):
        if started < len(chains):
            live[started] = True
            started += 1
        for k, chain in enumerate(chains):
            if live[k]:
                try:
                    next(chain)
                except StopIteration:
                    live[k] = False


def _run_ordered(chains, order):
    for k in order:
        next(chains[k], None)
    for chain in chains:
        assert next(chain, "done") == "done", "order does not cover every phase"


def _slab_chunks(ref, rows, nrow):
    return [ref[pl.ds(c, rows, stride=nrow), :] for c in range(nrow)]


def _slab_store(ref, val, nrow):
    rows = val.shape[0]
    for c in range(nrow):
        ref[pl.ds(c, rows, stride=nrow), :] = val[:, c * LANES:(c + 1) * LANES]


def _slab_of_row(ref, row, nrow):
    start = row * nrow
    return ref.at[pl.ds(start if isinstance(start, int) else pl.multiple_of(start, nrow), nrow)]


def _ada_kernel(c_ref, w_ref, b_ref, o_ref):
    o_ref[...] = _dot_exact(_silu(c_ref[...]), w_ref[...]) + b_ref[...]


def _ada(c, w_ada, b_ada):
    b, d = c.shape
    n = w_ada.shape[1]
    return pl.pallas_call(
        _ada_kernel,
        grid=(n // d,),
        in_specs=[pl.BlockSpec((b, d), lambda i: (0, 0)),
                  pl.BlockSpec((d, d), lambda i: (0, i)),
                  pl.BlockSpec((1, d), lambda i: (0, i))],
        out_specs=pl.BlockSpec((b, d), lambda i: (0, i)),
        out_shape=jax.ShapeDtypeStruct((b, n), F32),
        compiler_params=pltpu.CompilerParams(dimension_semantics=("arbitrary",),
                                             vmem_limit_bytes=VMEM_LIMIT_BYTES),
        name="ada",
    )(c, w_ada, b_ada.reshape(1, n))


def _mixer_kernel(x_ref, mod_ref, win_ref, wp_ref, ps_ref, cw_ref, cb_ref, dtb_ref, alog_ref,
                  dsk_ref, nw_ref, wout_ref, wrh_ref, wrl_ref, br_ref, rep_ref, tril3_ref,
                  x1_ref, h2_ref, route_ref, cnt_ref,
                  state_ref, xext_ref, uext_ref, xa_ref, hb_ref, *, pool_w, ssd_w):
    @pl.when((pl.program_id(0) == 0) & (pl.program_id(1) == 0))
    def _():
        cnt_ref[...] = jnp.zeros_like(cnt_ref)
        xext_ref[...] = jnp.zeros_like(xext_ref)

    @pl.when(pl.program_id(1) == 0)
    def _():
        state_ref[...] = jnp.zeros_like(state_ref)
        pc = x_ref.shape[1] // SUBLANES
        for bi in range(xext_ref.shape[0]):
            for lt in range(xext_ref.shape[1]):
                xext_ref[bi, lt, pl.ds(CONV_PAD * CONV_PITCH, pc, stride=CONV_PITCH), :] = (
                    jnp.zeros((pc, LANES), F32))
        uext_ref[:, 0:POOL_HALO, :] = jnp.zeros((uext_ref.shape[0], POOL_HALO, uext_ref.shape[2]), F32)

    _run_ordered([_mixer_tile(x_ref.at[bi], mod_ref.at[bi], win_ref, wp_ref, ps_ref, cw_ref, cb_ref, dtb_ref,
                             alog_ref, dsk_ref, nw_ref, wout_ref, wrh_ref, wrl_ref, br_ref, rep_ref, tril3_ref,
                             x1_ref.at[bi], h2_ref.at[bi], route_ref.at[bi], cnt_ref,
                             state_ref.at[bi], xext_ref.at[bi], uext_ref.at[bi], xa_ref.at[bi], hb_ref.at[bi], pool_w=pool_w, ssd_w=ssd_w)
                  for bi in range(x_ref.shape[0])], MIXER_ORDER)


def _mixer_tile(x_ref, mod_ref, win_ref, wp_ref, ps_ref, cw_ref, cb_ref, dtb_ref, alog_ref,
                dsk_ref, nw_ref, wout_ref, wrh_ref, wrl_ref, br_ref, rep_ref, tril3_ref,
                x1_ref, h2_ref, route_ref, cnt_ref,
                state_ref, xext_ref, uext_ref, xa_ref, hb_ref, *, pool_w, ssd_w):
    ts = x_ref.shape[0]
    d = x_ref.shape[1]
    L = SSD_CHUNK
    N = SSD_STATE
    P = SSD_HEAD_DIM
    j = pl.program_id(1)

    mod = mod_ref[...]
    sh1, sc1, g1 = mod[0:1], mod[1:2], mod[2:3]
    sh2, sc2 = mod[3:4], mod[4:5]

    for r in range(0, ts, NORM_ROWS):
        hb_ref[r:r + NORM_ROWS, :] = (_rms(x_ref[r:r + NORM_ROWS, :]) * (1.0 + sc1) + sh1).astype(BF16)
    h = hb_ref[...]
    yield
    n_proj = win_ref.shape[1]
    edges = [min(n_proj, e * PROJ_PIECE) for e in range(pl.cdiv(n_proj, PROJ_PIECE) + 1)]
    pieces = []
    for a, b in zip(edges[:-1], edges[1:]):
        pieces.append(_dot(h, win_ref[:, a:b]))
        yield
    proj = jnp.concatenate(pieces, axis=1)
    o_z = pool_w
    o_x = pool_w + ssd_w
    conv_dim = ssd_w + 2 * SSD_GROUPS * N
    o_dt = o_x + conv_dim
    u = proj[:, 0:pool_w]
    z = proj[:, o_z:o_x]
    xbc = proj[:, o_x:o_dt]
    dtr = proj[:, o_dt:o_dt + LANES]

    uext_ref[POOL_HALO:POOL_HALO + ts, :] = u
    lane_u = lax.broadcasted_iota(jnp.int32, (1, LANES), 1)
    upper = lane_u >= POOL_GROUP_DIM

    def ush(k, col):
        return uext_ref[POOL_HALO - k:POOL_HALO - k + ts, col * LANES:(col + 1) * LANES]

    w0, w1, w2, w3 = POOL_WINDOWS
    lo_a = ush(0, 0)
    for k in range(1, w0):
        lo_a = lo_a + ush(k, 0)
    lo_b = ush(w0, 0)
    for k in range(w0 + 1, w1):
        lo_b = lo_b + ush(k, 0)
    hi_a = ush(0, 1)
    for k in range(1, w2):
        hi_a = hi_a + ush(k, 1)
    hi_b = ush(w2, 1)
    for k in range(w2 + 1, w3):
        hi_b = hi_b + ush(k, 1)
    win_lo = lo_a + jnp.where(upper, lo_b, 0.0)
    win_hi = hi_a + jnp.where(upper, hi_b, 0.0)
    pos = (j * ts + 1 + lax.broadcasted_iota(jnp.int32, (ts, 1), 0)).astype(F32)
    cnt_lo = jnp.minimum(pos, jnp.where(upper, float(w1), float(w0)))
    cnt_hi = jnp.minimum(pos, jnp.where(upper, float(w3), float(w2)))
    pdiff = jnp.concatenate([win_lo / cnt_lo, win_hi / cnt_hi], axis=1) - u
    y_pool = _dot(pdiff.astype(BF16), wp_ref[...]) * ps_ref[...]
    uext_ref[0:POOL_HALO, :] = uext_ref[ts:ts + POOL_HALO, :]

    a_neg = -jnp.exp(alog_ref[...])
    dt_all = _softplus(dtr + dtb_ref[...])
    da = dt_all * a_neg
    d1 = da.astype(BF16)
    r1 = da - d1.astype(F32)
    d2 = r1.astype(BF16)
    d3 = (r1 - d2.astype(F32)).astype(BF16)
    a_cs = _dot(tril3_ref[...], jnp.concatenate([d1, d2, d3], axis=0))
    a_hi = a_cs.astype(BF16)
    a_lo = (a_cs - a_hi.astype(F32)).astype(BF16)
    a_q_all = a_hi.astype(F32) + a_lo.astype(F32)
    a_rep_all = _dot(jnp.concatenate([a_hi, a_lo], axis=1), rep_ref[...])
    yield

    pc = ts // SUBLANES
    n_lt = xext_ref.shape[0]
    for lt in range(n_lt):
        for col in range(SUBLANES):
            xext_ref[lt, pl.ds(CONV_PAD * CONV_PITCH + col + 1, pc, stride=CONV_PITCH), :] = (
                xbc[col * pc:(col + 1) * pc, lt * LANES:(lt + 1) * LANES])
    yield
    sub_i = lax.broadcasted_iota(jnp.int32, (SUBLANES, 1), 0)
    for lt in range(n_lt):
        lanes = slice(lt * LANES, (lt + 1) * LANES)
        cw = cw_ref[:, lanes]
        for col in range(SUBLANES):
            acc = cb_ref[:, lanes] + cw[SSD_CONV - 1:SSD_CONV] * xbc[col * pc:(col + 1) * pc, lanes]
            for k in range(1, SSD_CONV):
                main = xext_ref[lt, pl.ds((CONV_PAD - k) * CONV_PITCH + col + 1, pc, stride=CONV_PITCH), :]
                prev = xext_ref[lt, pl.ds((CONV_PAD + pc - k) * CONV_PITCH + col, SUBLANES,
                                          stride=CONV_PITCH), :]
                tap = jnp.concatenate([jnp.where(sub_i < k, prev, main[0:SUBLANES]), main[SUBLANES:]], axis=0)
                acc = acc + cw[SSD_CONV - 1 - k:SSD_CONV - k] * tap
            xa_ref[col * pc:(col + 1) * pc, lanes] = _silu(acc)
    for lt in range(n_lt):
        xext_ref[lt, pl.ds(CONV_PAD * CONV_PITCH, pc, stride=CONV_PITCH), :] = (
            xext_ref[lt, pl.ds(CONV_PAD * CONV_PITCH + SUBLANES, pc, stride=CONV_PITCH), :])
    yield

    row_i = lax.broadcasted_iota(jnp.int32, (L, L), 0)
    col_i = lax.broadcasted_iota(jnp.int32, (L, L), 1)
    causal = row_i >= col_i
    hpt = LANES // P
    lane_h = lax.broadcasted_iota(jnp.int32, (1, LANES), 1)
    y_chunks = []
    for c in range(ts // L):
        r0 = c * L
        xs_c = xa_ref[r0:r0 + L,0:ssd_w]
        b_c = xa_ref[r0:r0 + L,ssd_w:ssd_w + SSD_GROUPS * N]
        c_c = xa_ref[r0:r0 + L,ssd_w + SSD_GROUPS * N:conv_dim]
        a_rep = a_rep_all[r0:r0 + L]
        a_q_t = a_q_all[r0:r0 + L].T
        dt_t = dt_all[r0:r0 + L].T
        cbs, cgs, bg_ts = [], [], []
        for g in range(SSD_GROUPS):
            bg = b_c[:, g * N:(g + 1) * N]
            cg = c_c[:, g * N:(g + 1) * N]
            cbs.append(lax.dot_general(cg.astype(BF16), bg.astype(BF16), (((1,), (1,)), ((), ())),
                                       preferred_element_type=F32))
            cgs.append(cg)
            bg_ts.append(bg.T)
        y_tiles = []
        for q in range(SSD_HEADS // hpt):
            xs_q = xs_c[:, q * LANES:(q + 1) * LANES].astype(BF16)
            st = state_ref[q]
            rhs = jnp.concatenate([xs_q, st.astype(BF16)], axis=0)
            y_q = upd_q = keep_q = None
            for hh in range(hpt):
                hd = q * hpt + hh
                g = hd // SSD_HEADS_PER_GROUP
                a_col = a_rep[:, hd * LANES:(hd + 1) * LANES]
                a_row = a_q_t[hd:hd + 1, :]
                dt_row = dt_t[hd:hd + 1, :]
                decay = jnp.exp(jnp.where(causal, a_col - a_row, NEG_BIG))
                m = cbs[g] * decay * dt_row
                cs = cgs[g] * jnp.exp(a_col)
                lhs = jnp.concatenate([m, cs], axis=1).astype(BF16)
                y_h = _dot(lhs, rhs)
                a_end = a_row[:, L - 1:L]
                w_row = dt_row * jnp.exp(a_end - a_row)
                upd_h = _dot((bg_ts[g] * w_row).astype(BF16), xs_q)
                keep_h = jnp.exp(a_end)
                if hh == 0:
                    y_q, upd_q, keep_q = y_h, upd_h, keep_h
                else:
                    mine = (lane_h >= hh * P) & (lane_h < (hh + 1) * P)
                    y_q = jnp.where(mine, y_h, y_q)
                    upd_q = jnp.where(mine, upd_h, upd_q)
                    keep_q = jnp.where(mine, keep_h, keep_q)
            y_tiles.append(y_q)
            state_ref[q] = keep_q * st + upd_q
        y_chunks.append(jnp.concatenate(y_tiles, axis=1))
        yield
    y_all = jnp.concatenate(y_chunks, axis=0) if len(y_chunks) > 1 else y_chunks[0]

    gw = ssd_w // SSD_GROUPS
    lane_y = lax.broadcasted_iota(jnp.int32, (1, ssd_w), 1)
    for r in range(0, ts, NORM_ROWS):
        rows = slice(r, r + NORM_ROWS)
        y = (y_all[rows] + xa_ref[rows, 0:ssd_w] * dsk_ref[...]) * _silu(z[rows])
        y2 = y * y
        scale = jnp.zeros_like(y)
        for g in range(SSD_GROUPS):
            in_g = (lane_y >= g * gw) & (lane_y < (g + 1) * gw)
            ms = jnp.sum(jnp.where(in_g, y2, 0.0), axis=-1, keepdims=True) * (1.0 / gw)
            scale = scale + jnp.where(in_g, lax.rsqrt(ms + NORM_EPS), 0.0)
        hb_ref[rows, :] = jnp.concatenate([y_pool[rows], y * scale * nw_ref[...]], axis=1).astype(BF16)
    yield
    mix = _dot(hb_ref[...], wout_ref[...])
    yield

    nrow = d // LANES
    for r in range(0, ts, NORM_ROWS):
        rows = slice(r, r + NORM_ROWS)
        x1 = x_ref[rows, :] + g1 * mix[rows]
        x1_ref[rows, :] = x1
        h2 = _rms(x1) * (1.0 + sc2) + sh2
        for c in range(nrow):
            h2_ref[pl.ds(r * nrow + c, NORM_ROWS, stride=nrow), :] = h2[:, c * LANES:(c + 1) * LANES]
        hb_ref[rows, :] = h2.astype(BF16)
    h_hi = hb_ref[...]
    yield
    logits = _dot(h_hi, wrh_ref[...]) + br_ref[...]
    lane = lax.broadcasted_iota(jnp.int32, (1, LANES), 1).astype(F32)
    big = float(LANES)
    gl = jnp.where(lane < N_EXPERT_GROUPS, logits, NEG_BIG)
    gmax = jnp.max(gl, axis=-1, keepdims=True)
    gsum = jnp.sum(jnp.exp(gl - gmax), axis=-1, keepdims=True)
    p_g = 1.0 / gsum
    g_idx = jnp.min(jnp.where(gl == gmax, lane, big), axis=-1, keepdims=True)
    lo = N_EXPERT_GROUPS + EXPERTS_PER_GROUP * g_idx
    el = jnp.where((lane >= lo) & (lane < lo + EXPERTS_PER_GROUP), logits, NEG_BIG)
    v1 = jnp.max(el, axis=-1, keepdims=True)
    i1 = jnp.min(jnp.where(el == v1, lane, big), axis=-1, keepdims=True)
    el2 = jnp.where(lane == i1, NEG_BIG, el)
    v2 = jnp.max(el2, axis=-1, keepdims=True)
    i2 = jnp.min(jnp.where(el2 == v2, lane, big), axis=-1, keepdims=True)
    e21 = jnp.exp(v2 - v1)
    gate1 = p_g / (1.0 + e21)
    gate2 = p_g * e21 / (1.0 + e21)
    route = jnp.where(lane == 0, i1 - N_EXPERT_GROUPS,
                      jnp.where(lane == 1, i2 - N_EXPERT_GROUPS,
                                jnp.where(lane == 2, gate1, jnp.where(lane == 3, gate2, 0.0))))
    route_ref[...] = route
    chosen = (lane == i1 - N_EXPERT_GROUPS) | (lane == i2 - N_EXPERT_GROUPS)
    cnt_ref[...] += jnp.sum(jnp.where(chosen, 1.0, 0.0), axis=0, keepdims=True)


def _mixer(x, mod, win, wp, ps, cw, cb, dtb, alog, dsk, nw, wout, wrh, wrl, br, *, pool_w, ssd_w):
    k_idx = jnp.arange(2 * LANES, dtype=jnp.int32)[:, None] % LANES
    h_idx = jnp.arange(SSD_HEADS * LANES, dtype=jnp.int32)[None, :] // LANES
    rep = (k_idx == h_idx).astype(BF16)
    r_idx = jnp.arange(SEQ_TILE, dtype=jnp.int32)[:, None]
    c_idx = jnp.arange(3 * SEQ_TILE, dtype=jnp.int32)[None, :] % SEQ_TILE
    tril3 = ((r_idx >= c_idx) & (r_idx // SSD_CHUNK == c_idx // SSD_CHUNK)).astype(BF16)
    b, s, d = x.shape
    ts = SEQ_TILE
    conv_dim = cw.shape[1]
    nrow = d // LANES

    def full(a):
        nd = a.ndim
        return pl.BlockSpec(a.shape, lambda i, j, _nd=nd: (0,) * _nd)

    nb = MIXER_SEQS
    seq_tile = lambda i, j: (i, j, 0)
    x1, h2, route, counts = pl.pallas_call(
        functools.partial(_mixer_kernel, pool_w=pool_w, ssd_w=ssd_w),
        grid=(b // nb, s // ts),
        in_specs=[pl.BlockSpec((nb, ts, d), seq_tile),
                  pl.BlockSpec((nb, 6, d), lambda i, j: (i, 0, 0)),
                  full(win), full(wp), full(ps), full(cw), full(cb), full(dtb), full(alog),
                  full(dsk), full(nw), full(wout), full(wrh), full(wrl), full(br), full(rep), full(tril3)],
        out_specs=[pl.BlockSpec((nb, ts, d), seq_tile),
                   pl.BlockSpec((nb, ts * nrow, LANES), seq_tile),
                   pl.BlockSpec((nb, ts, LANES), seq_tile),
                   pl.BlockSpec((1, LANES), lambda i, j: (0, 0))],
        out_shape=[jax.ShapeDtypeStruct((b, s, d), F32),
                   jax.ShapeDtypeStruct((b, s * nrow, LANES), F32),
                   jax.ShapeDtypeStruct((b, s, LANES), F32),
                   jax.ShapeDtypeStruct((1, LANES), F32)],
        scratch_shapes=[pltpu.VMEM((nb, ssd_w // LANES, SSD_STATE, LANES), F32),
                        pltpu.VMEM((nb, conv_dim // LANES, (ts // SUBLANES + 2 * CONV_PAD) * CONV_PITCH, LANES),
                                   F32),
                        pltpu.VMEM((nb, POOL_HALO + ts, pool_w), F32),
                        pltpu.VMEM((nb, ts, conv_dim), F32),
                        pltpu.VMEM((nb, ts, d), BF16)],
        compiler_params=pltpu.CompilerParams(dimension_semantics=("arbitrary", "arbitrary"),
                                             vmem_limit_bytes=VMEM_LIMIT_BYTES),
        name="mixer",
    )(x, mod, win, wp, ps, cw, cb, dtb, alog, dsk, nw, wout, wrh, wrl, br, rep, tril3)
    return x1, h2.reshape(b * s * nrow, LANES), route.reshape(b * s, LANES), counts


def _rank_kernel(route_ref, start_ref, dest_ref, carry_ref):
    ta = route_ref.shape[0]

    @pl.when(pl.program_id(0) == 0)
    def _():
        carry_ref[...] = start_ref[...]

    route_t = route_ref[...].T
    e_idx = lax.broadcasted_iota(jnp.int32, (LANES, ta), 0).astype(F32)
    ohs = [jnp.where(e_idx == route_t[k:k + 1, :], 1.0, 0.0) for k in range(TOP_K)]
    oh = (ohs[0] + ohs[1]).astype(BF16)
    earlier = (lax.broadcasted_iota(jnp.int32, (ta, ta), 0)
               < lax.broadcasted_iota(jnp.int32, (ta, ta), 1))
    before = _dot(oh, jnp.where(earlier, 1.0, 0.0).astype(BF16))
    carry = carry_ref[...]
    base = jnp.concatenate([carry] * (ta // LANES), axis=1) + before
    for k in range(TOP_K):
        dest_ref[0, k:k + 1, :] = jnp.sum(ohs[k] * base, axis=0, keepdims=True).astype(jnp.int32)
    carry_ref[...] = carry + _dot(oh, jnp.ones((ta, LANES), BF16))


def _rank(route, start_rep):
    t = route.shape[0]
    return pl.pallas_call(
        _rank_kernel,
        grid=(t // ROW_COPY_TILE,),
        in_specs=[pl.BlockSpec((ROW_COPY_TILE, LANES), lambda i: (i, 0)),
                  pl.BlockSpec((LANES, LANES), lambda i: (0, 0))],
        out_specs=pl.BlockSpec((1, TOP_K, ROW_COPY_TILE), lambda i: (i, 0, 0)),
        out_shape=jax.ShapeDtypeStruct((t // ROW_COPY_TILE, TOP_K, ROW_COPY_TILE), jnp.int32),
        scratch_shapes=[pltpu.VMEM((LANES, LANES), F32)],
        compiler_params=pltpu.CompilerParams(dimension_semantics=("arbitrary",)),
        name="rank",
    )(route, start_rep)


def _dispatch_kernel(dest_ref, h2_ref, xs_hbm, sem, *, nrow):
    tile_rows = h2_ref.shape[0]

    part_tokens = dest_ref.shape[2]
    for part in range(dest_ref.shape[0]):
        for tk in range(part_tokens):
            for k in range(TOP_K):
                pltpu.make_async_copy(_slab_of_row(h2_ref, part * part_tokens + tk, nrow),
                                      _slab_of_row(xs_hbm, dest_ref[part, k, tk], nrow),
                                      sem).start(priority=(tk * TOP_K + k) % DMA_THREADS)

    for _ in range(TOP_K):
        pltpu.make_async_copy(h2_ref, xs_hbm.at[pl.ds(0, tile_rows)], sem).wait()


def _dispatch(dest, h2, nrow):
    t = h2.shape[0] // nrow
    return pl.pallas_call(
        functools.partial(_dispatch_kernel, nrow=nrow),
        grid=(t // (DISPATCH_PARTS * ROW_COPY_TILE),),
        in_specs=[pl.BlockSpec((DISPATCH_PARTS, TOP_K, ROW_COPY_TILE), lambda i: (i, 0, 0),
                               memory_space=pltpu.SMEM),
                  pl.BlockSpec((DISPATCH_PARTS * ROW_COPY_TILE * nrow, LANES), lambda i: (i, 0))],
        out_specs=pl.BlockSpec(memory_space=pl.ANY),
        out_shape=jax.ShapeDtypeStruct((t * TOP_K * nrow, LANES), F32),
        scratch_shapes=[pltpu.SemaphoreType.DMA(())],
        compiler_params=pltpu.CompilerParams(dimension_semantics=("arbitrary",),
                                             has_side_effects=True),
        name="dispatch",
    )(dest, h2)


def _expert_kernel(blk_ref, exp_ref, lo_ref, hi_ref, first_ref, new_ref, xs_ref, w13_ref, w2_ref, ys_ref,
                   w13_bf, w2_bf, *, nrow):
    i = pl.program_id(0)
    rows = xs_ref.shape[0] // nrow
    lo = lo_ref[i]
    hi = hi_ref[i]

    sub = rows // EXPERT_CHAINS

    @pl.when(new_ref[i] == 1)
    def _():
        w13_bf[...] = w13_ref[0].astype(BF16)
        w2_bf[...] = w2_ref[0].astype(BF16)

    def chain(r0):
        xb = jnp.concatenate([xs_ref[pl.ds(r0 * nrow + c, sub, stride=nrow), :].astype(BF16)
                              for c in range(nrow)], axis=1)
        yield
        hu = _dot(xb, w13_bf[...])
        yield
        f = hu.shape[1] // 2
        act = (_silu(hu[:, :f]) * hu[:, f:]).astype(BF16)
        yield
        y = _dot(act, w2_bf[...])
        yield
        ridx = r0 + lax.broadcasted_iota(jnp.int32, (sub, 1), 0)
        take = (ridx >= lo_eff) & (ridx < hi_eff)
        for c in range(nrow):
            sl = pl.ds(r0 * nrow + c, sub, stride=nrow)
            ys_ref[sl, :] = jnp.where(take, y[:, c * LANES:(c + 1) * LANES], ys_ref[sl, :])

    is_first = first_ref[i] == 1
    lo_eff = jnp.where(is_first, 0, lo)
    hi_eff = jnp.where(is_first, rows, hi)

    @pl.when(hi > lo)
    def _():
        _run_skewed([chain(k * sub) for k in range(EXPERT_CHAINS)])


def _experts(item_blk, item_exp, item_lo, item_hi, item_first, item_new, xs, w13, w2, nrow):
    a = xs.shape[0] // nrow
    n_items = item_blk.shape[0]
    d = w13.shape[1]
    ff2 = w13.shape[2]
    grid_spec = pltpu.PrefetchScalarGridSpec(
        num_scalar_prefetch=6,
        grid=(n_items,),
        in_specs=[pl.BlockSpec((EXPERT_BLOCK * nrow, LANES), lambda i, b, e, lo, hi, fr, nw: (b[i], 0)),
                  pl.BlockSpec((1, d, ff2), lambda i, b, e, lo, hi, fr, nw: (e[i], 0, 0)),
                  pl.BlockSpec((1, ff2 // 2, d), lambda i, b, e, lo, hi, fr, nw: (e[i], 0, 0))],
        out_specs=pl.BlockSpec((EXPERT_BLOCK * nrow, LANES), lambda i, b, e, lo, hi, fr, nw: (b[i], 0)),
        scratch_shapes=[pltpu.VMEM((d, ff2), BF16), pltpu.VMEM((ff2 // 2, d), BF16)],
    )
    return pl.pallas_call(
        functools.partial(_expert_kernel, nrow=nrow),
        grid_spec=grid_spec,
        out_shape=jax.ShapeDtypeStruct((a * nrow, LANES), F32),
        compiler_params=pltpu.CompilerParams(dimension_semantics=("arbitrary",),
                                             vmem_limit_bytes=VMEM_LIMIT_BYTES),
        name="experts",
    )(item_blk, item_exp, item_lo, item_hi, item_first, item_new, xs, w13, w2)


def _final_kernel(dest_ref, dnext_ref, x1_ref, route_ref, mod_ref, fw_ref, ys_hbm, o_ref,
                  buf_even, buf_odd, sems):
    tf = x1_ref.shape[1]
    nrow = buf_even.shape[1] // tf
    step = pl.program_id(0)
    last = step + 1 == pl.num_programs(0)

    def start_gathers(d_ref, buf, sem, t0, t1):
        for tk in range(t0, t1):
            for k in range(TOP_K):
                pltpu.make_async_copy(_slab_of_row(ys_hbm, d_ref[0, k, tk], nrow),
                                      _slab_of_row(buf.at[k], tk, nrow),
                                      sem).start(priority=(tk * TOP_K + k) % DMA_THREADS)

    def wait_gathers(buf, sem):
        for k in range(TOP_K):
            pltpu.make_async_copy(ys_hbm.at[pl.ds(0, tf * nrow)], buf.at[k], sem).wait()

    @pl.when(step == 0)
    def _():
        start_gathers(dest_ref, buf_even, sems.at[0], 0, tf)

    def body(cur, cur_sem, nxt, nxt_sem):
        wait_gathers(cur, cur_sem)
        g2 = mod_ref[0][5:6]
        for r0 in range(0, tf, NORM_ROWS):
            start_gathers(dnext_ref, nxt, nxt_sem, r0, r0 + NORM_ROWS)
            rows = slice(r0, r0 + NORM_ROWS)
            gates = route_ref[rows, :]
            y = jnp.concatenate(
                [cur[0, pl.ds(r0 * nrow + c, NORM_ROWS, stride=nrow), :] * gates[:, 2:3]
                 + cur[1, pl.ds(r0 * nrow + c, NORM_ROWS, stride=nrow), :] * gates[:, 3:4]
                 for c in range(nrow)], axis=1)
            o_ref[0, rows, :] = _rms(x1_ref[0, rows, :] + g2 * y) * fw_ref[...]

        @pl.when(last)
        def _():
            wait_gathers(nxt, nxt_sem)

    @pl.when(lax.rem(step, 2) == 0)
    def _():
        body(buf_even, sems.at[0], buf_odd, sems.at[1])

    @pl.when(lax.rem(step, 2) == 1)
    def _():
        body(buf_odd, sems.at[1], buf_even, sems.at[0])


def _final(dest, x1, route, mod, fw, ys):
    b, s, d = x1.shape
    tf = ROW_COPY_TILE
    nrow = d // LANES
    spt = s // tf
    n_steps = b * spt
    return pl.pallas_call(
        _final_kernel,
        grid=(n_steps,),
        in_specs=[pl.BlockSpec((1, TOP_K, tf), lambda i: (i, 0, 0), memory_space=pltpu.SMEM),
                  pl.BlockSpec((1, TOP_K, tf), lambda i: (jnp.minimum(i + 1, n_steps - 1), 0, 0),
                               memory_space=pltpu.SMEM),
                  pl.BlockSpec((1, tf, d), lambda i: (i // spt, i % spt, 0)),
                  pl.BlockSpec((tf, LANES), lambda i: (i, 0)),
                  pl.BlockSpec((1, 6, d), lambda i: (i // spt, 0, 0)),
                  pl.BlockSpec((1, d), lambda i: (0, 0)),
                  pl.BlockSpec(memory_space=pl.ANY)],
        out_specs=pl.BlockSpec((1, tf, d), lambda i: (i // spt, i % spt, 0)),
        out_shape=jax.ShapeDtypeStruct((b, s, d), F32),
        scratch_shapes=[pltpu.VMEM((TOP_K, tf * nrow, LANES), F32),
                        pltpu.VMEM((TOP_K, tf * nrow, LANES), F32),
                        pltpu.SemaphoreType.DMA((2,))],
        compiler_params=pltpu.CompilerParams(dimension_semantics=("arbitrary",),
                                             vmem_limit_bytes=VMEM_LIMIT_BYTES),
        name="final",
    )(dest, dest, x1, route, mod, fw, ys)


def _pad_lanes(v, fill=0.0):
    n = v.shape[-1]
    return jnp.pad(v, [(0, 0)] * (v.ndim - 1) + [(0, LANES - n)], constant_values=fill)


def _work_items(counts, n_blocks):
    n_items = n_blocks + N_EXPERTS - 1
    ends = jnp.cumsum(counts)
    starts = ends - counts
    first_blk = starts // EXPERT_BLOCK
    last_blk = jnp.maximum(ends - 1, starts) // EXPERT_BLOCK
    n_e = jnp.where(counts > 0, last_blk - first_blk + 1, 0)
    item_end = jnp.cumsum(n_e)
    item_start = item_end - n_e
    ids = jnp.arange(n_items, dtype=jnp.int32)
    total = item_end[-1]
    ids_c = jnp.minimum(ids, total - 1)
    e = jnp.sum((item_end[None, :] <= ids_c[:, None]).astype(jnp.int32), axis=1)
    onehot = (e[:, None] == jnp.arange(N_EXPERTS, dtype=jnp.int32)[None, :]).astype(jnp.int32)
    pick = lambda v: jnp.sum(onehot * v[None, :], axis=1)
    blk = pick(first_blk) + ids_c - pick(item_start)
    lo = jnp.clip(pick(starts) - blk * EXPERT_BLOCK, 0, EXPERT_BLOCK)
    hi = jnp.clip(pick(ends) - blk * EXPERT_BLOCK, 0, EXPERT_BLOCK)
    hi = jnp.where(ids < total, hi, lo)
    prev_blk = jnp.concatenate([jnp.full((1,), -1, jnp.int32), blk[:-1]])
    first = (blk != prev_blk).astype(jnp.int32)
    prev_e = jnp.concatenate([jnp.full((1,), -1, jnp.int32), e[:-1]])
    new_expert = (e != prev_e).astype(jnp.int32)
    return blk, e, lo, hi, first, new_expert


def kernel(x, c, w_ada, b_ada, w_in, w_pool, pool_scale, conv_w, conv_b, dt_bias, a_log, d_skip,
           ssd_norm_w, w_out, w_group, b_group, w_router, b_router, w13, w2, final_norm_w):
    b, s, d = x.shape
    depth = w_ada.shape[0]
    t = b * s
    pool_w = w_pool.shape[1] * w_pool.shape[2]
    ssd_w = SSD_HEADS * SSD_HEAD_DIM
    conv_dim = conv_w.shape[2]
    assert s % SEQ_TILE == 0 and SEQ_TILE % SSD_CHUNK == 0 and b % MIXER_SEQS == 0
    assert s % ROW_COPY_TILE == 0 and ROW_COPY_TILE % LANES == 0
    assert (t * TOP_K) % EXPERT_BLOCK == 0 and d % LANES == 0
    assert pool_w == 2 * LANES and len(POOL_WINDOWS) * POOL_GROUP_DIM == pool_w
    assert N_EXPERT_GROUPS + N_EXPERTS <= LANES and SSD_HEADS <= LANES
    assert depth == 1, "the final RMSNorm is fused into the last layer's combine step"

    for layer in range(depth):
        mod = _ada(c, w_ada[layer], b_ada[layer]).reshape(b, 6, d)

        o_dt = pool_w + ssd_w + conv_dim
        win = jnp.concatenate([w_in[layer][:, :o_dt], _pad_lanes(w_in[layer][:, o_dt:])], axis=1).astype(BF16)
        wp = jnp.zeros((pool_w, pool_w), F32)
        for g in range(len(POOL_WINDOWS)):
            sl = slice(g * POOL_GROUP_DIM, (g + 1) * POOL_GROUP_DIM)
            wp = wp.at[sl, sl].set(w_pool[layer, g])
        wp = wp.astype(BF16)
        w_route = _pad_lanes(jnp.concatenate([w_group[layer], w_router[layer]], axis=1))
        wrh = w_route.astype(BF16)
        wrl = (w_route - wrh.astype(F32)).astype(BF16)
        br = _pad_lanes(jnp.concatenate([b_group[layer], b_router[layer]])[None, :])

        x1, h2, route, counts_f = _mixer(
            x, mod, win, wp, pool_scale[layer][None, :], conv_w[layer], conv_b[layer][None, :],
            _pad_lanes(dt_bias[layer][None, :]), _pad_lanes(a_log[layer][None, :], fill=NEG_BIG),
            jnp.repeat(d_skip[layer], SSD_HEAD_DIM)[None, :], ssd_norm_w[layer][None, :],
            w_out[layer].astype(BF16), wrh, wrl, br, pool_w=pool_w, ssd_w=ssd_w)

        start_f = jnp.cumsum(counts_f, axis=1) - counts_f
        start_rep = jnp.broadcast_to(start_f.reshape(LANES, 1), (LANES, LANES))
        dest = _rank(route, start_rep)
        counts = counts_f[0, :N_EXPERTS].astype(jnp.int32)
        items = _work_items(counts, (t * TOP_K) // EXPERT_BLOCK)

        nrow = d // LANES
        xs = _dispatch(dest, h2, nrow)
        ys = _experts(*items, xs, w13[layer], w2[layer], nrow)
        x = _final(dest, x1, route, mod, final_norm_w[None, :], ys)
    return x
```

```python
import functools

import jax
import jax.numpy as jnp
from jax import lax
from jax.experimental import pallas as pl
from jax.experimental.pallas import tpu as pltpu

POOL_WINDOWS = (2, 4, 8, 16)
POOL_GROUP_DIM = 64
SSD_HEAD_DIM = 64
SSD_GROUPS = 4
SSD_HEADS_PER_GROUP = 3
SSD_HEADS = SSD_GROUPS * SSD_HEADS_PER_GROUP
SSD_STATE = 128
SSD_CONV = 4
SSD_CHUNK = 128
N_EXPERT_GROUPS = 4
EXPERTS_PER_GROUP = 8
N_EXPERTS = N_EXPERT_GROUPS * EXPERTS_PER_GROUP
TOP_K = 2
NORM_EPS = 1e-6

LANES = 128
SUBLANES = 8
VMEM_LIMIT_BYTES = 56 * 1024 * 1024

SEQ_TILE = 256
MIXER_SEQS = 2
PROJ_PIECE = 1024
NORM_ROWS = 32
MIXER_ORDER = (0, 1, 0, 0, 0,
               1, 0, 0, 0,
               1, 0,
               1, 0,
               0, 1, 1, 1,
               0, 1,
               0, 0, 1,
               1, 1, 1, 1)
CONV_PAD = SUBLANES
CONV_PITCH = SUBLANES + 1
POOL_HALO = 16
ROW_COPY_TILE = 512
EXPERT_BLOCK = 512
EXPERT_CHAINS = 2
DISPATCH_PARTS = 2
DMA_THREADS = 2

NEG_BIG = -1e30
F32 = jnp.float32
BF16 = jnp.bfloat16


def _silu(v):
    half = 0.5 * v
    return half + half * jnp.tanh(half)


def _softplus(v):
    return jnp.maximum(v, 0.0) + jnp.log(1.0 + jnp.exp(-jnp.abs(v)))


def _rms(v):
    return v * lax.rsqrt(jnp.mean(v * v, axis=-1, keepdims=True) + NORM_EPS)


def _dot(a, b):
    return jnp.dot(a, b, preferred_element_type=F32)


def _dot_exact(a, b):
    return jnp.dot(a, b, preferred_element_type=F32, precision=lax.Precision.HIGHEST)


def _run_skewed(chains):
    live = [False] * len(chains)
    started = 0
    while started < len(chains) or any(live):
        if started < len(chains):
            live[started] = True
            started += 1
        for k, chain in enumerate(chains):
            if live[k]:
                try:
                    next(chain)
                except StopIteration:
                    live[k] = False


def _run_ordered(chains, order):
    for k in order:
        next(chains[k], None)
    for chain in chains:
        assert next(chain, "done") == "done", "order does not cover every phase"


def _slab_chunks(ref, rows, nrow):
    return [ref[pl.ds(c, rows, stride=nrow), :] for c in range(nrow)]


def _slab_store(ref, val, nrow):
    rows = val.shape[0]
    for c in range(nrow):
        ref[pl.ds(c, rows, stride=nrow), :] = val[:, c * LANES:(c + 1) * LANES]


def _slab_of_row(ref, row, nrow):
    start = row * nrow
    return ref.at[pl.ds(start if isinstance(start, int) else pl.multiple_of(start, nrow), nrow)]


def _ada_kernel(c_ref, w_ref, b_ref, o_ref):
    o_ref[...] = _dot_exact(_silu(c_ref[...]), w_ref[...]) + b_ref[...]


def _ada(c, w_ada, b_ada):
    b, d = c.shape
    n = w_ada.shape[1]
    return pl.pallas_call(
        _ada_kernel,
        grid=(n // d,),
        in_specs=[pl.BlockSpec((b, d), lambda i: (0, 0)),
                  pl.BlockSpec((d, d), lambda i: (0, i)),
                  pl.BlockSpec((1, d), lambda i: (0, i))],
        out_specs=pl.BlockSpec((b, d), lambda i: (0, i)),
        out_shape=jax.ShapeDtypeStruct((b, n), F32),
        compiler_params=pltpu.CompilerParams(dimension_semantics=("arbitrary",),
                                             vmem_limit_bytes=VMEM_LIMIT_BYTES),
        name="ada",
    )(c, w_ada, b_ada.reshape(1, n))


def _mixer_kernel(x_ref, mod_ref, win_ref, wp_ref, ps_ref, cw_ref, cb_ref, dtb_ref, alog_ref,
                  dsk_ref, nw_ref, wout_ref, wrt_ref, br_ref, rep_ref, tril3_ref,
                  x1_ref, h2_ref, route_ref, rt_ref, cnt_ref,
                  state_ref, xext_ref, uext_ref, xa_ref, hb_ref, *, pool_w, ssd_w):
    @pl.when((pl.program_id(0) == 0) & (pl.program_id(1) == 0))
    def _():
        cnt_ref[...] = jnp.zeros_like(cnt_ref)
        xext_ref[...] = jnp.zeros_like(xext_ref)

    @pl.when(pl.program_id(1) == 0)
    def _():
        state_ref[...] = jnp.zeros_like(state_ref)
        pc = x_ref.shape[1] // SUBLANES
        for bi in range(xext_ref.shape[0]):
            for lt in range(xext_ref.shape[1]):
                xext_ref[bi, lt, pl.ds(CONV_PAD * CONV_PITCH, pc, stride=CONV_PITCH), :] = (
                    jnp.zeros((pc, LANES), F32))
        uext_ref[:, 0:POOL_HALO, :] = jnp.zeros((uext_ref.shape[0], POOL_HALO, uext_ref.shape[2]), F32)

    _run_ordered([_mixer_tile(x_ref.at[bi], mod_ref.at[bi], win_ref, wp_ref, ps_ref, cw_ref, cb_ref, dtb_ref,
                             alog_ref, dsk_ref, nw_ref, wout_ref, wrt_ref, br_ref, rep_ref, tril3_ref,
                             x1_ref.at[bi], h2_ref.at[bi], route_ref.at[bi], rt_ref.at[bi], cnt_ref,
                             state_ref.at[bi], xext_ref.at[bi], uext_ref.at[bi], xa_ref.at[bi], hb_ref.at[bi], pool_w=pool_w, ssd_w=ssd_w)
                  for bi in range(x_ref.shape[0])], MIXER_ORDER)


def _mixer_tile(x_ref, mod_ref, win_ref, wp_ref, ps_ref, cw_ref, cb_ref, dtb_ref, alog_ref,
                dsk_ref, nw_ref, wout_ref, wrt_ref, br_ref, rep_ref, tril3_ref,
                x1_ref, h2_ref, route_ref, rt_ref, cnt_ref,
                state_ref, xext_ref, uext_ref, xa_ref, hb_ref, *, pool_w, ssd_w):
    ts = x_ref.shape[0]
    d = x_ref.shape[1]
    L = SSD_CHUNK
    N = SSD_STATE
    P = SSD_HEAD_DIM
    j = pl.program_id(1)

    mod = mod_ref[...]
    sh1, sc1, g1 = mod[0:1], mod[1:2], mod[2:3]
    sh2, sc2 = mod[3:4], mod[4:5]

    for r in range(0, ts, NORM_ROWS):
        hb_ref[r:r + NORM_ROWS, :] = (_rms(x_ref[r:r + NORM_ROWS, :]) * (1.0 + sc1) + sh1).astype(BF16)
    h = hb_ref[...]
    yield
    n_proj = win_ref.shape[1]
    edges = [min(n_proj, e * PROJ_PIECE) for e in range(pl.cdiv(n_proj, PROJ_PIECE) + 1)]
    pieces = []
    for a, b in zip(edges[:-1], edges[1:]):
        pieces.append(_dot(h, win_ref[:, a:b]))
        yield
    proj = jnp.concatenate(pieces, axis=1)
    o_z = pool_w
    o_x = pool_w + ssd_w
    conv_dim = ssd_w + 2 * SSD_GROUPS * N
    o_dt = o_x + conv_dim
    u = proj[:, 0:pool_w]
    z = proj[:, o_z:o_x]
    xbc = proj[:, o_x:o_dt]
    dtr = proj[:, o_dt:o_dt + LANES]

    uext_ref[POOL_HALO:POOL_HALO + ts, :] = u
    lane_u = lax.broadcasted_iota(jnp.int32, (1, LANES), 1)
    upper = lane_u >= POOL_GROUP_DIM

    def ush(k, col):
        return uext_ref[POOL_HALO - k:POOL_HALO - k + ts, col * LANES:(col + 1) * LANES]

    w0, w1, w2, w3 = POOL_WINDOWS
    lo_a = ush(0, 0)
    for k in range(1, w0):
        lo_a = lo_a + ush(k, 0)
    lo_b = ush(w0, 0)
    for k in range(w0 + 1, w1):
        lo_b = lo_b + ush(k, 0)
    hi_a = ush(0, 1)
    for k in range(1, w2):
        hi_a = hi_a + ush(k, 1)
    hi_b = ush(w2, 1)
    for k in range(w2 + 1, w3):
        hi_b = hi_b + ush(k, 1)
    win_lo = lo_a + jnp.where(upper, lo_b, 0.0)
    win_hi = hi_a + jnp.where(upper, hi_b, 0.0)
    pos = (j * ts + 1 + lax.broadcasted_iota(jnp.int32, (ts, 1), 0)).astype(F32)
    cnt_lo = jnp.minimum(pos, jnp.where(upper, float(w1), float(w0)))
    cnt_hi = jnp.minimum(pos, jnp.where(upper, float(w3), float(w2)))
    pdiff = jnp.concatenate([win_lo / cnt_lo, win_hi / cnt_hi], axis=1) - u
    y_pool = _dot(pdiff.astype(BF16), wp_ref[...]) * ps_ref[...]
    uext_ref[0:POOL_HALO, :] = uext_ref[ts:ts + POOL_HALO, :]

    a_neg = -jnp.exp(alog_ref[...])
    dt_all = _softplus(dtr + dtb_ref[...])
    da = dt_all * a_neg
    d1 = da.astype(BF16)
    r1 = da - d1.astype(F32)
    d2 = r1.astype(BF16)
    d3 = (r1 - d2.astype(F32)).astype(BF16)
    a_cs = _dot(tril3_ref[...], jnp.concatenate([d1, d2, d3], axis=0))
    a_hi = a_cs.astype(BF16)
    a_lo = (a_cs - a_hi.astype(F32)).astype(BF16)
    a_q_all = a_hi.astype(F32) + a_lo.astype(F32)
    a_rep_all = _dot(jnp.concatenate([a_hi, a_lo], axis=1), rep_ref[...])
    yield

    pc = ts // SUBLANES
    n_lt = xext_ref.shape[0]
    for lt in range(n_lt):
        for col in range(SUBLANES):
            xext_ref[lt, pl.ds(CONV_PAD * CONV_PITCH + col + 1, pc, stride=CONV_PITCH), :] = (
                xbc[col * pc:(col + 1) * pc, lt * LANES:(lt + 1) * LANES])
    yield
    sub_i = lax.broadcasted_iota(jnp.int32, (SUBLANES, 1), 0)
    for lt in range(n_lt):
        lanes = slice(lt * LANES, (lt + 1) * LANES)
        cw = cw_ref[:, lanes]
        for col in range(SUBLANES):
            acc = cb_ref[:, lanes] + cw[SSD_CONV - 1:SSD_CONV] * xbc[col * pc:(col + 1) * pc, lanes]
            for k in range(1, SSD_CONV):
                main = xext_ref[lt, pl.ds((CONV_PAD - k) * CONV_PITCH + col + 1, pc, stride=CONV_PITCH), :]
                prev = xext_ref[lt, pl.ds((CONV_PAD + pc - k) * CONV_PITCH + col, SUBLANES,
                                          stride=CONV_PITCH), :]
                tap = jnp.concatenate([jnp.where(sub_i < k, prev, main[0:SUBLANES]), main[SUBLANES:]], axis=0)
                acc = acc + cw[SSD_CONV - 1 - k:SSD_CONV - k] * tap
            xa_ref[col * pc:(col + 1) * pc, lanes] = _silu(acc)
    for lt in range(n_lt):
        xext_ref[lt, pl.ds(CONV_PAD * CONV_PITCH, pc, stride=CONV_PITCH), :] = (
            xext_ref[lt, pl.ds(CONV_PAD * CONV_PITCH + SUBLANES, pc, stride=CONV_PITCH), :])
    yield

    row_i = lax.broadcasted_iota(jnp.int32, (L, L), 0)
    col_i = lax.broadcasted_iota(jnp.int32, (L, L), 1)
    causal = row_i >= col_i
    hpt = LANES // P
    lane_h = lax.broadcasted_iota(jnp.int32, (1, LANES), 1)
    y_chunks = []
    for c in range(ts // L):
        r0 = c * L
        xs_c = xa_ref[r0:r0 + L,0:ssd_w]
        b_c = xa_ref[r0:r0 + L,ssd_w:ssd_w + SSD_GROUPS * N]
        c_c = xa_ref[r0:r0 + L,ssd_w + SSD_GROUPS * N:conv_dim]
        a_rep = a_rep_all[r0:r0 + L]
        a_q_t = a_q_all[r0:r0 + L].T
        dt_t = dt_all[r0:r0 + L].T
        cbs, cgs, bg_ts = [], [], []
        for g in range(SSD_GROUPS):
            bg = b_c[:, g * N:(g + 1) * N]
            cg = c_c[:, g * N:(g + 1) * N]
            cbs.append(lax.dot_general(cg.astype(BF16), bg.astype(BF16), (((1,), (1,)), ((), ())),
                                       preferred_element_type=F32))
            cgs.append(cg)
            bg_ts.append(bg.T)
        y_tiles = []
        for q in range(SSD_HEADS // hpt):
            xs_q = xs_c[:, q * LANES:(q + 1) * LANES].astype(BF16)
            st = state_ref[q]
            rhs = jnp.concatenate([xs_q, st.astype(BF16)], axis=0)
            y_q = upd_q = keep_q = None
            for hh in range(hpt):
                hd = q * hpt + hh
                g = hd // SSD_HEADS_PER_GROUP
                a_col = a_rep[:, hd * LANES:(hd + 1) * LANES]
                a_row = a_q_t[hd:hd + 1, :]
                dt_row = dt_t[hd:hd + 1, :]
                decay = jnp.exp(jnp.where(causal, a_col - a_row, NEG_BIG))
                m = cbs[g] * decay * dt_row
                cs = cgs[g] * jnp.exp(a_col)
                lhs = jnp.concatenate([m, cs], axis=1).astype(BF16)
                y_h = _dot(lhs, rhs)
                a_end = a_row[:, L - 1:L]
                w_row = dt_row * jnp.exp(a_end - a_row)
                upd_h = _dot((bg_ts[g] * w_row).astype(BF16), xs_q)
                keep_h = jnp.exp(a_end)
                if hh == 0:
                    y_q, upd_q, keep_q = y_h, upd_h, keep_h
                else:
                    mine = (lane_h >= hh * P) & (lane_h < (hh + 1) * P)
                    y_q = jnp.where(mine, y_h, y_q)
                    upd_q = jnp.where(mine, upd_h, upd_q)
                    keep_q = jnp.where(mine, keep_h, keep_q)
            y_tiles.append(y_q)
            state_ref[q] = keep_q * st + upd_q
        y_chunks.append(jnp.concatenate(y_tiles, axis=1))
        yield
    y_all = jnp.concatenate(y_chunks, axis=0) if len(y_chunks) > 1 else y_chunks[0]

    gw = ssd_w // SSD_GROUPS
    lane_y = lax.broadcasted_iota(jnp.int32, (1, ssd_w), 1)
    for r in range(0, ts, NORM_ROWS):
        rows = slice(r, r + NORM_ROWS)
        y = (y_all[rows] + xa_ref[rows, 0:ssd_w] * dsk_ref[...]) * _silu(z[rows])
        y2 = y * y
        scale = jnp.zeros_like(y)
        for g in range(SSD_GROUPS):
            in_g = (lane_y >= g * gw) & (lane_y < (g + 1) * gw)
            ms = jnp.sum(jnp.where(in_g, y2, 0.0), axis=-1, keepdims=True) * (1.0 / gw)
            scale = scale + jnp.where(in_g, lax.rsqrt(ms + NORM_EPS), 0.0)
        hb_ref[rows, :] = jnp.concatenate([y_pool[rows], y * scale * nw_ref[...]], axis=1).astype(BF16)
    yield
    mix = _dot(hb_ref[...], wout_ref[...])
    yield

    nrow = d // LANES
    for r in range(0, ts, NORM_ROWS):
        rows = slice(r, r + NORM_ROWS)
        x1 = x_ref[rows, :] + g1 * mix[rows]
        x1_ref[rows, :] = x1
        h2 = _rms(x1) * (1.0 + sc2) + sh2
        for c in range(nrow):
            h2_ref[pl.ds(r * nrow + c, NORM_ROWS, stride=nrow), :] = h2[:, c * LANES:(c + 1) * LANES]
        hb_ref[rows, :] = h2.astype(BF16)
    h_hi = hb_ref[...]
    yield
    logits = lax.dot_general(wrt_ref[...], h_hi, (((1,), (1,)), ((), ())), preferred_element_type=F32)
    logits = logits + jnp.concatenate([br_ref[...]] * (ts // LANES), axis=1)
    row = lax.broadcasted_iota(jnp.int32, (LANES, ts), 0).astype(F32)
    big = float(LANES)
    gl = jnp.where(row < N_EXPERT_GROUPS, logits, NEG_BIG)
    gmax = jnp.max(gl, axis=0, keepdims=True)
    gsum = jnp.sum(jnp.exp(gl - gmax), axis=0, keepdims=True)
    p_g = 1.0 / gsum
    g_idx = jnp.min(jnp.where(gl == gmax, row, big), axis=0, keepdims=True)
    lo = N_EXPERT_GROUPS + EXPERTS_PER_GROUP * g_idx
    el = jnp.where((row >= lo) & (row < lo + EXPERTS_PER_GROUP), logits, NEG_BIG)
    v1 = jnp.max(el, axis=0, keepdims=True)
    i1 = jnp.min(jnp.where(el == v1, row, big), axis=0, keepdims=True)
    el2 = jnp.where(row == i1, NEG_BIG, el)
    v2 = jnp.max(el2, axis=0, keepdims=True)
    i2 = jnp.min(jnp.where(el2 == v2, row, big), axis=0, keepdims=True)
    e21 = jnp.exp(v2 - v1)
    gate1 = p_g / (1.0 + e21)
    gate2 = p_g * e21 / (1.0 + e21)
    e1 = i1 - N_EXPERT_GROUPS
    e2 = i2 - N_EXPERT_GROUPS
    route_t = jnp.where(row == 0, e1, jnp.where(row == 1, e2,
                                                jnp.where(row == 2, gate1, jnp.where(row == 3, gate2, 0.0))))
    rt_ref[...] = route_t[0:SUBLANES]
    route_ref[...] = route_t.T
    cnt_ref[...] += jnp.where((row == e1) | (row == e2), 1.0, 0.0)


def _mixer(x, mod, win, wp, ps, cw, cb, dtb, alog, dsk, nw, wout, wrt, br, *, pool_w, ssd_w):
    k_idx = jnp.arange(2 * LANES, dtype=jnp.int32)[:, None] % LANES
    h_idx = jnp.arange(SSD_HEADS * LANES, dtype=jnp.int32)[None, :] // LANES
    rep = (k_idx == h_idx).astype(BF16)
    r_idx = jnp.arange(SEQ_TILE, dtype=jnp.int32)[:, None]
    c_idx = jnp.arange(3 * SEQ_TILE, dtype=jnp.int32)[None, :] % SEQ_TILE
    tril3 = ((r_idx >= c_idx) & (r_idx // SSD_CHUNK == c_idx // SSD_CHUNK)).astype(BF16)
    b, s, d = x.shape
    ts = SEQ_TILE
    conv_dim = cw.shape[1]
    nrow = d // LANES

    def full(a):
        nd = a.ndim
        return pl.BlockSpec(a.shape, lambda i, j, _nd=nd: (0,) * _nd)

    nb = MIXER_SEQS
    seq_tile = lambda i, j: (i, j, 0)
    x1, h2, route, route_t, counts = pl.pallas_call(
        functools.partial(_mixer_kernel, pool_w=pool_w, ssd_w=ssd_w),
        grid=(b // nb, s // ts),
        in_specs=[pl.BlockSpec((nb, ts, d), seq_tile),
                  pl.BlockSpec((nb, 6, d), lambda i, j: (i, 0, 0)),
                  full(win), full(wp), full(ps), full(cw), full(cb), full(dtb), full(alog),
                  full(dsk), full(nw), full(wout), full(wrt), full(br), full(rep), full(tril3)],
        out_specs=[pl.BlockSpec((nb, ts, d), seq_tile),
                   pl.BlockSpec((nb, ts * nrow, LANES), seq_tile),
                   pl.BlockSpec((nb, ts, LANES), seq_tile),
                   pl.BlockSpec((nb, SUBLANES, ts), lambda i, j: (i, 0, j)),
                   pl.BlockSpec((LANES, ts), lambda i, j: (0, 0))],
        out_shape=[jax.ShapeDtypeStruct((b, s, d), F32),
                   jax.ShapeDtypeStruct((b, s * nrow, LANES), F32),
                   jax.ShapeDtypeStruct((b, s, LANES), F32),
                   jax.ShapeDtypeStruct((b, SUBLANES, s), F32),
                   jax.ShapeDtypeStruct((LANES, ts), F32)],
        scratch_shapes=[pltpu.VMEM((nb, ssd_w // LANES, SSD_STATE, LANES), F32),
                        pltpu.VMEM((nb, conv_dim // LANES, (ts // SUBLANES + 2 * CONV_PAD) * CONV_PITCH, LANES),
                                   F32),
                        pltpu.VMEM((nb, POOL_HALO + ts, pool_w), F32),
                        pltpu.VMEM((nb, ts, conv_dim), F32),
                        pltpu.VMEM((nb, ts, d), BF16)],
        compiler_params=pltpu.CompilerParams(dimension_semantics=("arbitrary", "arbitrary"),
                                             vmem_limit_bytes=VMEM_LIMIT_BYTES),
        name="mixer",
    )(x, mod, win, wp, ps, cw, cb, dtb, alog, dsk, nw, wout, wrt, br, rep, tril3)
    return x1, h2.reshape(b * s * nrow, LANES), route.reshape(b * s, LANES), route_t, counts


def _rank_kernel(route_ref, start_ref, dest_ref, carry_ref):
    ta = route_ref.shape[2]

    @pl.when(pl.program_id(0) == 0)
    def _():
        carry_ref[...] = start_ref[...]

    route_t = route_ref[0]
    e_idx = lax.broadcasted_iota(jnp.int32, (LANES, ta), 0).astype(F32)
    ohs = [jnp.where(e_idx == route_t[k:k + 1, :], 1.0, 0.0) for k in range(TOP_K)]
    oh = (ohs[0] + ohs[1]).astype(BF16)
    earlier = (lax.broadcasted_iota(jnp.int32, (ta, ta), 0)
               < lax.broadcasted_iota(jnp.int32, (ta, ta), 1))
    before = _dot(oh, jnp.where(earlier, 1.0, 0.0).astype(BF16))
    carry = carry_ref[...]
    base = jnp.concatenate([carry] * (ta // LANES), axis=1) + before
    for k in range(TOP_K):
        dest_ref[0, k:k + 1, :] = jnp.sum(ohs[k] * base, axis=0, keepdims=True).astype(jnp.int32)
    carry_ref[...] = carry + _dot(oh, jnp.ones((ta, LANES), BF16))


def _rank(route_t, start_rep):
    b, _, s = route_t.shape
    spt = s // ROW_COPY_TILE
    t = b * s
    return pl.pallas_call(
        _rank_kernel,
        grid=(t // ROW_COPY_TILE,),
        in_specs=[pl.BlockSpec((1, SUBLANES, ROW_COPY_TILE), lambda i: (i // spt, 0, i % spt)),
                  pl.BlockSpec((LANES, LANES), lambda i: (0, 0))],
        out_specs=pl.BlockSpec((1, TOP_K, ROW_COPY_TILE), lambda i: (i, 0, 0)),
        out_shape=jax.ShapeDtypeStruct((t // ROW_COPY_TILE, TOP_K, ROW_COPY_TILE), jnp.int32),
        scratch_shapes=[pltpu.VMEM((LANES, LANES), F32)],
        compiler_params=pltpu.CompilerParams(dimension_semantics=("arbitrary",)),
        name="rank",
    )(route_t, start_rep)


def _dispatch_kernel(dest_ref, h2_ref, xs_hbm, sem, *, nrow):
    tile_rows = h2_ref.shape[0]

    part_tokens = dest_ref.shape[2]
    for part in range(dest_ref.shape[0]):
        for tk in range(part_tokens):
            for k in range(TOP_K):
                pltpu.make_async_copy(_slab_of_row(h2_ref, part * part_tokens + tk, nrow),
                                      _slab_of_row(xs_hbm, dest_ref[part, k, tk], nrow),
                                      sem).start(priority=(tk * TOP_K + k) % DMA_THREADS)

    for _ in range(TOP_K):
        pltpu.make_async_copy(h2_ref, xs_hbm.at[pl.ds(0, tile_rows)], sem).wait()


def _dispatch(dest, h2, nrow):
    t = h2.shape[0] // nrow
    return pl.pallas_call(
        functools.partial(_dispatch_kernel, nrow=nrow),
        grid=(t // (DISPATCH_PARTS * ROW_COPY_TILE),),
        in_specs=[pl.BlockSpec((DISPATCH_PARTS, TOP_K, ROW_COPY_TILE), lambda i: (i, 0, 0),
                               memory_space=pltpu.SMEM),
                  pl.BlockSpec((DISPATCH_PARTS * ROW_COPY_TILE * nrow, LANES), lambda i: (i, 0))],
        out_specs=pl.BlockSpec(memory_space=pl.ANY),
        out_shape=jax.ShapeDtypeStruct((t * TOP_K * nrow, LANES), F32),
        scratch_shapes=[pltpu.SemaphoreType.DMA(())],
        compiler_params=pltpu.CompilerParams(dimension_semantics=("arbitrary",),
                                             has_side_effects=True),
        name="dispatch",
    )(dest, h2)


def _expert_kernel(blk_ref, exp_ref, lo_ref, hi_ref, first_ref, new_ref, xs_ref, w13_ref, w2_ref, ys_ref,
                   w13_bf, w2_bf, *, nrow):
    i = pl.program_id(0)
    rows = xs_ref.shape[0] // nrow
    lo = lo_ref[i]
    hi = hi_ref[i]

    sub = rows // EXPERT_CHAINS

    @pl.when(new_ref[i] == 1)
    def _():
        w13_bf[...] = w13_ref[0].astype(BF16)
        w2_bf[...] = w2_ref[0].astype(BF16)

    def chain(r0):
        xb = jnp.concatenate([xs_ref[pl.ds(r0 * nrow + c, sub, stride=nrow), :].astype(BF16)
                              for c in range(nrow)], axis=1)
        yield
        hu = _dot(xb, w13_bf[...])
        yield
        f = hu.shape[1] // 2
        act = (_silu(hu[:, :f]) * hu[:, f:]).astype(BF16)
        yield
        y = _dot(act, w2_bf[...])
        yield
        ridx = r0 + lax.broadcasted_iota(jnp.int32, (sub, 1), 0)
        take = (ridx >= lo_eff) & (ridx < hi_eff)
        for c in range(nrow):
            sl = pl.ds(r0 * nrow + c, sub, stride=nrow)
            ys_ref[sl, :] = jnp.where(take, y[:, c * LANES:(c + 1) * LANES], ys_ref[sl, :])

    is_first = first_ref[i] == 1
    lo_eff = jnp.where(is_first, 0, lo)
    hi_eff = jnp.where(is_first, rows, hi)

    @pl.when(hi > lo)
    def _():
        _run_skewed([chain(k * sub) for k in range(EXPERT_CHAINS)])


def _experts(item_blk, item_exp, item_lo, item_hi, item_first, item_new, xs, w13, w2, nrow):
    a = xs.shape[0] // nrow
    n_items = item_blk.shape[0]
    d = w13.shape[1]
    ff2 = w13.shape[2]
    grid_spec = pltpu.PrefetchScalarGridSpec(
        num_scalar_prefetch=6,
        grid=(n_items,),
        in_specs=[pl.BlockSpec((EXPERT_BLOCK * nrow, LANES), lambda i, b, e, lo, hi, fr, nw: (b[i], 0)),
                  pl.BlockSpec((1, d, ff2), lambda i, b, e, lo, hi, fr, nw: (e[i], 0, 0)),
                  pl.BlockSpec((1, ff2 // 2, d), lambda i, b, e, lo, hi, fr, nw: (e[i], 0, 0))],
        out_specs=pl.BlockSpec((EXPERT_BLOCK * nrow, LANES), lambda i, b, e, lo, hi, fr, nw: (b[i], 0)),
        scratch_shapes=[pltpu.VMEM((d, ff2), BF16), pltpu.VMEM((ff2 // 2, d), BF16)],
    )
    return pl.pallas_call(
        functools.partial(_expert_kernel, nrow=nrow),
        grid_spec=grid_spec,
        out_shape=jax.ShapeDtypeStruct((a * nrow, LANES), F32),
        compiler_params=pltpu.CompilerParams(dimension_semantics=("arbitrary",),
                                             vmem_limit_bytes=VMEM_LIMIT_BYTES),
        name="experts",
    )(item_blk, item_exp, item_lo, item_hi, item_first, item_new, xs, w13, w2)


def _final_kernel(dest_ref, dnext_ref, x1_ref, route_ref, mod_ref, fw_ref, ys_hbm, o_ref,
                  buf_even, buf_odd, sems):
    tf = x1_ref.shape[1]
    nrow = buf_even.shape[1] // tf
    step = pl.program_id(0)
    last = step + 1 == pl.num_programs(0)

    def start_gathers(d_ref, buf, sem, t0, t1):
        for tk in range(t0, t1):
            for k in range(TOP_K):
                pltpu.make_async_copy(_slab_of_row(ys_hbm, d_ref[0, k, tk], nrow),
                                      _slab_of_row(buf.at[k], tk, nrow),
                                      sem).start(priority=(tk * TOP_K + k) % DMA_THREADS)

    def wait_gathers(buf, sem):
        for k in range(TOP_K):
            pltpu.make_async_copy(ys_hbm.at[pl.ds(0, tf * nrow)], buf.at[k], sem).wait()

    @pl.when(step == 0)
    def _():
        start_gathers(dest_ref, buf_even, sems.at[0], 0, tf)

    def body(cur, cur_sem, nxt, nxt_sem):
        wait_gathers(cur, cur_sem)
        g2 = mod_ref[0][5:6]
        for r0 in range(0, tf, NORM_ROWS):
            start_gathers(dnext_ref, nxt, nxt_sem, r0, r0 + NORM_ROWS)
            rows = slice(r0, r0 + NORM_ROWS)
            gates = route_ref[rows, :]
            y = jnp.concatenate(
                [cur[0, pl.ds(r0 * nrow + c, NORM_ROWS, stride=nrow), :] * gates[:, 2:3]
                 + cur[1, pl.ds(r0 * nrow + c, NORM_ROWS, stride=nrow), :] * gates[:, 3:4]
                 for c in range(nrow)], axis=1)
            o_ref[0, rows, :] = _rms(x1_ref[0, rows, :] + g2 * y) * fw_ref[...]

        @pl.when(last)
        def _():
            wait_gathers(nxt, nxt_sem)

    @pl.when(lax.rem(step, 2) == 0)
    def _():
        body(buf_even, sems.at[0], buf_odd, sems.at[1])

    @pl.when(lax.rem(step, 2) == 1)
    def _():
        body(buf_odd, sems.at[1], buf_even, sems.at[0])


def _final(dest, x1, route, mod, fw, ys):
    b, s, d = x1.shape
    tf = ROW_COPY_TILE
    nrow = d // LANES
    spt = s // tf
    n_steps = b * spt
    return pl.pallas_call(
        _final_kernel,
        grid=(n_steps,),
        in_specs=[pl.BlockSpec((1, TOP_K, tf), lambda i: (i, 0, 0), memory_space=pltpu.SMEM),
                  pl.BlockSpec((1, TOP_K, tf), lambda i: (jnp.minimum(i + 1, n_steps - 1), 0, 0),
                               memory_space=pltpu.SMEM),
                  pl.BlockSpec((1, tf, d), lambda i: (i // spt, i % spt, 0)),
                  pl.BlockSpec((tf, LANES), lambda i: (i, 0)),
                  pl.BlockSpec((1, 6, d), lambda i: (i // spt, 0, 0)),
                  pl.BlockSpec((1, d), lambda i: (0, 0)),
                  pl.BlockSpec(memory_space=pl.ANY)],
        out_specs=pl.BlockSpec((1, tf, d), lambda i: (i // spt, i % spt, 0)),
        out_shape=jax.ShapeDtypeStruct((b, s, d), F32),
        scratch_shapes=[pltpu.VMEM((TOP_K, tf * nrow, LANES), F32),
                        pltpu.VMEM((TOP_K, tf * nrow, LANES), F32),
                        pltpu.SemaphoreType.DMA((2,))],
        compiler_params=pltpu.CompilerParams(dimension_semantics=("arbitrary",),
                                             vmem_limit_bytes=VMEM_LIMIT_BYTES),
        name="final",
    )(dest, dest, x1, route, mod, fw, ys)


def _pad_lanes(v, fill=0.0):
    n = v.shape[-1]
    return jnp.pad(v, [(0, 0)] * (v.ndim - 1) + [(0, LANES - n)], constant_values=fill)


def _work_items(counts, n_blocks):
    n_items = n_blocks + N_EXPERTS - 1
    ends = jnp.cumsum(counts)
    starts = ends - counts
    first_blk = starts // EXPERT_BLOCK
    last_blk = jnp.maximum(ends - 1, starts) // EXPERT_BLOCK
    n_e = jnp.where(counts > 0, last_blk - first_blk + 1, 0)
    item_end = jnp.cumsum(n_e)
    item_start = item_end - n_e
    ids = jnp.arange(n_items, dtype=jnp.int32)
    total = item_end[-1]
    ids_c = jnp.minimum(ids, total - 1)
    e = jnp.sum((item_end[None, :] <= ids_c[:, None]).astype(jnp.int32), axis=1)
    onehot = (e[:, None] == jnp.arange(N_EXPERTS, dtype=jnp.int32)[None, :]).astype(jnp.int32)
    pick = lambda v: jnp.sum(onehot * v[None, :], axis=1)
    blk = pick(first_blk) + ids_c - pick(item_start)
    lo = jnp.clip(pick(starts) - blk * EXPERT_BLOCK, 0, EXPERT_BLOCK)
    hi = jnp.clip(pick(ends) - blk * EXPERT_BLOCK, 0, EXPERT_BLOCK)
    hi = jnp.where(ids < total, hi, lo)
    prev_blk = jnp.concatenate([jnp.full((1,), -1, jnp.int32), blk[:-1]])
    first = (blk != prev_blk).astype(jnp.int32)
    prev_e = jnp.concatenate([jnp.full((1,), -1, jnp.int32), e[:-1]])
    new_expert = (e != prev_e).astype(jnp.int32)
    return blk, e, lo, hi, first, new_expert


def kernel(x, c, w_ada, b_ada, w_in, w_pool, pool_scale, conv_w, conv_b, dt_bias, a_log, d_skip,
           ssd_norm_w, w_out, w_group, b_group, w_router, b_router, w13, w2, final_norm_w):
    b, s, d = x.shape
    depth = w_ada.shape[0]
    t = b * s
    pool_w = w_pool.shape[1] * w_pool.shape[2]
    ssd_w = SSD_HEADS * SSD_HEAD_DIM
    conv_dim = conv_w.shape[2]
    assert s % SEQ_TILE == 0 and SEQ_TILE % SSD_CHUNK == 0 and b % MIXER_SEQS == 0
    assert s % ROW_COPY_TILE == 0 and ROW_COPY_TILE % LANES == 0
    assert (t * TOP_K) % EXPERT_BLOCK == 0 and d % LANES == 0
    assert pool_w == 2 * LANES and len(POOL_WINDOWS) * POOL_GROUP_DIM == pool_w
    assert N_EXPERT_GROUPS + N_EXPERTS <= LANES and SSD_HEADS <= LANES
    assert depth == 1, "the final RMSNorm is fused into the last layer's combine step"

    for layer in range(depth):
        mod = _ada(c, w_ada[layer], b_ada[layer]).reshape(b, 6, d)

        o_dt = pool_w + ssd_w + conv_dim
        win = jnp.pad(w_in[layer].astype(BF16), ((0, 0), (0, o_dt + LANES - w_in.shape[2])))
        wp = jnp.zeros((pool_w, pool_w), F32)
        for g in range(len(POOL_WINDOWS)):
            sl = slice(g * POOL_GROUP_DIM, (g + 1) * POOL_GROUP_DIM)
            wp = wp.at[sl, sl].set(w_pool[layer, g])
        wp = wp.astype(BF16)
        wrt = _pad_lanes(jnp.concatenate([w_group[layer], w_router[layer]], axis=1)).T.astype(BF16)
        b_route = _pad_lanes(jnp.concatenate([b_group[layer], b_router[layer]])[None, :])
        br = jnp.broadcast_to(b_route.reshape(LANES, 1), (LANES, LANES))

        x1, h2, route, route_t, counts_part = _mixer(
            x, mod, win, wp, pool_scale[layer][None, :], conv_w[layer], conv_b[layer][None, :],
            _pad_lanes(dt_bias[layer][None, :]), _pad_lanes(a_log[layer][None, :], fill=NEG_BIG),
            jnp.repeat(d_skip[layer], SSD_HEAD_DIM)[None, :], ssd_norm_w[layer][None, :],
            w_out[layer].astype(BF16), wrt, br, pool_w=pool_w, ssd_w=ssd_w)

        counts_f = jnp.sum(counts_part, axis=1, keepdims=True)
        start_rep = jnp.broadcast_to(jnp.cumsum(counts_f, axis=0) - counts_f, (LANES, LANES))
        dest = _rank(route_t, start_rep)
        counts = counts_f[:N_EXPERTS, 0].astype(jnp.int32)
        items = _work_items(counts, (t * TOP_K) // EXPERT_BLOCK)

        nrow = d // LANES
        xs = _dispatch(dest, h2, nrow)
        ys = _experts(*items, xs, w13[layer], w2[layer], nrow)
        x = _final(dest, x1, route, mod, final_norm_w[None, :], ys)
    return x
```

```python
import functools

import jax
import jax.numpy as jnp
from jax import lax
from jax.experimental import pallas as pl
from jax.experimental.pallas import tpu as pltpu

POOL_WINDOWS = (2, 4, 8, 16)
POOL_GROUP_DIM = 64
SSD_HEAD_DIM = 64
SSD_GROUPS = 4
SSD_HEADS_PER_GROUP = 3
SSD_HEADS = SSD_GROUPS * SSD_HEADS_PER_GROUP
SSD_STATE = 128
SSD_CONV = 4
SSD_CHUNK = 128
N_EXPERT_GROUPS = 4
EXPERTS_PER_GROUP = 8
N_EXPERTS = N_EXPERT_GROUPS * EXPERTS_PER_GROUP
TOP_K = 2
NORM_EPS = 1e-6

LANES = 128
SUBLANES = 8
VMEM_LIMIT_BYTES = 56 * 1024 * 1024

SEQ_TILE = 256
MIXER_SEQS = 2
PROJ_PIECE = 1024
NORM_ROWS = 32
CONV_PAD = SUBLANES
CONV_PITCH = SUBLANES + 1
POOL_HALO = 16
ROW_COPY_TILE = 512
EXPERT_BLOCK = 512
EXPERT_CHAINS = 2
DISPATCH_PARTS = 4
DMA_THREADS = 2

NEG_BIG = -1e30
F32 = jnp.float32
BF16 = jnp.bfloat16


def _silu(v):
    half = 0.5 * v
    return half + half * jnp.tanh(half)


def _softplus(v):
    return jnp.maximum(v, 0.0) + jnp.log(1.0 + jnp.exp(-jnp.abs(v)))


def _rms(v):
    return v * lax.rsqrt(jnp.mean(v * v, axis=-1, keepdims=True) + NORM_EPS)


def _dot(a, b):
    return jnp.dot(a, b, preferred_element_type=F32)


def _dot_exact(a, b):
    return jnp.dot(a, b, preferred_element_type=F32, precision=lax.Precision.HIGHEST)


def _run_skewed(chains):
    live = [False] * len(chains)
    started = 0
    while started < len(chains) or any(live):
        if started < len(chains):
            live[started] = True
            started += 1
        for k, chain in enumerate(chains):
            if live[k]:
                try:
                    next(chain)
                except StopIteration:
                    live[k] = False


def _slab_of_row(ref, row, nrow):
    start = row * nrow
    return ref.at[pl.ds(start if isinstance(start, int) else pl.multiple_of(start, nrow), nrow)]


def _ada_kernel(c_ref, w_ref, b_ref, o_ref):
    o_ref[...] = _dot_exact(_silu(c_ref[...]), w_ref[...]) + b_ref[...]


def _ada(c, w_ada, b_ada):
    b, d = c.shape
    n = w_ada.shape[1]
    return pl.pallas_call(
        _ada_kernel,
        grid=(n // d,),
        in_specs=[pl.BlockSpec((b, d), lambda i: (0, 0)),
                  pl.BlockSpec((d, d), lambda i: (0, i)),
                  pl.BlockSpec((1, d), lambda i: (0, i))],
        out_specs=pl.BlockSpec((b, d), lambda i: (0, i)),
        out_shape=jax.ShapeDtypeStruct((b, n), F32),
        compiler_params=pltpu.CompilerParams(dimension_semantics=("arbitrary",),
                                             vmem_limit_bytes=VMEM_LIMIT_BYTES),
        name="ada",
    )(c, w_ada, b_ada.reshape(1, n))


def _mixer_kernel(x_ref, mod_ref, win_ref, wp_ref, ps_ref, cw_ref, cb_ref, dtb_ref, alog_ref,
                  dsk_ref, nw_ref, wout_ref, wrt_ref, br_ref, rep_ref, tril3_ref,
                  x1_ref, h2_ref, route_ref, rt_ref, cnt_ref,
                  state_ref, xext_ref, uext_ref, xa_ref, hb_ref, *, pool_w, ssd_w):
    @pl.when((pl.program_id(0) == 0) & (pl.program_id(1) == 0))
    def _():
        cnt_ref[...] = jnp.zeros_like(cnt_ref)
        xext_ref[...] = jnp.zeros_like(xext_ref)

    @pl.when(pl.program_id(1) == 0)
    def _():
        state_ref[...] = jnp.zeros_like(state_ref)
        pc = x_ref.shape[1] // SUBLANES
        for bi in range(xext_ref.shape[0]):
            for lt in range(xext_ref.shape[1]):
                xext_ref[bi, lt, pl.ds(CONV_PAD * CONV_PITCH, pc, stride=CONV_PITCH), :] = (
                    jnp.zeros((pc, LANES), F32))
        uext_ref[:, 0:POOL_HALO, :] = jnp.zeros((uext_ref.shape[0], POOL_HALO, uext_ref.shape[2]), F32)

    _run_skewed([_mixer_tile(x_ref.at[bi], mod_ref.at[bi], win_ref, wp_ref, ps_ref, cw_ref, cb_ref, dtb_ref,
                             alog_ref, dsk_ref, nw_ref, wout_ref, wrt_ref, br_ref, rep_ref, tril3_ref,
                             x1_ref.at[bi], h2_ref.at[bi], route_ref.at[bi], rt_ref.at[bi], cnt_ref,
                             state_ref.at[bi], xext_ref.at[bi], uext_ref.at[bi], xa_ref.at[bi], hb_ref.at[bi], pool_w=pool_w, ssd_w=ssd_w)
                 for bi in range(x_ref.shape[0])])


def _mixer_tile(x_ref, mod_ref, win_ref, wp_ref, ps_ref, cw_ref, cb_ref, dtb_ref, alog_ref,
                dsk_ref, nw_ref, wout_ref, wrt_ref, br_ref, rep_ref, tril3_ref,
                x1_ref, h2_ref, route_ref, rt_ref, cnt_ref,
                state_ref, xext_ref, uext_ref, xa_ref, hb_ref, *, pool_w, ssd_w):
    ts = x_ref.shape[0]
    d = x_ref.shape[1]
    L = SSD_CHUNK
    N = SSD_STATE
    P = SSD_HEAD_DIM
    j = pl.program_id(1)

    mod = mod_ref[...]
    sh1, sc1, g1 = mod[0:1], mod[1:2], mod[2:3]
    sh2, sc2 = mod[3:4], mod[4:5]

    for r in range(0, ts, NORM_ROWS):
        hb_ref[r:r + NORM_ROWS, :] = (_rms(x_ref[r:r + NORM_ROWS, :]) * (1.0 + sc1) + sh1).astype(BF16)
    h = hb_ref[...]
    yield
    n_proj = win_ref.shape[1]
    edges = [min(n_proj, e * PROJ_PIECE) for e in range(pl.cdiv(n_proj, PROJ_PIECE) + 1)]
    pieces = []
    for a, b in zip(edges[:-1], edges[1:]):
        pieces.append(_dot(h, win_ref[:, a:b]))
        yield
    proj = jnp.concatenate(pieces, axis=1)
    o_z = pool_w
    o_x = pool_w + ssd_w
    conv_dim = ssd_w + 2 * SSD_GROUPS * N
    o_dt = o_x + conv_dim
    u = proj[:, 0:pool_w]
    z = proj[:, o_z:o_x]
    xbc = proj[:, o_x:o_dt]
    dtr = proj[:, o_dt:o_dt + LANES]

    uext_ref[POOL_HALO:POOL_HALO + ts, :] = u
    lane_u = lax.broadcasted_iota(jnp.int32, (1, LANES), 1)
    upper = lane_u >= POOL_GROUP_DIM

    def ush(k, col):
        return uext_ref[POOL_HALO - k:POOL_HALO - k + ts, col * LANES:(col + 1) * LANES]

    w0, w1, w2, w3 = POOL_WINDOWS
    lo_a = ush(0, 0)
    for k in range(1, w0):
        lo_a = lo_a + ush(k, 0)
    lo_b = ush(w0, 0)
    for k in range(w0 + 1, w1):
        lo_b = lo_b + ush(k, 0)
    hi_a = ush(0, 1)
    for k in range(1, w2):
        hi_a = hi_a + ush(k, 1)
    hi_b = ush(w2, 1)
    for k in range(w2 + 1, w3):
        hi_b = hi_b + ush(k, 1)
    win_lo = lo_a + jnp.where(upper, lo_b, 0.0)
    win_hi = hi_a + jnp.where(upper, hi_b, 0.0)
    pos = (j * ts + 1 + lax.broadcasted_iota(jnp.int32, (ts, 1), 0)).astype(F32)
    cnt_lo = jnp.minimum(pos, jnp.where(upper, float(w1), float(w0)))
    cnt_hi = jnp.minimum(pos, jnp.where(upper, float(w3), float(w2)))
    pdiff = jnp.concatenate([win_lo / cnt_lo, win_hi / cnt_hi], axis=1) - u
    y_pool = _dot(pdiff.astype(BF16), wp_ref[...]) * ps_ref[...]
    uext_ref[0:POOL_HALO, :] = uext_ref[ts:ts + POOL_HALO, :]

    a_neg = -jnp.exp(alog_ref[...])
    dt_all = _softplus(dtr + dtb_ref[...])
    da = dt_all * a_neg
    d1 = da.astype(BF16)
    r1 = da - d1.astype(F32)
    d2 = r1.astype(BF16)
    d3 = (r1 - d2.astype(F32)).astype(BF16)
    a_cs = _dot(tril3_ref[...], jnp.concatenate([d1, d2, d3], axis=0))
    a_hi = a_cs.astype(BF16)
    a_lo = (a_cs - a_hi.astype(F32)).astype(BF16)
    a_q_all = a_hi.astype(F32) + a_lo.astype(F32)
    a_rep_all = _dot(jnp.concatenate([a_hi, a_lo], axis=1), rep_ref[...])
    yield

    pc = ts // SUBLANES
    n_lt = xext_ref.shape[0]
    for lt in range(n_lt):
        for col in range(SUBLANES):
            xext_ref[lt, pl.ds(CONV_PAD * CONV_PITCH + col + 1, pc, stride=CONV_PITCH), :] = (
                xbc[col * pc:(col + 1) * pc, lt * LANES:(lt + 1) * LANES])
    yield
    sub_i = lax.broadcasted_iota(jnp.int32, (SUBLANES, 1), 0)
    for lt in range(n_lt):
        lanes = slice(lt * LANES, (lt + 1) * LANES)
        cw = cw_ref[:, lanes]
        for col in range(SUBLANES):
            acc = cb_ref[:, lanes] + cw[SSD_CONV - 1:SSD_CONV] * xbc[col * pc:(col + 1) * pc, lanes]
            for k in range(1, SSD_CONV):
                main = xext_ref[lt, pl.ds((CONV_PAD - k) * CONV_PITCH + col + 1, pc, stride=CONV_PITCH), :]
                prev = xext_ref[lt, pl.ds((CONV_PAD + pc - k) * CONV_PITCH + col, SUBLANES,
                                          stride=CONV_PITCH), :]
                tap = jnp.concatenate([jnp.where(sub_i < k, prev, main[0:SUBLANES]), main[SUBLANES:]], axis=0)
                acc = acc + cw[SSD_CONV - 1 - k:SSD_CONV - k] * tap
            xa_ref[col * pc:(col + 1) * pc, lanes] = _silu(acc)
    for lt in range(n_lt):
        xext_ref[lt, pl.ds(CONV_PAD * CONV_PITCH, pc, stride=CONV_PITCH), :] = (
            xext_ref[lt, pl.ds(CONV_PAD * CONV_PITCH + SUBLANES, pc, stride=CONV_PITCH), :])
    yield

    row_i = lax.broadcasted_iota(jnp.int32, (L, L), 0)
    col_i = lax.broadcasted_iota(jnp.int32, (L, L), 1)
    causal = row_i >= col_i
    hpt = LANES // P
    lane_h = lax.broadcasted_iota(jnp.int32, (1, LANES), 1)
    y_chunks = []
    for c in range(ts // L):
        r0 = c * L
        xs_c = xa_ref[r0:r0 + L,0:ssd_w]
        b_c = xa_ref[r0:r0 + L,ssd_w:ssd_w + SSD_GROUPS * N]
        c_c = xa_ref[r0:r0 + L,ssd_w + SSD_GROUPS * N:conv_dim]
        a_rep = a_rep_all[r0:r0 + L]
        a_q_t = a_q_all[r0:r0 + L].T
        dt_t = dt_all[r0:r0 + L].T
        cbs, cgs, bg_ts = [], [], []
        for g in range(SSD_GROUPS):
            bg = b_c[:, g * N:(g + 1) * N]
            cg = c_c[:, g * N:(g + 1) * N]
            cbs.append(lax.dot_general(cg.astype(BF16), bg.astype(BF16), (((1,), (1,)), ((), ())),
                                       preferred_element_type=F32))
            cgs.append(cg)
            bg_ts.append(bg.T)
        y_tiles = []
        for q in range(SSD_HEADS // hpt):
            xs_q = xs_c[:, q * LANES:(q + 1) * LANES].astype(BF16)
            st = state_ref[q]
            rhs = jnp.concatenate([xs_q, st.astype(BF16)], axis=0)
            y_q = upd_q = keep_q = None
            for hh in range(hpt):
                hd = q * hpt + hh
                g = hd // SSD_HEADS_PER_GROUP
                a_col = a_rep[:, hd * LANES:(hd + 1) * LANES]
                a_row = a_q_t[hd:hd + 1, :]
                dt_row = dt_t[hd:hd + 1, :]
                decay = jnp.exp(jnp.where(causal, a_col - a_row, NEG_BIG))
                m = cbs[g] * decay * dt_row
                cs = cgs[g] * jnp.exp(a_col)
                lhs = jnp.concatenate([m, cs], axis=1).astype(BF16)
                y_h = _dot(lhs, rhs)
                a_end = a_row[:, L - 1:L]
                w_row = dt_row * jnp.exp(a_end - a_row)
                upd_h = _dot((bg_ts[g] * w_row).astype(BF16), xs_q)
                keep_h = jnp.exp(a_end)
                if hh == 0:
                    y_q, upd_q, keep_q = y_h, upd_h, keep_h
                else:
                    mine = (lane_h >= hh * P) & (lane_h < (hh + 1) * P)
                    y_q = jnp.where(mine, y_h, y_q)
                    upd_q = jnp.where(mine, upd_h, upd_q)
                    keep_q = jnp.where(mine, keep_h, keep_q)
            y_tiles.append(y_q)
            state_ref[q] = keep_q * st + upd_q
        y_chunks.append(jnp.concatenate(y_tiles, axis=1))
        yield
    y_all = jnp.concatenate(y_chunks, axis=0) if len(y_chunks) > 1 else y_chunks[0]

    gw = ssd_w // SSD_GROUPS
    lane_y = lax.broadcasted_iota(jnp.int32, (1, ssd_w), 1)
    for r in range(0, ts, NORM_ROWS):
        rows = slice(r, r + NORM_ROWS)
        y = (y_all[rows] + xa_ref[rows, 0:ssd_w] * dsk_ref[...]) * _silu(z[rows])
        y2 = y * y
        scale = jnp.zeros_like(y)
        for g in range(SSD_GROUPS):
            in_g = (lane_y >= g * gw) & (lane_y < (g + 1) * gw)
            ms = jnp.sum(jnp.where(in_g, y2, 0.0), axis=-1, keepdims=True) * (1.0 / gw)
            scale = scale + jnp.where(in_g, lax.rsqrt(ms + NORM_EPS), 0.0)
        hb_ref[rows, :] = jnp.concatenate([y_pool[rows], y * scale * nw_ref[...]], axis=1).astype(BF16)
    yield
    mix = _dot(hb_ref[...], wout_ref[...])
    yield

    nrow = d // LANES
    for r in range(0, ts, NORM_ROWS):
        rows = slice(r, r + NORM_ROWS)
        x1 = x_ref[rows, :] + g1 * mix[rows]
        x1_ref[rows, :] = x1
        h2 = _rms(x1) * (1.0 + sc2) + sh2
        for c in range(nrow):
            h2_ref[pl.ds(r * nrow + c, NORM_ROWS, stride=nrow), :] = h2[:, c * LANES:(c + 1) * LANES]
        hb_ref[rows, :] = h2.astype(BF16)
    h_hi = hb_ref[...]
    yield
    logits = lax.dot_general(wrt_ref[...], h_hi, (((1,), (1,)), ((), ())), preferred_element_type=F32)
    logits = logits + jnp.concatenate([br_ref[...]] * (ts // LANES), axis=1)
    row = lax.broadcasted_iota(jnp.int32, (LANES, ts), 0).astype(F32)
    big = float(LANES)
    gl = jnp.where(row < N_EXPERT_GROUPS, logits, NEG_BIG)
    gmax = jnp.max(gl, axis=0, keepdims=True)
    gsum = jnp.sum(jnp.exp(gl - gmax), axis=0, keepdims=True)
    p_g = 1.0 / gsum
    g_idx = jnp.min(jnp.where(gl == gmax, row, big), axis=0, keepdims=True)
    lo = N_EXPERT_GROUPS + EXPERTS_PER_GROUP * g_idx
    el = jnp.where((row >= lo) & (row < lo + EXPERTS_PER_GROUP), logits, NEG_BIG)
    v1 = jnp.max(el, axis=0, keepdims=True)
    i1 = jnp.min(jnp.where(el == v1, row, big), axis=0, keepdims=True)
    el2 = jnp.where(row == i1, NEG_BIG, el)
    v2 = jnp.max(el2, axis=0, keepdims=True)
    i2 = jnp.min(jnp.where(el2 == v2, row, big), axis=0, keepdims=True)
    e21 = jnp.exp(v2 - v1)
    gate1 = p_g / (1.0 + e21)
    gate2 = p_g * e21 / (1.0 + e21)
    e1 = i1 - N_EXPERT_GROUPS
    e2 = i2 - N_EXPERT_GROUPS
    route_t = jnp.where(row == 0, e1, jnp.where(row == 1, e2,
                                                jnp.where(row == 2, gate1, jnp.where(row == 3, gate2, 0.0))))
    rt_ref[...] = route_t[0:SUBLANES]
    route_ref[...] = route_t.T
    cnt_ref[...] += jnp.where((row == e1) | (row == e2), 1.0, 0.0)


def _mixer(x, mod, win, wp, ps, cw, cb, dtb, alog, dsk, nw, wout, wrt, br, *, pool_w, ssd_w):
    k_idx = jnp.arange(2 * LANES, dtype=jnp.int32)[:, None] % LANES
    h_idx = jnp.arange(SSD_HEADS * LANES, dtype=jnp.int32)[None, :] // LANES
    rep = (k_idx == h_idx).astype(BF16)
    r_idx = jnp.arange(SEQ_TILE, dtype=jnp.int32)[:, None]
    c_idx = jnp.arange(3 * SEQ_TILE, dtype=jnp.int32)[None, :] % SEQ_TILE
    tril3 = ((r_idx >= c_idx) & (r_idx // SSD_CHUNK == c_idx // SSD_CHUNK)).astype(BF16)
    b, s, d = x.shape
    ts = SEQ_TILE
    conv_dim = cw.shape[1]
    nrow = d // LANES

    def full(a):
        nd = a.ndim
        return pl.BlockSpec(a.shape, lambda i, j, _nd=nd: (0,) * _nd)

    nb = MIXER_SEQS
    seq_tile = lambda i, j: (i, j, 0)
    x1, h2, route, route_t, counts = pl.pallas_call(
        functools.partial(_mixer_kernel, pool_w=pool_w, ssd_w=ssd_w),
        grid=(b // nb, s // ts),
        in_specs=[pl.BlockSpec((nb, ts, d), seq_tile),
                  pl.BlockSpec((nb, 6, d), lambda i, j: (i, 0, 0)),
                  full(win), full(wp), full(ps), full(cw), full(cb), full(dtb), full(alog),
                  full(dsk), full(nw), full(wout), full(wrt), full(br), full(rep), full(tril3)],
        out_specs=[pl.BlockSpec((nb, ts, d), seq_tile),
                   pl.BlockSpec((nb, ts * nrow, LANES), seq_tile),
                   pl.BlockSpec((nb, ts, LANES), seq_tile),
                   pl.BlockSpec((nb, SUBLANES, ts), lambda i, j: (i, 0, j)),
                   pl.BlockSpec((LANES, ts), lambda i, j: (0, 0))],
        out_shape=[jax.ShapeDtypeStruct((b, s, d), F32),
                   jax.ShapeDtypeStruct((b, s * nrow, LANES), F32),
                   jax.ShapeDtypeStruct((b, s, LANES), F32),
                   jax.ShapeDtypeStruct((b, SUBLANES, s), F32),
                   jax.ShapeDtypeStruct((LANES, ts), F32)],
        scratch_shapes=[pltpu.VMEM((nb, ssd_w // LANES, SSD_STATE, LANES), F32),
                        pltpu.VMEM((nb, conv_dim // LANES, (ts // SUBLANES + 2 * CONV_PAD) * CONV_PITCH, LANES),
                                   F32),
                        pltpu.VMEM((nb, POOL_HALO + ts, pool_w), F32),
                        pltpu.VMEM((nb, ts, conv_dim), F32),
                        pltpu.VMEM((nb, ts, d), BF16)],
        compiler_params=pltpu.CompilerParams(dimension_semantics=("arbitrary", "arbitrary"),
                                             vmem_limit_bytes=VMEM_LIMIT_BYTES),
        name="mixer",
    )(x, mod, win, wp, ps, cw, cb, dtb, alog, dsk, nw, wout, wrt, br, rep, tril3)
    return x1, h2.reshape(b * s * nrow, LANES), route.reshape(b * s, LANES), route_t, counts


def _rank_kernel(route_ref, start_ref, dest_ref, carry_ref):
    ta = route_ref.shape[2]

    @pl.when(pl.program_id(0) == 0)
    def _():
        carry_ref[...] = start_ref[...]

    route_t = route_ref[0]
    e_idx = lax.broadcasted_iota(jnp.int32, (LANES, ta), 0).astype(F32)
    ohs = [jnp.where(e_idx == route_t[k:k + 1, :], 1.0, 0.0) for k in range(TOP_K)]
    oh = (ohs[0] + ohs[1]).astype(BF16)
    earlier = (lax.broadcasted_iota(jnp.int32, (ta, ta), 0)
               < lax.broadcasted_iota(jnp.int32, (ta, ta), 1))
    before = _dot(oh, jnp.where(earlier, 1.0, 0.0).astype(BF16))
    carry = carry_ref[...]
    base = jnp.concatenate([carry] * (ta // LANES), axis=1) + before
    for k in range(TOP_K):
        dest_ref[0, k:k + 1, :] = jnp.sum(ohs[k] * base, axis=0, keepdims=True).astype(jnp.int32)
    carry_ref[...] = carry + _dot(oh, jnp.ones((ta, LANES), BF16))


def _rank(route_t, start_rep):
    b, _, s = route_t.shape
    spt = s // ROW_COPY_TILE
    t = b * s
    return pl.pallas_call(
        _rank_kernel,
        grid=(t // ROW_COPY_TILE,),
        in_specs=[pl.BlockSpec((1, SUBLANES, ROW_COPY_TILE), lambda i: (i // spt, 0, i % spt)),
                  pl.BlockSpec((LANES, LANES), lambda i: (0, 0))],
        out_specs=pl.BlockSpec((1, TOP_K, ROW_COPY_TILE), lambda i: (i, 0, 0)),
        out_shape=jax.ShapeDtypeStruct((t // ROW_COPY_TILE, TOP_K, ROW_COPY_TILE), jnp.int32),
        scratch_shapes=[pltpu.VMEM((LANES, LANES), F32)],
        compiler_params=pltpu.CompilerParams(dimension_semantics=("arbitrary",)),
        name="rank",
    )(route_t, start_rep)


def _dispatch_kernel(dest_ref, h2_ref, xs_hbm, sem, *, nrow):
    tile_rows = h2_ref.shape[0]

    part_tokens = dest_ref.shape[2]
    for part in range(dest_ref.shape[0]):
        for tk in range(part_tokens):
            for k in range(TOP_K):
                pltpu.make_async_copy(_slab_of_row(h2_ref, part * part_tokens + tk, nrow),
                                      _slab_of_row(xs_hbm, dest_ref[part, k, tk], nrow),
                                      sem).start(priority=(tk * TOP_K + k) % DMA_THREADS)

    for _ in range(TOP_K):
        pltpu.make_async_copy(h2_ref, xs_hbm.at[pl.ds(0, tile_rows)], sem).wait()


def _dispatch(dest, h2, nrow):
    t = h2.shape[0] // nrow
    return pl.pallas_call(
        functools.partial(_dispatch_kernel, nrow=nrow),
        grid=(t // (DISPATCH_PARTS * ROW_COPY_TILE),),
        in_specs=[pl.BlockSpec((DISPATCH_PARTS, TOP_K, ROW_COPY_TILE), lambda i: (i, 0, 0),
                               memory_space=pltpu.SMEM),
                  pl.BlockSpec((DISPATCH_PARTS * ROW_COPY_TILE * nrow, LANES), lambda i: (i, 0))],
        out_specs=pl.BlockSpec(memory_space=pl.ANY),
        out_shape=jax.ShapeDtypeStruct((t * TOP_K * nrow, LANES), F32),
        scratch_shapes=[pltpu.SemaphoreType.DMA(())],
        compiler_params=pltpu.CompilerParams(dimension_semantics=("arbitrary",),
                                             has_side_effects=True),
        name="dispatch",
    )(dest, h2)


def _expert_kernel(blk_ref, exp_ref, lo_ref, hi_ref, first_ref, new_ref, xs_ref, w13_ref, w2_ref, ys_ref,
                   w13_bf, w2_bf, *, nrow):
    i = pl.program_id(0)
    rows = xs_ref.shape[0] // nrow
    lo = lo_ref[i]
    hi = hi_ref[i]

    sub = rows // EXPERT_CHAINS

    @pl.when(new_ref[i] == 1)
    def _():
        w13_bf[...] = w13_ref[0].astype(BF16)
        w2_bf[...] = w2_ref[0].astype(BF16)

    def chain(r0):
        xb = jnp.concatenate([xs_ref[pl.ds(r0 * nrow + c, sub, stride=nrow), :].astype(BF16)
                              for c in range(nrow)], axis=1)
        yield
        hu = _dot(xb, w13_bf[...])
        yield
        f = hu.shape[1] // 2
        act = (_silu(hu[:, :f]) * hu[:, f:]).astype(BF16)
        yield
        y = _dot(act, w2_bf[...])
        yield
        ridx = r0 + lax.broadcasted_iota(jnp.int32, (sub, 1), 0)
        take = (ridx >= lo_eff) & (ridx < hi_eff)
        for c in range(nrow):
            sl = pl.ds(r0 * nrow + c, sub, stride=nrow)
            ys_ref[sl, :] = jnp.where(take, y[:, c * LANES:(c + 1) * LANES], ys_ref[sl, :])

    is_first = first_ref[i] == 1
    lo_eff = jnp.where(is_first, 0, lo)
    hi_eff = jnp.where(is_first, rows, hi)

    @pl.when(hi > lo)
    def _():
        _run_skewed([chain(k * sub) for k in range(EXPERT_CHAINS)])


def _experts(item_blk, item_exp, item_lo, item_hi, item_first, item_new, xs, w13, w2, nrow):
    a = xs.shape[0] // nrow
    n_items = item_blk.shape[0]
    d = w13.shape[1]
    ff2 = w13.shape[2]
    grid_spec = pltpu.PrefetchScalarGridSpec(
        num_scalar_prefetch=6,
        grid=(n_items,),
        in_specs=[pl.BlockSpec((EXPERT_BLOCK * nrow, LANES), lambda i, b, e, lo, hi, fr, nw: (b[i], 0)),
                  pl.BlockSpec((1, d, ff2), lambda i, b, e, lo, hi, fr, nw: (e[i], 0, 0)),
                  pl.BlockSpec((1, ff2 // 2, d), lambda i, b, e, lo, hi, fr, nw: (e[i], 0, 0))],
        out_specs=pl.BlockSpec((EXPERT_BLOCK * nrow, LANES), lambda i, b, e, lo, hi, fr, nw: (b[i], 0)),
        scratch_shapes=[pltpu.VMEM((d, ff2), BF16), pltpu.VMEM((ff2 // 2, d), BF16)],
    )
    return pl.pallas_call(
        functools.partial(_expert_kernel, nrow=nrow),
        grid_spec=grid_spec,
        out_shape=jax.ShapeDtypeStruct((a * nrow, LANES), F32),
        compiler_params=pltpu.CompilerParams(dimension_semantics=("arbitrary",),
                                             vmem_limit_bytes=VMEM_LIMIT_BYTES),
        name="experts",
    )(item_blk, item_exp, item_lo, item_hi, item_first, item_new, xs, w13, w2)


def _final_kernel(dest_ref, dnext_ref, x1_ref, route_ref, mod_ref, fw_ref, ys_hbm, o_ref,
                  buf_even, buf_odd, sems):
    tf = x1_ref.shape[1]
    nrow = buf_even.shape[1] // tf
    step = pl.program_id(0)
    last = step + 1 == pl.num_programs(0)

    def start_gathers(d_ref, buf, sem, t0, t1):
        for tk in range(t0, t1):
            for k in range(TOP_K):
                pltpu.make_async_copy(_slab_of_row(ys_hbm, d_ref[0, k, tk], nrow),
                                      _slab_of_row(buf.at[k], tk, nrow),
                                      sem).start(priority=(tk * TOP_K + k) % DMA_THREADS)

    def wait_gathers(buf, sem):
        for k in range(TOP_K):
            pltpu.make_async_copy(ys_hbm.at[pl.ds(0, tf * nrow)], buf.at[k], sem).wait()

    @pl.when(step == 0)
    def _():
        start_gathers(dest_ref, buf_even, sems.at[0], 0, tf)

    def body(cur, cur_sem, nxt, nxt_sem):
        wait_gathers(cur, cur_sem)
        g2 = mod_ref[0][5:6]
        for r0 in range(0, tf, NORM_ROWS):
            start_gathers(dnext_ref, nxt, nxt_sem, r0, r0 + NORM_ROWS)
            rows = slice(r0, r0 + NORM_ROWS)
            gates = route_ref[rows, :]
            y = jnp.concatenate(
                [cur[0, pl.ds(r0 * nrow + c, NORM_ROWS, stride=nrow), :] * gates[:, 2:3]
                 + cur[1, pl.ds(r0 * nrow + c, NORM_ROWS, stride=nrow), :] * gates[:, 3:4]
                 for c in range(nrow)], axis=1)
            o_ref[0, rows, :] = _rms(x1_ref[0, rows, :] + g2 * y) * fw_ref[...]

        @pl.when(last)
        def _():
            wait_gathers(nxt, nxt_sem)

    @pl.when(lax.rem(step, 2) == 0)
    def _():
        body(buf_even, sems.at[0], buf_odd, sems.at[1])

    @pl.when(lax.rem(step, 2) == 1)
    def _():
        body(buf_odd, sems.at[1], buf_even, sems.at[0])


def _final(dest, x1, route, mod, fw, ys):
    b, s, d = x1.shape
    tf = ROW_COPY_TILE
    nrow = d // LANES
    spt = s // tf
    n_steps = b * spt
    return pl.pallas_call(
        _final_kernel,
        grid=(n_steps,),
        in_specs=[pl.BlockSpec((1, TOP_K, tf), lambda i: (i, 0, 0), memory_space=pltpu.SMEM),
                  pl.BlockSpec((1, TOP_K, tf), lambda i: (jnp.minimum(i + 1, n_steps - 1), 0, 0),
                               memory_space=pltpu.SMEM),
                  pl.BlockSpec((1, tf, d), lambda i: (i // spt, i % spt, 0)),
                  pl.BlockSpec((tf, LANES), lambda i: (i, 0)),
                  pl.BlockSpec((1, 6, d), lambda i: (i // spt, 0, 0)),
                  pl.BlockSpec((1, d), lambda i: (0, 0)),
                  pl.BlockSpec(memory_space=pl.ANY)],
        out_specs=pl.BlockSpec((1, tf, d), lambda i: (i // spt, i % spt, 0)),
        out_shape=jax.ShapeDtypeStruct((b, s, d), F32),
        scratch_shapes=[pltpu.VMEM((TOP_K, tf * nrow, LANES), F32),
                        pltpu.VMEM((TOP_K, tf * nrow, LANES), F32),
                        pltpu.SemaphoreType.DMA((2,))],
        compiler_params=pltpu.CompilerParams(dimension_semantics=("arbitrary",),
                                             vmem_limit_bytes=VMEM_LIMIT_BYTES),
        name="final",
    )(dest, dest, x1, route, mod, fw, ys)


def _pad_lanes(v, fill=0.0):
    n = v.shape[-1]
    return jnp.pad(v, [(0, 0)] * (v.ndim - 1) + [(0, LANES - n)], constant_values=fill)


def _work_items(counts, n_blocks):
    n_items = n_blocks + N_EXPERTS - 1
    ends = jnp.cumsum(counts)
    starts = ends - counts
    first_blk = starts // EXPERT_BLOCK
    last_blk = jnp.maximum(ends - 1, starts) // EXPERT_BLOCK
    n_e = jnp.where(counts > 0, last_blk - first_blk + 1, 0)
    item_end = jnp.cumsum(n_e)
    item_start = item_end - n_e
    ids = jnp.arange(n_items, dtype=jnp.int32)
    total = item_end[-1]
    ids_c = jnp.minimum(ids, total - 1)
    e = jnp.sum((item_end[None, :] <= ids_c[:, None]).astype(jnp.int32), axis=1)
    onehot = (e[:, None] == jnp.arange(N_EXPERTS, dtype=jnp.int32)[None, :]).astype(jnp.int32)
    pick = lambda v: jnp.sum(onehot * v[None, :], axis=1)
    blk = pick(first_blk) + ids_c - pick(item_start)
    lo = jnp.clip(pick(starts) - blk * EXPERT_BLOCK, 0, EXPERT_BLOCK)
    hi = jnp.clip(pick(ends) - blk * EXPERT_BLOCK, 0, EXPERT_BLOCK)
    hi = jnp.where(ids < total, hi, lo)
    prev_blk = jnp.concatenate([jnp.full((1,), -1, jnp.int32), blk[:-1]])
    first = (blk != prev_blk).astype(jnp.int32)
    prev_e = jnp.concatenate([jnp.full((1,), -1, jnp.int32), e[:-1]])
    new_expert = (e != prev_e).astype(jnp.int32)
    return blk, e, lo, hi, first, new_expert


def kernel(x, c, w_ada, b_ada, w_in, w_pool, pool_scale, conv_w, conv_b, dt_bias, a_log, d_skip,
           ssd_norm_w, w_out, w_group, b_group, w_router, b_router, w13, w2, final_norm_w):
    b, s, d = x.shape
    depth = w_ada.shape[0]
    t = b * s
    pool_w = w_pool.shape[1] * w_pool.shape[2]
    ssd_w = SSD_HEADS * SSD_HEAD_DIM
    conv_dim = conv_w.shape[2]
    assert s % SEQ_TILE == 0 and SEQ_TILE % SSD_CHUNK == 0 and b % MIXER_SEQS == 0
    assert s % ROW_COPY_TILE == 0 and ROW_COPY_TILE % LANES == 0
    assert t % (DISPATCH_PARTS * ROW_COPY_TILE) == 0 and SEQ_TILE % NORM_ROWS == 0
    assert (t * TOP_K) % EXPERT_BLOCK == 0 and d % LANES == 0
    assert pool_w == 2 * LANES and len(POOL_WINDOWS) * POOL_GROUP_DIM == pool_w
    assert N_EXPERT_GROUPS + N_EXPERTS <= LANES and SSD_HEADS <= LANES
    assert depth == 1, "the final RMSNorm is fused into the last layer's combine step"

    for layer in range(depth):
        mod = _ada(c, w_ada[layer], b_ada[layer]).reshape(b, 6, d)

        o_dt = pool_w + ssd_w + conv_dim
        win = jnp.pad(w_in[layer].astype(BF16), ((0, 0), (0, o_dt + LANES - w_in.shape[2])))
        wp = jnp.zeros((pool_w, pool_w), F32)
        for g in range(len(POOL_WINDOWS)):
            sl = slice(g * POOL_GROUP_DIM, (g + 1) * POOL_GROUP_DIM)
            wp = wp.at[sl, sl].set(w_pool[layer, g])
        wp = wp.astype(BF16)
        wrt = _pad_lanes(jnp.concatenate([w_group[layer], w_router[layer]], axis=1)).T.astype(BF16)
        b_route = _pad_lanes(jnp.concatenate([b_group[layer], b_router[layer]])[None, :])
        br = jnp.broadcast_to(b_route.reshape(LANES, 1), (LANES, LANES))

        x1, h2, route, route_t, counts_part = _mixer(
            x, mod, win, wp, pool_scale[layer][None, :], conv_w[layer], conv_b[layer][None, :],
            _pad_lanes(dt_bias[layer][None, :]), _pad_lanes(a_log[layer][None, :], fill=NEG_BIG),
            jnp.repeat(d_skip[layer], SSD_HEAD_DIM)[None, :], ssd_norm_w[layer][None, :],
            w_out[layer].astype(BF16), wrt, br, pool_w=pool_w, ssd_w=ssd_w)

        counts_f = jnp.sum(counts_part, axis=1, keepdims=True)
        start_rep = jnp.broadcast_to(jnp.cumsum(counts_f, axis=0) - counts_f, (LANES, LANES))
        dest = _rank(route_t, start_rep)
        counts = counts_f[:N_EXPERTS, 0].astype(jnp.int32)
        items = _work_items(counts, (t * TOP_K) // EXPERT_BLOCK)

        nrow = d // LANES
        xs = _dispatch(dest, h2, nrow)
        ys = _experts(*items, xs, w13[layer], w2[layer], nrow)
        x = _final(dest, x1, route, mod, final_norm_w[None, :], ys)
    return x
```

```python
import functools

import jax
import jax.numpy as jnp
from jax import lax
from jax.experimental import pallas as pl
from jax.experimental.pallas import tpu as pltpu

POOL_WINDOWS = (2, 4, 8, 16)
POOL_GROUP_DIM = 64
SSD_HEAD_DIM = 64
SSD_GROUPS = 4
SSD_HEADS_PER_GROUP = 3
SSD_HEADS = SSD_GROUPS * SSD_HEADS_PER_GROUP
SSD_STATE = 128
SSD_CONV = 4
SSD_CHUNK = 128
N_EXPERT_GROUPS = 4
EXPERTS_PER_GROUP = 8
N_EXPERTS = N_EXPERT_GROUPS * EXPERTS_PER_GROUP
TOP_K = 2
NORM_EPS = 1e-6

LANES = 128
SUBLANES = 8
VMEM_LIMIT_BYTES = 56 * 1024 * 1024

SEQ_TILE = 256
MIXER_SEQS = 2
PROJ_PIECE = 1024
NORM_ROWS = 32
CONV_PAD = SUBLANES
CONV_PITCH = SUBLANES + 1
POOL_HALO = 16
ROW_COPY_TILE = 512
EXPERT_BLOCK = 512
EXPERT_CHAINS = 2
DISPATCH_PARTS = 4
DMA_THREADS = 2

NEG_BIG = -1e30
F32 = jnp.float32
BF16 = jnp.bfloat16


def _silu(v):
    half = 0.5 * v
    return half + half * jnp.tanh(half)


def _softplus(v):
    return jnp.maximum(v, 0.0) + jnp.log(1.0 + jnp.exp(-jnp.abs(v)))


def _rms(v):
    return v * lax.rsqrt(jnp.mean(v * v, axis=-1, keepdims=True) + NORM_EPS)


def _dot(a, b):
    return jnp.dot(a, b, preferred_element_type=F32)


def _dot_exact(a, b):
    return jnp.dot(a, b, preferred_element_type=F32, precision=lax.Precision.HIGHEST)


def _run_skewed(chains):
    live = [False] * len(chains)
    started = 0
    while started < len(chains) or any(live):
        if started < len(chains):
            live[started] = True
            started += 1
        for k, chain in enumerate(chains):
            if live[k]:
                try:
                    next(chain)
                except StopIteration:
                    live[k] = False


def _slab_of_row(ref, row, nrow):
    start = row * nrow
    return ref.at[pl.ds(start if isinstance(start, int) else pl.multiple_of(start, nrow), nrow)]


def _ada_kernel(c_ref, w_ref, b_ref, o_ref):
    o_ref[...] = _dot_exact(_silu(c_ref[...]), w_ref[...]) + b_ref[...]


def _ada(c, w_ada, b_ada):
    b, d = c.shape
    n = w_ada.shape[1]
    return pl.pallas_call(
        _ada_kernel,
        grid=(n // d,),
        in_specs=[pl.BlockSpec((b, d), lambda i: (0, 0)),
                  pl.BlockSpec((d, d), lambda i: (0, i)),
                  pl.BlockSpec((1, d), lambda i: (0, i))],
        out_specs=pl.BlockSpec((b, d), lambda i: (0, i)),
        out_shape=jax.ShapeDtypeStruct((b, n), F32),
        compiler_params=pltpu.CompilerParams(dimension_semantics=("arbitrary",),
                                             vmem_limit_bytes=VMEM_LIMIT_BYTES),
        name="ada",
    )(c, w_ada, b_ada.reshape(1, n))


def _mixer_kernel(x_ref, mod_ref, win_ref, wp_ref, ps_ref, cw_ref, cb_ref, dtb_ref, alog_ref,
                  dsk_ref, nw_ref, wout_ref, wrt_ref, br_ref, rep_ref, tril3_ref,
                  x1_ref, h2_ref, route_ref, rt_ref, cnt_ref,
                  state_ref, xext_ref, uext_ref, xa_ref, hb_ref, *, pool_w, ssd_w):
    @pl.when((pl.program_id(0) == 0) & (pl.program_id(1) == 0))
    def _():
        cnt_ref[...] = jnp.zeros_like(cnt_ref)
        xext_ref[...] = jnp.zeros_like(xext_ref)

    @pl.when(pl.program_id(1) == 0)
    def _():
        state_ref[...] = jnp.zeros_like(state_ref)
        pc = x_ref.shape[1] // SUBLANES
        for bi in range(xext_ref.shape[0]):
            for lt in range(xext_ref.shape[1]):
                xext_ref[bi, lt, pl.ds(CONV_PAD * CONV_PITCH, pc, stride=CONV_PITCH), :] = (
                    jnp.zeros((pc, LANES), F32))
        uext_ref[:, 0:POOL_HALO, :] = jnp.zeros((uext_ref.shape[0], POOL_HALO, uext_ref.shape[2]), F32)

    _run_skewed([_mixer_tile(x_ref.at[bi], mod_ref.at[bi], win_ref, wp_ref, ps_ref, cw_ref, cb_ref, dtb_ref,
                             alog_ref, dsk_ref, nw_ref, wout_ref, wrt_ref, br_ref, rep_ref, tril3_ref,
                             x1_ref.at[bi], h2_ref.at[bi], route_ref.at[bi], rt_ref.at[bi], cnt_ref,
                             state_ref.at[bi], xext_ref.at[bi], uext_ref.at[bi], xa_ref.at[bi], hb_ref.at[bi], pool_w=pool_w, ssd_w=ssd_w)
                 for bi in range(x_ref.shape[0])])


def _mixer_tile(x_ref, mod_ref, win_ref, wp_ref, ps_ref, cw_ref, cb_ref, dtb_ref, alog_ref,
                dsk_ref, nw_ref, wout_ref, wrt_ref, br_ref, rep_ref, tril3_ref,
                x1_ref, h2_ref, route_ref, rt_ref, cnt_ref,
                state_ref, xext_ref, uext_ref, xa_ref, hb_ref, *, pool_w, ssd_w):
    ts = x_ref.shape[0]
    d = x_ref.shape[1]
    L = SSD_CHUNK
    N = SSD_STATE
    P = SSD_HEAD_DIM
    j = pl.program_id(1)

    mod = mod_ref[...]
    sh1, sc1, g1 = mod[0:1], mod[1:2], mod[2:3]
    sh2, sc2 = mod[3:4], mod[4:5]

    for r in range(0, ts, NORM_ROWS):
        hb_ref[r:r + NORM_ROWS, :] = (_rms(x_ref[r:r + NORM_ROWS, :]) * (1.0 + sc1) + sh1).astype(BF16)
    h = hb_ref[...]
    yield
    n_proj = win_ref.shape[1]
    edges = [min(n_proj, e * PROJ_PIECE) for e in range(pl.cdiv(n_proj, PROJ_PIECE) + 1)]
    pieces = []
    for a, b in zip(edges[:-1], edges[1:]):
        pieces.append(_dot(h, win_ref[:, a:b]))
        yield
    proj = jnp.concatenate(pieces, axis=1)
    o_z = pool_w
    o_x = pool_w + ssd_w
    conv_dim = ssd_w + 2 * SSD_GROUPS * N
    o_dt = o_x + conv_dim
    u = proj[:, 0:pool_w]
    z = proj[:, o_z:o_x]
    xbc = proj[:, o_x:o_dt]
    dtr = proj[:, o_dt:o_dt + LANES]

    uext_ref[POOL_HALO:POOL_HALO + ts, :] = u
    lane_u = lax.broadcasted_iota(jnp.int32, (1, LANES), 1)
    upper = lane_u >= POOL_GROUP_DIM

    def ush(k, col):
        return uext_ref[POOL_HALO - k:POOL_HALO - k + ts, col * LANES:(col + 1) * LANES]

    w0, w1, w2, w3 = POOL_WINDOWS
    lo_a = ush(0, 0)
    for k in range(1, w0):
        lo_a = lo_a + ush(k, 0)
    lo_b = ush(w0, 0)
    for k in range(w0 + 1, w1):
        lo_b = lo_b + ush(k, 0)
    hi_a = ush(0, 1)
    for k in range(1, w2):
        hi_a = hi_a + ush(k, 1)
    hi_b = ush(w2, 1)
    for k in range(w2 + 1, w3):
        hi_b = hi_b + ush(k, 1)
    win_lo = lo_a + jnp.where(upper, lo_b, 0.0)
    win_hi = hi_a + jnp.where(upper, hi_b, 0.0)
    pos = (j * ts + 1 + lax.broadcasted_iota(jnp.int32, (ts, 1), 0)).astype(F32)
    cnt_lo = jnp.minimum(pos, jnp.where(upper, float(w1), float(w0)))
    cnt_hi = jnp.minimum(pos, jnp.where(upper, float(w3), float(w2)))
    pdiff = jnp.concatenate([win_lo / cnt_lo, win_hi / cnt_hi], axis=1) - u
    y_pool = _dot(pdiff.astype(BF16), wp_ref[...]) * ps_ref[...]
    uext_ref[0:POOL_HALO, :] = uext_ref[ts:ts + POOL_HALO, :]

    a_neg = -jnp.exp(alog_ref[...])
    dt_all = _softplus(dtr + dtb_ref[...])
    da = dt_all * a_neg
    d1 = da.astype(BF16)
    r1 = da - d1.astype(F32)
    d2 = r1.astype(BF16)
    d3 = (r1 - d2.astype(F32)).astype(BF16)
    a_cs = _dot(tril3_ref[...], jnp.concatenate([d1, d2, d3], axis=0))
    a_hi = a_cs.astype(BF16)
    a_lo = (a_cs - a_hi.astype(F32)).astype(BF16)
    a_q_all = a_hi.astype(F32) + a_lo.astype(F32)
    a_rep_all = _dot(jnp.concatenate([a_hi, a_lo], axis=1), rep_ref[...])
    yield

    pc = ts // SUBLANES
    n_lt = xext_ref.shape[0]
    for lt in range(n_lt):
        for col in range(SUBLANES):
            xext_ref[lt, pl.ds(CONV_PAD * CONV_PITCH + col + 1, pc, stride=CONV_PITCH), :] = (
                xbc[col * pc:(col + 1) * pc, lt * LANES:(lt + 1) * LANES])
    yield
    sub_i = lax.broadcasted_iota(jnp.int32, (SUBLANES, 1), 0)
    for lt in range(n_lt):
        lanes = slice(lt * LANES, (lt + 1) * LANES)
        cw = cw_ref[:, lanes]
        for col in range(SUBLANES):
            acc = cb_ref[:, lanes] + cw[SSD_CONV - 1:SSD_CONV] * xbc[col * pc:(col + 1) * pc, lanes]
            for k in range(1, SSD_CONV):
                main = xext_ref[lt, pl.ds((CONV_PAD - k) * CONV_PITCH + col + 1, pc, stride=CONV_PITCH), :]
                prev = xext_ref[lt, pl.ds((CONV_PAD + pc - k) * CONV_PITCH + col, SUBLANES,
                                          stride=CONV_PITCH), :]
                tap = jnp.concatenate([jnp.where(sub_i < k, prev, main[0:SUBLANES]), main[SUBLANES:]], axis=0)
                acc = acc + cw[SSD_CONV - 1 - k:SSD_CONV - k] * tap
            xa_ref[col * pc:(col + 1) * pc, lanes] = _silu(acc)
    for lt in range(n_lt):
        xext_ref[lt, pl.ds(CONV_PAD * CONV_PITCH, pc, stride=CONV_PITCH), :] = (
            xext_ref[lt, pl.ds(CONV_PAD * CONV_PITCH + SUBLANES, pc, stride=CONV_PITCH), :])
    yield

    row_i = lax.broadcasted_iota(jnp.int32, (L, L), 0)
    col_i = lax.broadcasted_iota(jnp.int32, (L, L), 1)
    causal = row_i >= col_i
    hpt = LANES // P
    lane_h = lax.broadcasted_iota(jnp.int32, (1, LANES), 1)
    y_chunks = []
    for c in range(ts // L):
        r0 = c * L
        xs_c = xa_ref[r0:r0 + L,0:ssd_w]
        b_c = xa_ref[r0:r0 + L,ssd_w:ssd_w + SSD_GROUPS * N]
        c_c = xa_ref[r0:r0 + L,ssd_w + SSD_GROUPS * N:conv_dim]
        a_rep = a_rep_all[r0:r0 + L]
        a_q_t = a_q_all[r0:r0 + L].T
        dt_t = dt_all[r0:r0 + L].T
        cbs, cgs, bg_ts = [], [], []
        for g in range(SSD_GROUPS):
            bg = b_c[:, g * N:(g + 1) * N]
            cg = c_c[:, g * N:(g + 1) * N]
            cbs.append(lax.dot_general(cg.astype(BF16), bg.astype(BF16), (((1,), (1,)), ((), ())),
                                       preferred_element_type=F32))
            cgs.append(cg)
            bg_ts.append(bg.T)
        y_tiles = []
        for q in range(SSD_HEADS // hpt):
            xs_q = xs_c[:, q * LANES:(q + 1) * LANES].astype(BF16)
            st = state_ref[q]
            rhs = jnp.concatenate([xs_q, st.astype(BF16)], axis=0)
            y_q = upd_q = keep_q = None
            for hh in range(hpt):
                hd = q * hpt + hh
                g = hd // SSD_HEADS_PER_GROUP
                a_col = a_rep[:, hd * LANES:(hd + 1) * LANES]
                a_row = a_q_t[hd:hd + 1, :]
                dt_row = dt_t[hd:hd + 1, :]
                decay = jnp.exp(jnp.where(causal, a_col - a_row, NEG_BIG))
                m = cbs[g] * decay * dt_row
                cs = cgs[g] * jnp.exp(a_col)
                lhs = jnp.concatenate([m, cs], axis=1).astype(BF16)
                y_h = _dot(lhs, rhs)
                a_end = a_row[:, L - 1:L]
                w_row = dt_row * jnp.exp(a_end - a_row)
                upd_h = _dot((bg_ts[g] * w_row).astype(BF16), xs_q)
                keep_h = jnp.exp(a_end)
                if hh == 0:
                    y_q, upd_q, keep_q = y_h, upd_h, keep_h
                else:
                    mine = (lane_h >= hh * P) & (lane_h < (hh + 1) * P)
                    y_q = jnp.where(mine, y_h, y_q)
                    upd_q = jnp.where(mine, upd_h, upd_q)
                    keep_q = jnp.where(mine, keep_h, keep_q)
            y_tiles.append(y_q)
            state_ref[q] = keep_q * st + upd_q
        y_chunks.append(jnp.concatenate(y_tiles, axis=1))
        yield
    y_all = jnp.concatenate(y_chunks, axis=0) if len(y_chunks) > 1 else y_chunks[0]

    gw = ssd_w // SSD_GROUPS
    lane_y = lax.broadcasted_iota(jnp.int32, (1, ssd_w), 1)
    for r in range(0, ts, NORM_ROWS):
        rows = slice(r, r + NORM_ROWS)
        y = (y_all[rows] + xa_ref[rows, 0:ssd_w] * dsk_ref[...]) * _silu(z[rows])
        y2 = y * y
        scale = jnp.zeros_like(y)
        for g in range(SSD_GROUPS):
            in_g = (lane_y >= g * gw) & (lane_y < (g + 1) * gw)
            ms = jnp.sum(jnp.where(in_g, y2, 0.0), axis=-1, keepdims=True) * (1.0 / gw)
            scale = scale + jnp.where(in_g, lax.rsqrt(ms + NORM_EPS), 0.0)
        hb_ref[rows, :] = jnp.concatenate([y_pool[rows], y * scale * nw_ref[...]], axis=1).astype(BF16)
    yield
    mix = _dot(hb_ref[...], wout_ref[...])
    yield

    nrow = d // LANES
    for r in range(0, ts, NORM_ROWS):
        rows = slice(r, r + NORM_ROWS)
        x1 = x_ref[rows, :] + g1 * mix[rows]
        x1_ref[rows, :] = x1
        h2 = _rms(x1) * (1.0 + sc2) + sh2
        for c in range(nrow):
            h2_ref[pl.ds(r * nrow + c, NORM_ROWS, stride=nrow), :] = h2[:, c * LANES:(c + 1) * LANES]
        hb_ref[rows, :] = h2.astype(BF16)
    h_hi = hb_ref[...]
    yield
    logits = lax.dot_general(wrt_ref[...], h_hi, (((1,), (1,)), ((), ())), preferred_element_type=F32)
    logits = logits + jnp.concatenate([br_ref[...]] * (ts // LANES), axis=1)
    row = lax.broadcasted_iota(jnp.int32, (LANES, ts), 0).astype(F32)
    big = float(LANES)
    gl = jnp.where(row < N_EXPERT_GROUPS, logits, NEG_BIG)
    gmax = jnp.max(gl, axis=0, keepdims=True)
    gsum = jnp.sum(jnp.exp(gl - gmax), axis=0, keepdims=True)
    p_g = 1.0 / gsum
    g_idx = jnp.min(jnp.where(gl == gmax, row, big), axis=0, keepdims=True)
    lo = N_EXPERT_GROUPS + EXPERTS_PER_GROUP * g_idx
    el = jnp.where((row >= lo) & (row < lo + EXPERTS_PER_GROUP), logits, NEG_BIG)
    v1 = jnp.max(el, axis=0, keepdims=True)
    i1 = jnp.min(jnp.where(el == v1, row, big), axis=0, keepdims=True)
    el2 = jnp.where(row == i1, NEG_BIG, el)
    v2 = jnp.max(el2, axis=0, keepdims=True)
    i2 = jnp.min(jnp.where(el2 == v2, row, big), axis=0, keepdims=True)
    e21 = jnp.exp(v2 - v1)
    gate1 = p_g / (1.0 + e21)
    gate2 = p_g * e21 / (1.0 + e21)
    e1 = i1 - N_EXPERT_GROUPS
    e2 = i2 - N_EXPERT_GROUPS
    route_t = jnp.where(row == 0, e1, jnp.where(row == 1, e2,
                                                jnp.where(row == 2, gate1, jnp.where(row == 3, gate2, 0.0))))
    rt_ref[...] = route_t[0:SUBLANES]
    route_ref[...] = route_t.T
    cnt_ref[...] += jnp.where((row == e1) | (row == e2), 1.0, 0.0)


def _mixer(x, mod, win, wp, ps, cw, cb, dtb, alog, dsk, nw, wout, wrt, br, *, pool_w, ssd_w):
    k_idx = jnp.arange(2 * LANES, dtype=jnp.int32)[:, None] % LANES
    h_idx = jnp.arange(SSD_HEADS * LANES, dtype=jnp.int32)[None, :] // LANES
    rep = (k_idx == h_idx).astype(BF16)
    r_idx = jnp.arange(SEQ_TILE, dtype=jnp.int32)[:, None]
    c_idx = jnp.arange(3 * SEQ_TILE, dtype=jnp.int32)[None, :] % SEQ_TILE
    tril3 = ((r_idx >= c_idx) & (r_idx // SSD_CHUNK == c_idx // SSD_CHUNK)).astype(BF16)
    b, s, d = x.shape
    ts = SEQ_TILE
    conv_dim = cw.shape[1]
    nrow = d // LANES

    def full(a):
        nd = a.ndim
        return pl.BlockSpec(a.shape, lambda i, j, _nd=nd: (0,) * _nd)

    nb = MIXER_SEQS
    seq_tile = lambda i, j: (i, j, 0)
    x1, h2, route, route_t, counts = pl.pallas_call(
        functools.partial(_mixer_kernel, pool_w=pool_w, ssd_w=ssd_w),
        grid=(b // nb, s // ts),
        in_specs=[pl.BlockSpec((nb, ts, d), seq_tile),
                  pl.BlockSpec((nb, 6, d), lambda i, j: (i, 0, 0)),
                  full(win), full(wp), full(ps), full(cw), full(cb), full(dtb), full(alog),
                  full(dsk), full(nw), full(wout), full(wrt), full(br), full(rep), full(tril3)],
        out_specs=[pl.BlockSpec((nb, ts, d), seq_tile),
                   pl.BlockSpec((nb, ts * nrow, LANES), seq_tile),
                   pl.BlockSpec((nb, ts, LANES), seq_tile),
                   pl.BlockSpec((nb, SUBLANES, ts), lambda i, j: (i, 0, j)),
                   pl.BlockSpec((LANES, ts), lambda i, j: (0, 0))],
        out_shape=[jax.ShapeDtypeStruct((b, s, d), F32),
                   jax.ShapeDtypeStruct((b, s * nrow, LANES), F32),
                   jax.ShapeDtypeStruct((b, s, LANES), F32),
                   jax.ShapeDtypeStruct((b, SUBLANES, s), F32),
                   jax.ShapeDtypeStruct((LANES, ts), F32)],
        scratch_shapes=[pltpu.VMEM((nb, ssd_w // LANES, SSD_STATE, LANES), F32),
                        pltpu.VMEM((nb, conv_dim // LANES, (ts // SUBLANES + 2 * CONV_PAD) * CONV_PITCH, LANES),
                                   F32),
                        pltpu.VMEM((nb, POOL_HALO + ts, pool_w), F32),
                        pltpu.VMEM((nb, ts, conv_dim), F32),
                        pltpu.VMEM((nb, ts, d), BF16)],
        compiler_params=pltpu.CompilerParams(dimension_semantics=("arbitrary", "arbitrary"),
                                             vmem_limit_bytes=VMEM_LIMIT_BYTES),
        name="mixer",
    )(x, mod, win, wp, ps, cw, cb, dtb, alog, dsk, nw, wout, wrt, br, rep, tril3)
    return x1, h2.reshape(b * s * nrow, LANES), route.reshape(b * s, LANES), route_t, counts


def _rank_kernel(route_ref, start_ref, dest_ref, carry_ref):
    ta = route_ref.shape[2]

    @pl.when(pl.program_id(0) == 0)
    def _():
        carry_ref[...] = start_ref[...]

    route_t = route_ref[0]
    e_idx = lax.broadcasted_iota(jnp.int32, (LANES, ta), 0).astype(F32)
    ohs = [jnp.where(e_idx == route_t[k:k + 1, :], 1.0, 0.0) for k in range(TOP_K)]
    oh = (ohs[0] + ohs[1]).astype(BF16)
    earlier = (lax.broadcasted_iota(jnp.int32, (ta, ta), 0)
               < lax.broadcasted_iota(jnp.int32, (ta, ta), 1))
    before = _dot(oh, jnp.where(earlier, 1.0, 0.0).astype(BF16))
    carry = carry_ref[...]
    base = jnp.concatenate([carry] * (ta // LANES), axis=1) + before
    for k in range(TOP_K):
        dest_ref[0, k:k + 1, :] = jnp.sum(ohs[k] * base, axis=0, keepdims=True).astype(jnp.int32)
    carry_ref[...] = carry + _dot(oh, jnp.ones((ta, LANES), BF16))


def _rank(route_t, start_rep):
    b, _, s = route_t.shape
    spt = s // ROW_COPY_TILE
    t = b * s
    return pl.pallas_call(
        _rank_kernel,
        grid=(t // ROW_COPY_TILE,),
        in_specs=[pl.BlockSpec((1, SUBLANES, ROW_COPY_TILE), lambda i: (i // spt, 0, i % spt)),
                  pl.BlockSpec((LANES, LANES), lambda i: (0, 0))],
        out_specs=pl.BlockSpec((1, TOP_K, ROW_COPY_TILE), lambda i: (i, 0, 0)),
        out_shape=jax.ShapeDtypeStruct((t // ROW_COPY_TILE, TOP_K, ROW_COPY_TILE), jnp.int32),
        scratch_shapes=[pltpu.VMEM((LANES, LANES), F32)],
        compiler_params=pltpu.CompilerParams(dimension_semantics=("arbitrary",)),
        name="rank",
    )(route_t, start_rep)


def _dispatch_kernel(dest_ref, h2_ref, xs_hbm, sem, *, nrow):
    tile_rows = h2_ref.shape[0]

    part_tokens = dest_ref.shape[2]
    for part in range(dest_ref.shape[0]):
        for tk in range(part_tokens):
            for k in range(TOP_K):
                pltpu.make_async_copy(_slab_of_row(h2_ref, part * part_tokens + tk, nrow),
                                      _slab_of_row(xs_hbm, dest_ref[part, k, tk], nrow),
                                      sem).start(priority=(tk * TOP_K + k) % DMA_THREADS)

    for _ in range(TOP_K):
        pltpu.make_async_copy(h2_ref, xs_hbm.at[pl.ds(0, tile_rows)], sem).wait()


def _dispatch(dest, h2, nrow):
    t = h2.shape[0] // nrow
    return pl.pallas_call(
        functools.partial(_dispatch_kernel, nrow=nrow),
        grid=(t // (DISPATCH_PARTS * ROW_COPY_TILE),),
        in_specs=[pl.BlockSpec((DISPATCH_PARTS, TOP_K, ROW_COPY_TILE), lambda i: (i, 0, 0),
                               memory_space=pltpu.SMEM),
                  pl.BlockSpec((DISPATCH_PARTS * ROW_COPY_TILE * nrow, LANES), lambda i: (i, 0))],
        out_specs=pl.BlockSpec(memory_space=pl.ANY),
        out_shape=jax.ShapeDtypeStruct((t * TOP_K * nrow, LANES), F32),
        scratch_shapes=[pltpu.SemaphoreType.DMA(())],
        compiler_params=pltpu.CompilerParams(dimension_semantics=("arbitrary",),
                                             has_side_effects=True),
        name="dispatch",
    )(dest, h2)


def _expert_kernel(blk_ref, exp_ref, lo_ref, hi_ref, first_ref, new_ref, slot_ref, next_ref,
                   xs_ref, w13_hbm, w2_hbm, ys_ref, w13_f32, w2_f32, w13_bf, w2_bf, sems, *, nrow):
    i = pl.program_id(0)
    rows = xs_ref.shape[0] // nrow
    lo = lo_ref[i]
    hi = hi_ref[i]

    sub = rows // EXPERT_CHAINS

    def weight_copies(expert, slot):
        return (pltpu.make_async_copy(w13_hbm.at[expert], w13_f32.at[slot], sems.at[slot]),
                pltpu.make_async_copy(w2_hbm.at[expert], w2_f32.at[slot], sems.at[slot]))

    @pl.when(i == 0)
    def _():
        for cp in weight_copies(exp_ref[0], slot_ref[0]):
            cp.start()

    @pl.when(new_ref[i] == 1)
    def _():
        slot = slot_ref[i]
        for cp in weight_copies(exp_ref[i], slot):
            cp.wait()

        @pl.when(next_ref[i] >= 0)
        def _():
            for cp in weight_copies(next_ref[i], 1 - slot):
                cp.start()

        w13_bf[...] = w13_f32[slot].astype(BF16)
        w2_bf[...] = w2_f32[slot].astype(BF16)

    def chain(r0):
        xb = jnp.concatenate([xs_ref[pl.ds(r0 * nrow + c, sub, stride=nrow), :].astype(BF16)
                              for c in range(nrow)], axis=1)
        yield
        hu = _dot(xb, w13_bf[...])
        yield
        f = hu.shape[1] // 2
        act = (_silu(hu[:, :f]) * hu[:, f:]).astype(BF16)
        yield
        y = _dot(act, w2_bf[...])
        yield
        ridx = r0 + lax.broadcasted_iota(jnp.int32, (sub, 1), 0)
        take = (ridx >= lo_eff) & (ridx < hi_eff)
        for c in range(nrow):
            sl = pl.ds(r0 * nrow + c, sub, stride=nrow)
            ys_ref[sl, :] = jnp.where(take, y[:, c * LANES:(c + 1) * LANES], ys_ref[sl, :])

    is_first = first_ref[i] == 1
    lo_eff = jnp.where(is_first, 0, lo)
    hi_eff = jnp.where(is_first, rows, hi)

    @pl.when(hi > lo)
    def _():
        _run_skewed([chain(k * sub) for k in range(EXPERT_CHAINS)])


def _experts(items, xs, w13, w2, nrow):
    a = xs.shape[0] // nrow
    n_items = items[0].shape[0]
    d = w13.shape[1]
    ff2 = w13.shape[2]
    grid_spec = pltpu.PrefetchScalarGridSpec(
        num_scalar_prefetch=len(items),
        grid=(n_items,),
        in_specs=[pl.BlockSpec((EXPERT_BLOCK * nrow, LANES), lambda i, b, *_: (b[i], 0)),
                  pl.BlockSpec(memory_space=pl.ANY),
                  pl.BlockSpec(memory_space=pl.ANY)],
        out_specs=pl.BlockSpec((EXPERT_BLOCK * nrow, LANES), lambda i, b, *_: (b[i], 0)),
        scratch_shapes=[pltpu.VMEM((2, d, ff2), F32), pltpu.VMEM((2, ff2 // 2, d), F32),
                        pltpu.VMEM((d, ff2), BF16), pltpu.VMEM((ff2 // 2, d), BF16),
                        pltpu.SemaphoreType.DMA((2,))],
    )
    return pl.pallas_call(
        functools.partial(_expert_kernel, nrow=nrow),
        grid_spec=grid_spec,
        out_shape=jax.ShapeDtypeStruct((a * nrow, LANES), F32),
        compiler_params=pltpu.CompilerParams(dimension_semantics=("arbitrary",),
                                             vmem_limit_bytes=VMEM_LIMIT_BYTES),
        name="experts",
    )(*items, xs, w13, w2)


def _final_kernel(dest_ref, dnext_ref, x1_ref, route_ref, mod_ref, fw_ref, ys_hbm, o_ref,
                  buf_even, buf_odd, sems):
    tf = x1_ref.shape[1]
    nrow = buf_even.shape[1] // tf
    step = pl.program_id(0)
    last = step + 1 == pl.num_programs(0)

    def start_gathers(d_ref, buf, sem, t0, t1):
        for tk in range(t0, t1):
            for k in range(TOP_K):
                pltpu.make_async_copy(_slab_of_row(ys_hbm, d_ref[0, k, tk], nrow),
                                      _slab_of_row(buf.at[k], tk, nrow),
                                      sem).start(priority=(tk * TOP_K + k) % DMA_THREADS)

    def wait_gathers(buf, sem):
        for k in range(TOP_K):
            pltpu.make_async_copy(ys_hbm.at[pl.ds(0, tf * nrow)], buf.at[k], sem).wait()

    @pl.when(step == 0)
    def _():
        start_gathers(dest_ref, buf_even, sems.at[0], 0, tf)

    def body(cur, cur_sem, nxt, nxt_sem):
        wait_gathers(cur, cur_sem)
        g2 = mod_ref[0][5:6]
        for r0 in range(0, tf, NORM_ROWS):
            start_gathers(dnext_ref, nxt, nxt_sem, r0, r0 + NORM_ROWS)
            rows = slice(r0, r0 + NORM_ROWS)
            gates = route_ref[rows, :]
            y = jnp.concatenate(
                [cur[0, pl.ds(r0 * nrow + c, NORM_ROWS, stride=nrow), :] * gates[:, 2:3]
                 + cur[1, pl.ds(r0 * nrow + c, NORM_ROWS, stride=nrow), :] * gates[:, 3:4]
                 for c in range(nrow)], axis=1)
            o_ref[0, rows, :] = _rms(x1_ref[0, rows, :] + g2 * y) * fw_ref[...]

        @pl.when(last)
        def _():
            wait_gathers(nxt, nxt_sem)

    @pl.when(lax.rem(step, 2) == 0)
    def _():
        body(buf_even, sems.at[0], buf_odd, sems.at[1])

    @pl.when(lax.rem(step, 2) == 1)
    def _():
        body(buf_odd, sems.at[1], buf_even, sems.at[0])


def _final(dest, x1, route, mod, fw, ys):
    b, s, d = x1.shape
    tf = ROW_COPY_TILE
    nrow = d // LANES
    spt = s // tf
    n_steps = b * spt
    return pl.pallas_call(
        _final_kernel,
        grid=(n_steps,),
        in_specs=[pl.BlockSpec((1, TOP_K, tf), lambda i: (i, 0, 0), memory_space=pltpu.SMEM),
                  pl.BlockSpec((1, TOP_K, tf), lambda i: (jnp.minimum(i + 1, n_steps - 1), 0, 0),
                               memory_space=pltpu.SMEM),
                  pl.BlockSpec((1, tf, d), lambda i: (i // spt, i % spt, 0)),
                  pl.BlockSpec((tf, LANES), lambda i: (i, 0)),
                  pl.BlockSpec((1, 6, d), lambda i: (i // spt, 0, 0)),
                  pl.BlockSpec((1, d), lambda i: (0, 0)),
                  pl.BlockSpec(memory_space=pl.ANY)],
        out_specs=pl.BlockSpec((1, tf, d), lambda i: (i // spt, i % spt, 0)),
        out_shape=jax.ShapeDtypeStruct((b, s, d), F32),
        scratch_shapes=[pltpu.VMEM((TOP_K, tf * nrow, LANES), F32),
                        pltpu.VMEM((TOP_K, tf * nrow, LANES), F32),
                        pltpu.SemaphoreType.DMA((2,))],
        compiler_params=pltpu.CompilerParams(dimension_semantics=("arbitrary",),
                                             vmem_limit_bytes=VMEM_LIMIT_BYTES),
        name="final",
    )(dest, dest, x1, route, mod, fw, ys)


def _pad_lanes(v, fill=0.0):
    n = v.shape[-1]
    return jnp.pad(v, [(0, 0)] * (v.ndim - 1) + [(0, LANES - n)], constant_values=fill)


def _work_items(counts, n_blocks):
    n_items = n_blocks + N_EXPERTS - 1
    ends = jnp.cumsum(counts)
    starts = ends - counts
    first_blk = starts // EXPERT_BLOCK
    last_blk = jnp.maximum(ends - 1, starts) // EXPERT_BLOCK
    n_e = jnp.where(counts > 0, last_blk - first_blk + 1, 0)
    item_end = jnp.cumsum(n_e)
    item_start = item_end - n_e
    ids = jnp.arange(n_items, dtype=jnp.int32)
    total = item_end[-1]
    ids_c = jnp.minimum(ids, total - 1)
    e = jnp.sum((item_end[None, :] <= ids_c[:, None]).astype(jnp.int32), axis=1)
    onehot = (e[:, None] == jnp.arange(N_EXPERTS, dtype=jnp.int32)[None, :]).astype(jnp.int32)
    pick = lambda v: jnp.sum(onehot * v[None, :], axis=1)
    blk = pick(first_blk) + ids_c - pick(item_start)
    lo = jnp.clip(pick(starts) - blk * EXPERT_BLOCK, 0, EXPERT_BLOCK)
    hi = jnp.clip(pick(ends) - blk * EXPERT_BLOCK, 0, EXPERT_BLOCK)
    hi = jnp.where(ids < total, hi, lo)
    prev_blk = jnp.concatenate([jnp.full((1,), -1, jnp.int32), blk[:-1]])
    first = (blk != prev_blk).astype(jnp.int32)
    prev_e = jnp.concatenate([jnp.full((1,), -1, jnp.int32), e[:-1]])
    new_expert = (e != prev_e).astype(jnp.int32)
    has_rows = (counts > 0).astype(jnp.int32)
    ordinal = jnp.cumsum(has_rows) - has_rows
    later = (jnp.arange(N_EXPERTS)[None, :] > jnp.arange(N_EXPERTS)[:, None]) & (counts[None, :] > 0)
    next_of = jnp.where(jnp.any(later, axis=1), jnp.argmax(later, axis=1), -1).astype(jnp.int32)
    return blk, e, lo, hi, first, new_expert, pick(ordinal) % 2, pick(next_of)


def kernel(x, c, w_ada, b_ada, w_in, w_pool, pool_scale, conv_w, conv_b, dt_bias, a_log, d_skip,
           ssd_norm_w, w_out, w_group, b_group, w_router, b_router, w13, w2, final_norm_w):
    b, s, d = x.shape
    depth = w_ada.shape[0]
    t = b * s
    pool_w = w_pool.shape[1] * w_pool.shape[2]
    ssd_w = SSD_HEADS * SSD_HEAD_DIM
    conv_dim = conv_w.shape[2]
    assert s % SEQ_TILE == 0 and SEQ_TILE % SSD_CHUNK == 0 and b % MIXER_SEQS == 0
    assert s % ROW_COPY_TILE == 0 and ROW_COPY_TILE % LANES == 0
    assert t % (DISPATCH_PARTS * ROW_COPY_TILE) == 0 and SEQ_TILE % NORM_ROWS == 0
    assert (t * TOP_K) % EXPERT_BLOCK == 0 and d % LANES == 0
    assert pool_w == 2 * LANES and len(POOL_WINDOWS) * POOL_GROUP_DIM == pool_w
    assert N_EXPERT_GROUPS + N_EXPERTS <= LANES and SSD_HEADS <= LANES
    assert depth == 1, "the final RMSNorm is fused into the last layer's combine step"

    for layer in range(depth):
        mod = _ada(c, w_ada[layer], b_ada[layer]).reshape(b, 6, d)

        o_dt = pool_w + ssd_w + conv_dim
        win = jnp.pad(w_in[layer], ((0, 0), (0, o_dt + LANES - w_in.shape[2]))).astype(BF16)
        wp = jnp.zeros((pool_w, pool_w), F32)
        for g in range(len(POOL_WINDOWS)):
            sl = slice(g * POOL_GROUP_DIM, (g + 1) * POOL_GROUP_DIM)
            wp = wp.at[sl, sl].set(w_pool[layer, g])
        wp = wp.astype(BF16)
        wrt = _pad_lanes(jnp.concatenate([w_group[layer], w_router[layer]], axis=1)).T.astype(BF16)
        b_route = _pad_lanes(jnp.concatenate([b_group[layer], b_router[layer]])[None, :])
        br = jnp.broadcast_to(b_route.reshape(LANES, 1), (LANES, LANES))

        x1, h2, route, route_t, counts_part = _mixer(
            x, mod, win, wp, pool_scale[layer][None, :], conv_w[layer], conv_b[layer][None, :],
            _pad_lanes(dt_bias[layer][None, :]), _pad_lanes(a_log[layer][None, :], fill=NEG_BIG),
            jnp.repeat(d_skip[layer], SSD_HEAD_DIM)[None, :], ssd_norm_w[layer][None, :],
            w_out[layer].astype(BF16), wrt, br, pool_w=pool_w, ssd_w=ssd_w)

        counts_f = jnp.sum(counts_part, axis=1, keepdims=True)
        start_rep = jnp.broadcast_to(jnp.cumsum(counts_f, axis=0) - counts_f, (LANES, LANES))
        dest = _rank(route_t, start_rep)
        counts = counts_f[:N_EXPERTS, 0].astype(jnp.int32)
        items = _work_items(counts, (t * TOP_K) // EXPERT_BLOCK)

        nrow = d // LANES
        xs = _dispatch(dest, h2, nrow)
        ys = _experts(items, xs, w13[layer], w2[layer], nrow)
        x = _final(dest, x1, route, mod, final_norm_w[None, :], ys)
    return x
```

```python
import functools

import jax
import jax.numpy as jnp
from jax import lax
from jax.experimental import pallas as pl
from jax.experimental.pallas import tpu as pltpu

POOL_WINDOWS = (2, 4, 8, 16)
POOL_GROUP_DIM = 64
SSD_HEAD_DIM = 64
SSD_GROUPS = 4
SSD_HEADS_PER_GROUP = 3
SSD_HEADS = SSD_GROUPS * SSD_HEADS_PER_GROUP
SSD_STATE = 128
SSD_CONV = 4
SSD_CHUNK = 128
N_EXPERT_GROUPS = 4
EXPERTS_PER_GROUP = 8
N_EXPERTS = N_EXPERT_GROUPS * EXPERTS_PER_GROUP
TOP_K = 2
NORM_EPS = 1e-6

LANES = 128
SUBLANES = 8
VMEM_LIMIT_BYTES = 56 * 1024 * 1024

SEQ_TILE = 256
MIXER_SEQS = 2
PROJ_PIECE = 1024
NORM_ROWS = 64
CONV_PAD = SUBLANES
CONV_PITCH = SUBLANES + 1
POOL_HALO = 16
ROW_COPY_TILE = 512
EXPERT_BLOCK = 512
EXPERT_CHAINS = 2
DISPATCH_PARTS = 4
DMA_THREADS = 2

NEG_BIG = -1e30
F32 = jnp.float32
BF16 = jnp.bfloat16


def _silu(v):
    half = 0.5 * v
    return half + half * jnp.tanh(half)


def _softplus(v):
    return jnp.maximum(v, 0.0) + jnp.log(1.0 + jnp.exp(-jnp.abs(v)))


def _rms(v):
    return v * lax.rsqrt(jnp.mean(v * v, axis=-1, keepdims=True) + NORM_EPS)


def _dot(a, b):
    return jnp.dot(a, b, preferred_element_type=F32)


def _run_skewed(chains):
    live = [False] * len(chains)
    started = 0
    while started < len(chains) or any(live):
        if started < len(chains):
            live[started] = True
            started += 1
        for k, chain in enumerate(chains):
            if live[k]:
                try:
                    next(chain)
                except StopIteration:
                    live[k] = False


def _slab_of_row(ref, row, nrow):
    start = row * nrow
    return ref.at[pl.ds(start if isinstance(start, int) else pl.multiple_of(start, nrow), nrow)]


def _ada_kernel(c_ref, w_ref, b_ref, o_ref):
    o_ref[...] = _dot(_silu(c_ref[...]).astype(BF16), w_ref[...].astype(BF16)) + b_ref[...]


def _ada(c, w_ada, b_ada):
    b, d = c.shape
    n = w_ada.shape[1]
    return pl.pallas_call(
        _ada_kernel,
        grid=(n // d,),
        in_specs=[pl.BlockSpec((b, d), lambda i: (0, 0)),
                  pl.BlockSpec((d, d), lambda i: (0, i)),
                  pl.BlockSpec((1, d), lambda i: (0, i))],
        out_specs=pl.BlockSpec((b, d), lambda i: (0, i)),
        out_shape=jax.ShapeDtypeStruct((b, n), F32),
        compiler_params=pltpu.CompilerParams(dimension_semantics=("arbitrary",),
                                             vmem_limit_bytes=VMEM_LIMIT_BYTES),
        name="ada",
    )(c, w_ada, b_ada.reshape(1, n))


def _mixer_kernel(x_ref, mod_ref, win_ref, wp_ref, ps_ref, cw_ref, cb_ref, dtb_ref, alog_ref,
                  dsk_ref, nw_ref, wout_ref, wrt_ref, br_ref, rep_ref, tril3_ref,
                  x1_ref, h2_ref, route_ref, rt_ref, cnt_ref,
                  state_ref, xext_ref, uext_ref, xa_ref, hb_ref, *, pool_w, ssd_w):
    @pl.when((pl.program_id(0) == 0) & (pl.program_id(1) == 0))
    def _():
        cnt_ref[...] = jnp.zeros_like(cnt_ref)
        xext_ref[...] = jnp.zeros_like(xext_ref)

    @pl.when(pl.program_id(1) == 0)
    def _():
        state_ref[...] = jnp.zeros_like(state_ref)
        pc = x_ref.shape[1] // SUBLANES
        for bi in range(xext_ref.shape[0]):
            for lt in range(xext_ref.shape[1]):
                xext_ref[bi, lt, pl.ds(CONV_PAD * CONV_PITCH, pc, stride=CONV_PITCH), :] = (
                    jnp.zeros((pc, LANES), F32))
        uext_ref[:, 0:POOL_HALO, :] = jnp.zeros((uext_ref.shape[0], POOL_HALO, uext_ref.shape[2]), F32)

    _run_skewed([_mixer_tile(x_ref.at[bi], mod_ref.at[bi], win_ref, wp_ref, ps_ref, cw_ref, cb_ref, dtb_ref,
                             alog_ref, dsk_ref, nw_ref, wout_ref, wrt_ref, br_ref, rep_ref, tril3_ref,
                             x1_ref.at[bi], h2_ref.at[bi], route_ref.at[bi], rt_ref.at[bi], cnt_ref,
                             state_ref.at[bi], xext_ref.at[bi], uext_ref.at[bi], xa_ref.at[bi], hb_ref.at[bi], pool_w=pool_w, ssd_w=ssd_w)
                 for bi in range(x_ref.shape[0])])


def _mixer_tile(x_ref, mod_ref, win_ref, wp_ref, ps_ref, cw_ref, cb_ref, dtb_ref, alog_ref,
                dsk_ref, nw_ref, wout_ref, wrt_ref, br_ref, rep_ref, tril3_ref,
                x1_ref, h2_ref, route_ref, rt_ref, cnt_ref,
                state_ref, xext_ref, uext_ref, xa_ref, hb_ref, *, pool_w, ssd_w):
    ts = x_ref.shape[0]
    d = x_ref.shape[1]
    L = SSD_CHUNK
    N = SSD_STATE
    P = SSD_HEAD_DIM
    j = pl.program_id(1)

    mod = mod_ref[...]
    sh1, sc1, g1 = mod[0:1], mod[1:2], mod[2:3]
    sh2, sc2 = mod[3:4], mod[4:5]

    for r in range(0, ts, NORM_ROWS):
        hb_ref[r:r + NORM_ROWS, :] = (_rms(x_ref[r:r + NORM_ROWS, :]) * (1.0 + sc1) + sh1).astype(BF16)
    h = hb_ref[...]
    yield
    n_proj = win_ref.shape[1]
    edges = [min(n_proj, e * PROJ_PIECE) for e in range(pl.cdiv(n_proj, PROJ_PIECE) + 1)]
    pieces = []
    for a, b in zip(edges[:-1], edges[1:]):
        pieces.append(_dot(h, win_ref[:, a:b]))
        yield
    proj = jnp.concatenate(pieces, axis=1)
    o_z = pool_w
    o_x = pool_w + ssd_w
    conv_dim = ssd_w + 2 * SSD_GROUPS * N
    o_dt = o_x + conv_dim
    u = proj[:, 0:pool_w]
    z = proj[:, o_z:o_x]
    xbc = proj[:, o_x:o_dt]
    dtr = proj[:, o_dt:o_dt + LANES]

    uext_ref[POOL_HALO:POOL_HALO + ts, :] = u
    lane_u = lax.broadcasted_iota(jnp.int32, (1, LANES), 1)
    upper = lane_u >= POOL_GROUP_DIM

    def ush(k, col):
        return uext_ref[POOL_HALO - k:POOL_HALO - k + ts, col * LANES:(col + 1) * LANES]

    w0, w1, w2, w3 = POOL_WINDOWS
    lo_a = ush(0, 0)
    for k in range(1, w0):
        lo_a = lo_a + ush(k, 0)
    lo_b = ush(w0, 0)
    for k in range(w0 + 1, w1):
        lo_b = lo_b + ush(k, 0)
    hi_a = ush(0, 1)
    for k in range(1, w2):
        hi_a = hi_a + ush(k, 1)
    hi_b = ush(w2, 1)
    for k in range(w2 + 1, w3):
        hi_b = hi_b + ush(k, 1)
    win_lo = lo_a + jnp.where(upper, lo_b, 0.0)
    win_hi = hi_a + jnp.where(upper, hi_b, 0.0)
    pos = (j * ts + 1 + lax.broadcasted_iota(jnp.int32, (ts, 1), 0)).astype(F32)
    cnt_lo = jnp.minimum(pos, jnp.where(upper, float(w1), float(w0)))
    cnt_hi = jnp.minimum(pos, jnp.where(upper, float(w3), float(w2)))
    pdiff = jnp.concatenate([win_lo / cnt_lo, win_hi / cnt_hi], axis=1) - u
    y_pool = _dot(pdiff.astype(BF16), wp_ref[...]) * ps_ref[...]
    uext_ref[0:POOL_HALO, :] = uext_ref[ts:ts + POOL_HALO, :]

    a_neg = -jnp.exp(alog_ref[...])
    dt_all = _softplus(dtr + dtb_ref[...])
    da = dt_all * a_neg
    d1 = da.astype(BF16)
    r1 = da - d1.astype(F32)
    d2 = r1.astype(BF16)
    d3 = (r1 - d2.astype(F32)).astype(BF16)
    a_cs = _dot(tril3_ref[...], jnp.concatenate([d1, d2, d3], axis=0))
    a_hi = a_cs.astype(BF16)
    a_lo = (a_cs - a_hi.astype(F32)).astype(BF16)
    a_q_all = a_hi.astype(F32) + a_lo.astype(F32)
    a_rep_all = _dot(jnp.concatenate([a_hi, a_lo], axis=1), rep_ref[...])
    yield

    pc = ts // SUBLANES
    n_lt = xext_ref.shape[0]
    for lt in range(n_lt):
        for col in range(SUBLANES):
            xext_ref[lt, pl.ds(CONV_PAD * CONV_PITCH + col + 1, pc, stride=CONV_PITCH), :] = (
                xbc[col * pc:(col + 1) * pc, lt * LANES:(lt + 1) * LANES])
    yield
    sub_i = lax.broadcasted_iota(jnp.int32, (SUBLANES, 1), 0)
    for lt in range(n_lt):
        lanes = slice(lt * LANES, (lt + 1) * LANES)
        cw = cw_ref[:, lanes]
        for col in range(SUBLANES):
            acc = cb_ref[:, lanes] + cw[SSD_CONV - 1:SSD_CONV] * xbc[col * pc:(col + 1) * pc, lanes]
            for k in range(1, SSD_CONV):
                main = xext_ref[lt, pl.ds((CONV_PAD - k) * CONV_PITCH + col + 1, pc, stride=CONV_PITCH), :]
                prev = xext_ref[lt, pl.ds((CONV_PAD + pc - k) * CONV_PITCH + col, SUBLANES,
                                          stride=CONV_PITCH), :]
                tap = jnp.concatenate([jnp.where(sub_i < k, prev, main[0:SUBLANES]), main[SUBLANES:]], axis=0)
                acc = acc + cw[SSD_CONV - 1 - k:SSD_CONV - k] * tap
            xa_ref[col * pc:(col + 1) * pc, lanes] = _silu(acc)
    for lt in range(n_lt):
        xext_ref[lt, pl.ds(CONV_PAD * CONV_PITCH, pc, stride=CONV_PITCH), :] = (
            xext_ref[lt, pl.ds(CONV_PAD * CONV_PITCH + SUBLANES, pc, stride=CONV_PITCH), :])
    yield

    row_i = lax.broadcasted_iota(jnp.int32, (L, L), 0)
    col_i = lax.broadcasted_iota(jnp.int32, (L, L), 1)
    causal = row_i >= col_i
    hpt = LANES // P
    lane_h = lax.broadcasted_iota(jnp.int32, (1, LANES), 1)
    y_chunks = []
    for c in range(ts // L):
        r0 = c * L
        xs_c = xa_ref[r0:r0 + L,0:ssd_w]
        b_c = xa_ref[r0:r0 + L,ssd_w:ssd_w + SSD_GROUPS * N]
        c_c = xa_ref[r0:r0 + L,ssd_w + SSD_GROUPS * N:conv_dim]
        a_rep = a_rep_all[r0:r0 + L]
        a_q_t = a_q_all[r0:r0 + L].T
        dt_t = dt_all[r0:r0 + L].T
        cbs, cgs, bg_ts = [], [], []
        for g in range(SSD_GROUPS):
            bg = b_c[:, g * N:(g + 1) * N]
            cg = c_c[:, g * N:(g + 1) * N]
            cbs.append(lax.dot_general(cg.astype(BF16), bg.astype(BF16), (((1,), (1,)), ((), ())),
                                       preferred_element_type=F32))
            cgs.append(cg)
            bg_ts.append(bg.T)
        y_tiles = []
        for q in range(SSD_HEADS // hpt):
            xs_q = xs_c[:, q * LANES:(q + 1) * LANES].astype(BF16)
            st = state_ref[q]
            rhs = jnp.concatenate([xs_q, st.astype(BF16)], axis=0)
            y_q = upd_q = keep_q = None
            for hh in range(hpt):
                hd = q * hpt + hh
                g = hd // SSD_HEADS_PER_GROUP
                a_col = a_rep[:, hd * LANES:(hd + 1) * LANES]
                a_row = a_q_t[hd:hd + 1, :]
                dt_row = dt_t[hd:hd + 1, :]
                decay = jnp.exp(jnp.where(causal, a_col - a_row, NEG_BIG))
                m = cbs[g] * decay * dt_row
                cs = cgs[g] * jnp.exp(a_col)
                lhs = jnp.concatenate([m, cs], axis=1).astype(BF16)
                y_h = _dot(lhs, rhs)
                a_end = a_row[:, L - 1:L]
                w_row = dt_row * jnp.exp(a_end - a_row)
                upd_h = _dot((bg_ts[g] * w_row).astype(BF16), xs_q)
                keep_h = jnp.exp(a_end)
                if hh == 0:
                    y_q, upd_q, keep_q = y_h, upd_h, keep_h
                else:
                    mine = (lane_h >= hh * P) & (lane_h < (hh + 1) * P)
                    y_q = jnp.where(mine, y_h, y_q)
                    upd_q = jnp.where(mine, upd_h, upd_q)
                    keep_q = jnp.where(mine, keep_h, keep_q)
            y_tiles.append(y_q)
            state_ref[q] = keep_q * st + upd_q
        y_chunks.append(jnp.concatenate(y_tiles, axis=1))
        yield
    y_all = jnp.concatenate(y_chunks, axis=0) if len(y_chunks) > 1 else y_chunks[0]

    gw = ssd_w // SSD_GROUPS
    lane_y = lax.broadcasted_iota(jnp.int32, (1, ssd_w), 1)
    for r in range(0, ts, NORM_ROWS):
        rows = slice(r, r + NORM_ROWS)
        y = (y_all[rows] + xa_ref[rows, 0:ssd_w] * dsk_ref[...]) * _silu(z[rows])
        y2 = y * y
        scale = jnp.zeros_like(y)
        for g in range(SSD_GROUPS):
            in_g = (lane_y >= g * gw) & (lane_y < (g + 1) * gw)
            ms = jnp.sum(jnp.where(in_g, y2, 0.0), axis=-1, keepdims=True) * (1.0 / gw)
            scale = scale + jnp.where(in_g, lax.rsqrt(ms + NORM_EPS), 0.0)
        hb_ref[rows, :] = jnp.concatenate([y_pool[rows], y * scale * nw_ref[...]], axis=1).astype(BF16)
    yield
    mix = _dot(hb_ref[...], wout_ref[...])
    yield

    nrow = d // LANES
    for r in range(0, ts, NORM_ROWS):
        rows = slice(r, r + NORM_ROWS)
        x1 = x_ref[rows, :] + g1 * mix[rows]
        x1_ref[rows, :] = x1
        h2 = _rms(x1) * (1.0 + sc2) + sh2
        for c in range(nrow):
            h2_ref[pl.ds(r * nrow + c, NORM_ROWS, stride=nrow), :] = h2[:, c * LANES:(c + 1) * LANES]
        hb_ref[rows, :] = h2.astype(BF16)
    h_hi = hb_ref[...]
    yield
    logits = lax.dot_general(wrt_ref[...], h_hi, (((1,), (1,)), ((), ())), preferred_element_type=F32)
    logits = logits + jnp.concatenate([br_ref[...]] * (ts // LANES), axis=1)
    row = lax.broadcasted_iota(jnp.int32, (LANES, ts), 0).astype(F32)
    big = float(LANES)
    gl = jnp.where(row < N_EXPERT_GROUPS, logits, NEG_BIG)
    gmax = jnp.max(gl, axis=0, keepdims=True)
    gsum = jnp.sum(jnp.exp(gl - gmax), axis=0, keepdims=True)
    p_g = 1.0 / gsum
    g_idx = jnp.min(jnp.where(gl == gmax, row, big), axis=0, keepdims=True)
    lo = N_EXPERT_GROUPS + EXPERTS_PER_GROUP * g_idx
    el = jnp.where((row >= lo) & (row < lo + EXPERTS_PER_GROUP), logits, NEG_BIG)
    v1 = jnp.max(el, axis=0, keepdims=True)
    i1 = jnp.min(jnp.where(el == v1, row, big), axis=0, keepdims=True)
    el2 = jnp.where(row == i1, NEG_BIG, el)
    v2 = jnp.max(el2, axis=0, keepdims=True)
    i2 = jnp.min(jnp.where(el2 == v2, row, big), axis=0, keepdims=True)
    e21 = jnp.exp(v2 - v1)
    gate1 = p_g / (1.0 + e21)
    gate2 = p_g * e21 / (1.0 + e21)
    e1 = i1 - N_EXPERT_GROUPS
    e2 = i2 - N_EXPERT_GROUPS
    route_t = jnp.where(row == 0, e1, jnp.where(row == 1, e2,
                                                jnp.where(row == 2, gate1, jnp.where(row == 3, gate2, 0.0))))
    rt_ref[...] = route_t[0:SUBLANES]
    route_ref[...] = route_t.T
    cnt_ref[...] += jnp.where((row == e1) | (row == e2), 1.0, 0.0)


def _mixer(x, mod, win, wp, ps, cw, cb, dtb, alog, dsk, nw, wout, wrt, br, *, pool_w, ssd_w):
    k_idx = jnp.arange(2 * LANES, dtype=jnp.int32)[:, None] % LANES
    h_idx = jnp.arange(SSD_HEADS * LANES, dtype=jnp.int32)[None, :] // LANES
    rep = (k_idx == h_idx).astype(BF16)
    r_idx = jnp.arange(SEQ_TILE, dtype=jnp.int32)[:, None]
    c_idx = jnp.arange(3 * SEQ_TILE, dtype=jnp.int32)[None, :] % SEQ_TILE
    tril3 = ((r_idx >= c_idx) & (r_idx // SSD_CHUNK == c_idx // SSD_CHUNK)).astype(BF16)
    b, s, d = x.shape
    ts = SEQ_TILE
    conv_dim = cw.shape[1]
    nrow = d // LANES

    def full(a):
        nd = a.ndim
        return pl.BlockSpec(a.shape, lambda i, j, _nd=nd: (0,) * _nd)

    nb = MIXER_SEQS
    seq_tile = lambda i, j: (i, j, 0)
    x1, h2, route, route_t, counts = pl.pallas_call(
        functools.partial(_mixer_kernel, pool_w=pool_w, ssd_w=ssd_w),
        grid=(b // nb, s // ts),
        in_specs=[pl.BlockSpec((nb, ts, d), seq_tile),
                  pl.BlockSpec((nb, 6, d), lambda i, j: (i, 0, 0)),
                  full(win), full(wp), full(ps), full(cw), full(cb), full(dtb), full(alog),
                  full(dsk), full(nw), full(wout), full(wrt), full(br), full(rep), full(tril3)],
        out_specs=[pl.BlockSpec((nb, ts, d), seq_tile),
                   pl.BlockSpec((nb, ts * nrow, LANES), seq_tile),
                   pl.BlockSpec((nb, ts, LANES), seq_tile),
                   pl.BlockSpec((nb, SUBLANES, ts), lambda i, j: (i, 0, j)),
                   pl.BlockSpec((LANES, ts), lambda i, j: (0, 0))],
        out_shape=[jax.ShapeDtypeStruct((b, s, d), F32),
                   jax.ShapeDtypeStruct((b, s * nrow, LANES), F32),
                   jax.ShapeDtypeStruct((b, s, LANES), F32),
                   jax.ShapeDtypeStruct((b, SUBLANES, s), F32),
                   jax.ShapeDtypeStruct((LANES, ts), F32)],
        scratch_shapes=[pltpu.VMEM((nb, ssd_w // LANES, SSD_STATE, LANES), F32),
                        pltpu.VMEM((nb, conv_dim // LANES, (ts // SUBLANES + 2 * CONV_PAD) * CONV_PITCH, LANES),
                                   F32),
                        pltpu.VMEM((nb, POOL_HALO + ts, pool_w), F32),
                        pltpu.VMEM((nb, ts, conv_dim), F32),
                        pltpu.VMEM((nb, ts, d), BF16)],
        compiler_params=pltpu.CompilerParams(dimension_semantics=("arbitrary", "arbitrary"),
                                             vmem_limit_bytes=VMEM_LIMIT_BYTES),
        name="mixer",
    )(x, mod, win, wp, ps, cw, cb, dtb, alog, dsk, nw, wout, wrt, br, rep, tril3)
    return x1, h2.reshape(b * s * nrow, LANES), route.reshape(b * s, LANES), route_t, counts


def _rank_kernel(route_ref, start_ref, dest_ref, carry_ref):
    ta = route_ref.shape[2]

    @pl.when(pl.program_id(0) == 0)
    def _():
        carry_ref[...] = start_ref[...]

    route_t = route_ref[0]
    e_idx = lax.broadcasted_iota(jnp.int32, (LANES, ta), 0).astype(F32)
    ohs = [jnp.where(e_idx == route_t[k:k + 1, :], 1.0, 0.0) for k in range(TOP_K)]
    oh = (ohs[0] + ohs[1]).astype(BF16)
    earlier = (lax.broadcasted_iota(jnp.int32, (ta, ta), 0)
               < lax.broadcasted_iota(jnp.int32, (ta, ta), 1))
    before = _dot(oh, jnp.where(earlier, 1.0, 0.0).astype(BF16))
    carry = carry_ref[...]
    base = jnp.concatenate([carry] * (ta // LANES), axis=1) + before
    for k in range(TOP_K):
        dest_ref[0, k:k + 1, :] = jnp.sum(ohs[k] * base, axis=0, keepdims=True).astype(jnp.int32)
    carry_ref[...] = carry + _dot(oh, jnp.ones((ta, LANES), BF16))


def _rank(route_t, start_rep):
    b, _, s = route_t.shape
    spt = s // ROW_COPY_TILE
    t = b * s
    return pl.pallas_call(
        _rank_kernel,
        grid=(t // ROW_COPY_TILE,),
        in_specs=[pl.BlockSpec((1, SUBLANES, ROW_COPY_TILE), lambda i: (i // spt, 0, i % spt)),
                  pl.BlockSpec((LANES, LANES), lambda i: (0, 0))],
        out_specs=pl.BlockSpec((1, TOP_K, ROW_COPY_TILE), lambda i: (i, 0, 0)),
        out_shape=jax.ShapeDtypeStruct((t // ROW_COPY_TILE, TOP_K, ROW_COPY_TILE), jnp.int32),
        scratch_shapes=[pltpu.VMEM((LANES, LANES), F32)],
        compiler_params=pltpu.CompilerParams(dimension_semantics=("arbitrary",)),
        name="rank",
    )(route_t, start_rep)


def _dispatch_kernel(dest_ref, h2_ref, xs_hbm, sem, *, nrow):
    tile_rows = h2_ref.shape[0]

    part_tokens = dest_ref.shape[2]
    for part in range(dest_ref.shape[0]):
        for tk in range(part_tokens):
            for k in range(TOP_K):
                pltpu.make_async_copy(_slab_of_row(h2_ref, part * part_tokens + tk, nrow),
                                      _slab_of_row(xs_hbm, dest_ref[part, k, tk], nrow),
                                      sem).start(priority=(tk * TOP_K + k) % DMA_THREADS)

    for _ in range(TOP_K):
        pltpu.make_async_copy(h2_ref, xs_hbm.at[pl.ds(0, tile_rows)], sem).wait()


def _dispatch(dest, h2, nrow):
    t = h2.shape[0] // nrow
    return pl.pallas_call(
        functools.partial(_dispatch_kernel, nrow=nrow),
        grid=(t // (DISPATCH_PARTS * ROW_COPY_TILE),),
        in_specs=[pl.BlockSpec((DISPATCH_PARTS, TOP_K, ROW_COPY_TILE), lambda i: (i, 0, 0),
                               memory_space=pltpu.SMEM),
                  pl.BlockSpec((DISPATCH_PARTS * ROW_COPY_TILE * nrow, LANES), lambda i: (i, 0))],
        out_specs=pl.BlockSpec(memory_space=pl.ANY),
        out_shape=jax.ShapeDtypeStruct((t * TOP_K * nrow, LANES), F32),
        scratch_shapes=[pltpu.SemaphoreType.DMA(())],
        compiler_params=pltpu.CompilerParams(dimension_semantics=("arbitrary",),
                                             has_side_effects=True),
        name="dispatch",
    )(dest, h2)


def _expert_kernel(blk_ref, exp_ref, lo_ref, hi_ref, first_ref, new_ref, slot_ref, next_ref,
                   xs_ref, w13_hbm, w2_hbm, ys_ref, w13_f32, w2_f32, w13_bf, w2_bf, sems, *, nrow):
    i = pl.program_id(0)
    rows = xs_ref.shape[0] // nrow
    lo = lo_ref[i]
    hi = hi_ref[i]

    sub = rows // EXPERT_CHAINS

    def weight_copies(expert, slot):
        return (pltpu.make_async_copy(w13_hbm.at[expert], w13_f32.at[slot], sems.at[slot]),
                pltpu.make_async_copy(w2_hbm.at[expert], w2_f32.at[slot], sems.at[slot]))

    @pl.when(i == 0)
    def _():
        for cp in weight_copies(exp_ref[0], slot_ref[0]):
            cp.start()

    @pl.when(new_ref[i] == 1)
    def _():
        slot = slot_ref[i]
        for cp in weight_copies(exp_ref[i], slot):
            cp.wait()

        @pl.when(next_ref[i] >= 0)
        def _():
            for cp in weight_copies(next_ref[i], 1 - slot):
                cp.start()

        w13_bf[...] = w13_f32[slot].astype(BF16)
        w2_bf[...] = w2_f32[slot].astype(BF16)

    def chain(r0):
        xb = jnp.concatenate([xs_ref[pl.ds(r0 * nrow + c, sub, stride=nrow), :].astype(BF16)
                              for c in range(nrow)], axis=1)
        yield
        hu = _dot(xb, w13_bf[...])
        yield
        f = hu.shape[1] // 2
        act = (_silu(hu[:, :f]) * hu[:, f:]).astype(BF16)
        yield
        y = _dot(act, w2_bf[...])
        yield
        ridx = r0 + lax.broadcasted_iota(jnp.int32, (sub, 1), 0)
        take = (ridx >= lo_eff) & (ridx < hi_eff)
        for c in range(nrow):
            sl = pl.ds(r0 * nrow + c, sub, stride=nrow)
            ys_ref[sl, :] = jnp.where(take, y[:, c * LANES:(c + 1) * LANES], ys_ref[sl, :])

    is_first = first_ref[i] == 1
    lo_eff = jnp.where(is_first, 0, lo)
    hi_eff = jnp.where(is_first, rows, hi)

    @pl.when(hi > lo)
    def _():
        _run_skewed([chain(k * sub) for k in range(EXPERT_CHAINS)])


def _experts(items, xs, w13, w2, nrow):
    a = xs.shape[0] // nrow
    n_items = items[0].shape[0]
    d = w13.shape[1]
    ff2 = w13.shape[2]
    grid_spec = pltpu.PrefetchScalarGridSpec(
        num_scalar_prefetch=len(items),
        grid=(n_items,),
        in_specs=[pl.BlockSpec((EXPERT_BLOCK * nrow, LANES), lambda i, b, *_: (b[i], 0)),
                  pl.BlockSpec(memory_space=pl.ANY),
                  pl.BlockSpec(memory_space=pl.ANY)],
        out_specs=pl.BlockSpec((EXPERT_BLOCK * nrow, LANES), lambda i, b, *_: (b[i], 0)),
        scratch_shapes=[pltpu.VMEM((2, d, ff2), F32), pltpu.VMEM((2, ff2 // 2, d), F32),
                        pltpu.VMEM((d, ff2), BF16), pltpu.VMEM((ff2 // 2, d), BF16),
                        pltpu.SemaphoreType.DMA((2,))],
    )
    return pl.pallas_call(
        functools.partial(_expert_kernel, nrow=nrow),
        grid_spec=grid_spec,
        out_shape=jax.ShapeDtypeStruct((a * nrow, LANES), F32),
        compiler_params=pltpu.CompilerParams(dimension_semantics=("arbitrary",),
                                             vmem_limit_bytes=VMEM_LIMIT_BYTES),
        name="experts",
    )(*items, xs, w13, w2)


def _final_kernel(dest_ref, dnext_ref, x1_ref, route_ref, mod_ref, fw_ref, ys_hbm, o_ref,
                  buf_even, buf_odd, sems):
    tf = x1_ref.shape[1]
    nrow = buf_even.shape[1] // tf
    step = pl.program_id(0)
    last = step + 1 == pl.num_programs(0)

    def start_gathers(d_ref, buf, sem, t0, t1):
        for tk in range(t0, t1):
            for k in range(TOP_K):
                pltpu.make_async_copy(_slab_of_row(ys_hbm, d_ref[0, k, tk], nrow),
                                      _slab_of_row(buf.at[k], tk, nrow),
                                      sem).start(priority=(tk * TOP_K + k) % DMA_THREADS)

    def wait_gathers(buf, sem):
        for k in range(TOP_K):
            pltpu.make_async_copy(ys_hbm.at[pl.ds(0, tf * nrow)], buf.at[k], sem).wait()

    @pl.when(step == 0)
    def _():
        start_gathers(dest_ref, buf_even, sems.at[0], 0, tf)

    def body(cur, cur_sem, nxt, nxt_sem):
        wait_gathers(cur, cur_sem)
        g2 = mod_ref[0][5:6]
        for r0 in range(0, tf, NORM_ROWS):
            start_gathers(dnext_ref, nxt, nxt_sem, r0, r0 + NORM_ROWS)
            rows = slice(r0, r0 + NORM_ROWS)
            gates = route_ref[rows, :]
            y = jnp.concatenate(
                [cur[0, pl.ds(r0 * nrow + c, NORM_ROWS, stride=nrow), :] * gates[:, 2:3]
                 + cur[1, pl.ds(r0 * nrow + c, NORM_ROWS, stride=nrow), :] * gates[:, 3:4]
                 for c in range(nrow)], axis=1)
            o_ref[0, rows, :] = _rms(x1_ref[0, rows, :] + g2 * y) * fw_ref[...]

        @pl.when(last)
        def _():
            wait_gathers(nxt, nxt_sem)

    @pl.when(lax.rem(step, 2) == 0)
    def _():
        body(buf_even, sems.at[0], buf_odd, sems.at[1])

    @pl.when(lax.rem(step, 2) == 1)
    def _():
        body(buf_odd, sems.at[1], buf_even, sems.at[0])


def _final(dest, x1, route, mod, fw, ys):
    b, s, d = x1.shape
    tf = ROW_COPY_TILE
    nrow = d // LANES
    spt = s // tf
    n_steps = b * spt
    return pl.pallas_call(
        _final_kernel,
        grid=(n_steps,),
        in_specs=[pl.BlockSpec((1, TOP_K, tf), lambda i: (i, 0, 0), memory_space=pltpu.SMEM),
                  pl.BlockSpec((1, TOP_K, tf), lambda i: (jnp.minimum(i + 1, n_steps - 1), 0, 0),
                               memory_space=pltpu.SMEM),
                  pl.BlockSpec((1, tf, d), lambda i: (i // spt, i % spt, 0)),
                  pl.BlockSpec((tf, LANES), lambda i: (i, 0)),
                  pl.BlockSpec((1, 6, d), lambda i: (i // spt, 0, 0)),
                  pl.BlockSpec((1, d), lambda i: (0, 0)),
                  pl.BlockSpec(memory_space=pl.ANY)],
        out_specs=pl.BlockSpec((1, tf, d), lambda i: (i // spt, i % spt, 0)),
        out_shape=jax.ShapeDtypeStruct((b, s, d), F32),
        scratch_shapes=[pltpu.VMEM((TOP_K, tf * nrow, LANES), F32),
                        pltpu.VMEM((TOP_K, tf * nrow, LANES), F32),
                        pltpu.SemaphoreType.DMA((2,))],
        compiler_params=pltpu.CompilerParams(dimension_semantics=("arbitrary",),
                                             vmem_limit_bytes=VMEM_LIMIT_BYTES),
        name="final",
    )(dest, dest, x1, route, mod, fw, ys)


def _pad_lanes(v, fill=0.0):
    n = v.shape[-1]
    return jnp.pad(v, [(0, 0)] * (v.ndim - 1) + [(0, LANES - n)], constant_values=fill)


def _work_items(counts, n_blocks):
    n_items = n_blocks + N_EXPERTS - 1
    ends = jnp.cumsum(counts)
    starts = ends - counts
    first_blk = starts // EXPERT_BLOCK
    last_blk = jnp.maximum(ends - 1, starts) // EXPERT_BLOCK
    n_e = jnp.where(counts > 0, last_blk - first_blk + 1, 0)
    item_end = jnp.cumsum(n_e)
    item_start = item_end - n_e
    ids = jnp.arange(n_items, dtype=jnp.int32)
    total = item_end[-1]
    ids_c = jnp.minimum(ids, total - 1)
    e = jnp.sum((item_end[None, :] <= ids_c[:, None]).astype(jnp.int32), axis=1)
    onehot = (e[:, None] == jnp.arange(N_EXPERTS, dtype=jnp.int32)[None, :]).astype(jnp.int32)
    pick = lambda v: jnp.sum(onehot * v[None, :], axis=1)
    blk = pick(first_blk) + ids_c - pick(item_start)
    lo = jnp.clip(pick(starts) - blk * EXPERT_BLOCK, 0, EXPERT_BLOCK)
    hi = jnp.clip(pick(ends) - blk * EXPERT_BLOCK, 0, EXPERT_BLOCK)
    hi = jnp.where(ids < total, hi, lo)
    prev_blk = jnp.concatenate([jnp.full((1,), -1, jnp.int32), blk[:-1]])
    first = (blk != prev_blk).astype(jnp.int32)
    prev_e = jnp.concatenate([jnp.full((1,), -1, jnp.int32), e[:-1]])
    new_expert = (e != prev_e).astype(jnp.int32)
    has_rows = (counts > 0).astype(jnp.int32)
    ordinal = jnp.cumsum(has_rows) - has_rows
    later = (jnp.arange(N_EXPERTS)[None, :] > jnp.arange(N_EXPERTS)[:, None]) & (counts[None, :] > 0)
    next_of = jnp.where(jnp.any(later, axis=1), jnp.argmax(later, axis=1), -1).astype(jnp.int32)
    return blk, e, lo, hi, first, new_expert, pick(ordinal) % 2, pick(next_of)


def kernel(x, c, w_ada, b_ada, w_in, w_pool, pool_scale, conv_w, conv_b, dt_bias, a_log, d_skip,
           ssd_norm_w, w_out, w_group, b_group, w_router, b_router, w13, w2, final_norm_w):
    b, s, d = x.shape
    depth = w_ada.shape[0]
    t = b * s
    pool_w = w_pool.shape[1] * w_pool.shape[2]
    ssd_w = SSD_HEADS * SSD_HEAD_DIM
    conv_dim = conv_w.shape[2]
    assert s % SEQ_TILE == 0 and SEQ_TILE % SSD_CHUNK == 0 and b % MIXER_SEQS == 0
    assert s % ROW_COPY_TILE == 0 and ROW_COPY_TILE % LANES == 0
    assert t % (DISPATCH_PARTS * ROW_COPY_TILE) == 0 and SEQ_TILE % NORM_ROWS == 0
    assert (t * TOP_K) % EXPERT_BLOCK == 0 and d % LANES == 0
    assert pool_w == 2 * LANES and len(POOL_WINDOWS) * POOL_GROUP_DIM == pool_w
    assert N_EXPERT_GROUPS + N_EXPERTS <= LANES and SSD_HEADS <= LANES
    assert depth == 1, "the final RMSNorm is fused into the last layer's combine step"

    for layer in range(depth):
        mod = _ada(c, w_ada[layer], b_ada[layer]).reshape(b, 6, d)

        o_dt = pool_w + ssd_w + conv_dim
        win = jnp.pad(w_in[layer], ((0, 0), (0, o_dt + LANES - w_in.shape[2]))).astype(BF16)
        wp = jnp.zeros((pool_w, pool_w), F32)
        for g in range(len(POOL_WINDOWS)):
            sl = slice(g * POOL_GROUP_DIM, (g + 1) * POOL_GROUP_DIM)
            wp = wp.at[sl, sl].set(w_pool[layer, g])
        wp = wp.astype(BF16)
        wrt = _pad_lanes(jnp.concatenate([w_group[layer], w_router[layer]], axis=1)).T.astype(BF16)
        b_route = _pad_lanes(jnp.concatenate([b_group[layer], b_router[layer]])[None, :])
        br = jnp.broadcast_to(b_route.reshape(LANES, 1), (LANES, LANES))

        x1, h2, route, route_t, counts_part = _mixer(
            x, mod, win, wp, pool_scale[layer][None, :], conv_w[layer], conv_b[layer][None, :],
            _pad_lanes(dt_bias[layer][None, :]), _pad_lanes(a_log[layer][None, :], fill=NEG_BIG),
            jnp.repeat(d_skip[layer], SSD_HEAD_DIM)[None, :], ssd_norm_w[layer][None, :],
            w_out[layer].astype(BF16), wrt, br, pool_w=pool_w, ssd_w=ssd_w)

        counts_f = jnp.sum(counts_part, axis=1, keepdims=True)
        start_rep = jnp.broadcast_to(jnp.cumsum(counts_f, axis=0) - counts_f, (LANES, LANES))
        dest = _rank(route_t, start_rep)
        counts = counts_f[:N_EXPERTS, 0].astype(jnp.int32)
        items = _work_items(counts, (t * TOP_K) // EXPERT_BLOCK)

        nrow = d // LANES
        xs = _dispatch(dest, h2, nrow)
        ys = _experts(items, xs, w13[layer], w2[layer], nrow)
        x = _final(dest, x1, route, mod, final_norm_w[None, :], ys)
    return x
```
